```python
import jax
import jax.numpy as jnp
from jax import lax
import numpy as np

D_MODEL = 2048
BATCH = 1
SEQ = 8192
DEPTH = 2

N_BRANCHES = 3
DEEPNORM_ALPHA = (2 * DEPTH) ** 0.25
DEEPNORM_BETA = (8 * DEPTH) ** -0.25
LN_EPS = 1e-5
RMS_EPS = 1e-6
Q_BLOCK = 128

RNN_WIDTH = D_MODEL // 2
RNN_BLOCKS = 8
RNN_BLOCK_W = RNN_WIDTH // RNN_BLOCKS
CONV_WIDTH = 4
RG_LRU_C = 8.0

FOX_HEADS = 8
FOX_HEAD_DIM = 128
FOX_WIDTH = FOX_HEADS * FOX_HEAD_DIM
FORGET_BIAS_INIT = 4.0

MLA_HEADS = 8
MLA_Q_RANK = 512
MLA_KV_RANK = 512
MLA_NOPE_DIM = 128
MLA_ROPE_DIM = 64
MLA_V_DIM = 128
MLA_QK_DIM = MLA_NOPE_DIM + MLA_ROPE_DIM
ROPE_THETA = 10000.0

N_EXPERTS = 32
TOP_K = 4
EXPERT_FF = D_MODEL
SWIGLU_ALPHA = 1.702
SWIGLU_LIMIT = 7.0
ROW_BLOCK = 256

_IN_SIZES = (N_BRANCHES * D_MODEL,
             RNN_WIDTH, RNN_WIDTH,
             FOX_WIDTH, FOX_WIDTH, FOX_WIDTH,
             FOX_HEADS,
             MLA_Q_RANK, MLA_KV_RANK,
             MLA_ROPE_DIM)
IN_COLS = sum(_IN_SIZES)
_IN_SPLITS = tuple(int(v) for v in np.cumsum(_IN_SIZES)[:-1])

kernel_name = 'hybrid_rglru_fox_mla_moe_deepnorm'


def layer_norm(x, g, b):
    xf = x.astype(jnp.float32)
    mu = jnp.mean(xf, axis=-1, keepdims=True)
    var = jnp.mean(jnp.square(xf - mu), axis=-1, keepdims=True)
    y = (xf - mu) * lax.rsqrt(var + LN_EPS) * g.astype(jnp.float32) + b.astype(jnp.float32)
    return y.astype(x.dtype)


def rms_norm(x, g):
    xf = x.astype(jnp.float32)
    y = xf * lax.rsqrt(jnp.mean(jnp.square(xf), axis=-1, keepdims=True) + RMS_EPS)
    return (y * g.astype(jnp.float32)).astype(x.dtype)


def rope_cos_sin(positions):
    half = MLA_ROPE_DIM // 2
    inv_freq = ROPE_THETA ** (-jnp.arange(half, dtype=jnp.float32) / half)
    ang = positions.astype(jnp.float32)[:, :, None] * inv_freq
    return jnp.cos(ang)[:, :, None, :], jnp.sin(ang)[:, :, None, :]


def apply_rope(x, cos, sin):
    half = x.shape[-1] // 2
    xf = x.astype(jnp.float32)
    x1, x2 = xf[..., :half], xf[..., half:]
    out = jnp.concatenate([x1 * cos - x2 * sin, x2 * cos + x1 * sin], axis=-1)
    return out.astype(x.dtype)


def causal_block_attention(q, k, v, cum_logf=None):
    b, s, h, dk = q.shape
    n_blk = s // Q_BLOCK
    scale = dk ** -0.5
    q_blocks = jnp.swapaxes(q.reshape(b, n_blk, Q_BLOCK, h, dk), 0, 1)
    k_pos = jnp.arange(s)

    def one_block(args):
        blk, q_blk = args
        q_pos = blk * Q_BLOCK + jnp.arange(Q_BLOCK)
        logits = jnp.einsum('bqhd,bkhd->bhqk', q_blk, k,
                            preferred_element_type=jnp.float32) * scale
        if cum_logf is not None:
            cum_q = lax.dynamic_slice_in_dim(cum_logf, blk * Q_BLOCK, Q_BLOCK, axis=2)
            logits = logits + (cum_q[..., :, None] - cum_logf[..., None, :])
        logits = jnp.where(k_pos[None, :] <= q_pos[:, None], logits, -jnp.inf)
        probs = jax.nn.softmax(logits, axis=-1)
        return jnp.einsum('bhqk,bkhd->bqhd', probs.astype(v.dtype), v)

    out = lax.map(one_block, (jnp.arange(n_blk), q_blocks))
    return jnp.swapaxes(out, 0, 1).reshape(b, s, h, v.shape[-1])


def rglru_mixer(u_gate, u_rec, conv_w, conv_b, w_rg, b_rg, w_ig, b_ig, lam):
    y = jax.nn.gelu(u_gate)
    xc = lax.conv_general_dilated(
        u_rec, conv_w[:, None, :], window_strides=(1,),
        padding=[(CONV_WIDTH - 1, 0)],
        dimension_numbers=('NWC', 'WIO', 'NWC'),
        feature_group_count=RNN_WIDTH) + conv_b
    b, s, _ = xc.shape
    xb = xc.reshape(b, s, RNN_BLOCKS, RNN_BLOCK_W)
    r = jax.nn.sigmoid(jnp.einsum('bsnd,nde->bsne', xb, w_rg) + b_rg).reshape(b, s, RNN_WIDTH)
    i = jax.nn.sigmoid(jnp.einsum('bsnd,nde->bsne', xb, w_ig) + b_ig).reshape(b, s, RNN_WIDTH)
    log_a = -RG_LRU_C * jax.nn.softplus(-lam.astype(jnp.float32)) * r.astype(jnp.float32)
    a = jnp.exp(log_a)
    gated_x = jnp.sqrt(-jnp.expm1(2.0 * log_a)) * (i * xc).astype(jnp.float32)

    def combine(left, right):
        return left[0] * right[0], right[0] * left[1] + right[1]

    _, h = lax.associative_scan(combine, (a, gated_x), axis=1)
    return h.astype(y.dtype) * y


def fox_mixer(fq, fk, fv, f_logit, b_f):
    b, s, _ = fq.shape
    q = fq.reshape(b, s, FOX_HEADS, FOX_HEAD_DIM)
    k = fk.reshape(b, s, FOX_HEADS, FOX_HEAD_DIM)
    v = fv.reshape(b, s, FOX_HEADS, FOX_HEAD_DIM)
    log_f = jax.nn.log_sigmoid(f_logit.astype(jnp.float32) + b_f.astype(jnp.float32))
    cum = jnp.swapaxes(jnp.cumsum(log_f, axis=1), 1, 2)
    o = causal_block_attention(q, k, v, cum)
    return o.reshape(b, s, FOX_WIDTH)


def mla_mixer(c_q, c_kv, k_rope, cos, sin, g_cq, g_ckv, w_uq, w_ukv):
    b, s, _ = c_q.shape
    q = jnp.einsum('bsr,rf->bsf', rms_norm(c_q, g_cq), w_uq).reshape(b, s, MLA_HEADS, MLA_QK_DIM)
    kv = jnp.einsum('bsr,rf->bsf', rms_norm(c_kv, g_ckv), w_ukv).reshape(
        b, s, MLA_HEADS, MLA_NOPE_DIM + MLA_V_DIM)
    q_nope, q_rot = q[..., :MLA_NOPE_DIM], q[..., MLA_NOPE_DIM:]
    k_nope, v = kv[..., :MLA_NOPE_DIM], kv[..., MLA_NOPE_DIM:]
    q_rot = apply_rope(q_rot, cos, sin)
    k_rot = apply_rope(k_rope[:, :, None, :], cos, sin)
    q_full = jnp.concatenate([q_nope, q_rot], axis=-1)
    k_full = jnp.concatenate(
        [k_nope, jnp.broadcast_to(k_rot, (b, s, MLA_HEADS, MLA_ROPE_DIM))], axis=-1)
    o = causal_block_attention(q_full, k_full, v)
    return o.reshape(b, s, MLA_HEADS * MLA_V_DIM)


def moe_ffn(x, w_router, b_router, w_up, b_up, w_down, b_down):
    b, s, d = x.shape
    t = b * s
    xt = x.reshape(t, d)
    logits = jnp.einsum('td,de->te', xt, w_router,
                        preferred_element_type=jnp.float32) + b_router.astype(jnp.float32)
    top_val, top_idx = lax.top_k(logits, TOP_K)
    gate = jax.nn.softmax(top_val, axis=-1)

    n_assign = t * TOP_K
    n_blocks = -(-n_assign // ROW_BLOCK) + N_EXPERTS
    e_flat = top_idx.reshape(-1)
    order = jnp.argsort(e_flat)
    e_sorted = e_flat[order]
    counts = jnp.bincount(e_flat, length=N_EXPERTS)
    padded = (counts + ROW_BLOCK - 1) // ROW_BLOCK * ROW_BLOCK
    start = jnp.cumsum(counts) - counts
    pend = jnp.cumsum(padded)
    pstart = pend - padded
    dest = pstart[e_sorted] + jnp.arange(n_assign) - start[e_sorted]
    row_tok = jnp.full((n_blocks * ROW_BLOCK,), t, jnp.int32).at[dest].set(
        (order // TOP_K).astype(jnp.int32))
    row_gate = jnp.zeros((n_blocks * ROW_BLOCK,), jnp.float32).at[dest].set(
        gate.reshape(-1)[order])
    block_expert = jnp.minimum(
        jnp.searchsorted(pend, jnp.arange(n_blocks) * ROW_BLOCK, side='right'), N_EXPERTS - 1)

    x_pad = jnp.concatenate([xt, jnp.zeros((1, d), xt.dtype)], axis=0)
    rows = x_pad[row_tok].reshape(n_blocks, ROW_BLOCK, d)

    def expert_block(args):
        xb, e = args
        h = jnp.einsum('rd,df->rf', xb, w_up[e]) + b_up[e]
        h_glu = jnp.minimum(h[:, :EXPERT_FF], SWIGLU_LIMIT)
        h_lin = jnp.clip(h[:, EXPERT_FF:], -SWIGLU_LIMIT, SWIGLU_LIMIT)
        act = h_glu * jax.nn.sigmoid(SWIGLU_ALPHA * h_glu) * (h_lin + 1.0)
        return jnp.einsum('rf,fd->rd', act, w_down[e]) + b_down[e]

    y_rows = lax.map(expert_block, (rows, block_expert)).reshape(n_blocks * ROW_BLOCK, d)
    contrib = y_rows * row_gate[:, None].astype(y_rows.dtype)
    y = jax.ops.segment_sum(contrib, row_tok, num_segments=t + 1)[:t]
    return y.reshape(b, s, d)


def setup_inputs(seed: int = 0) -> dict:
    key = jax.random.key(seed)
    ks = jax.random.split(key, 32)
    f32 = jnp.float32

    def nrm(k, shape, scale):
        return jax.random.normal(k, shape, f32) * scale

    a0 = jnp.sqrt(jax.random.uniform(ks[9], (DEPTH, RNN_WIDTH), f32,
                                     minval=0.9 ** 2, maxval=0.999 ** 2))
    base = a0 ** (1.0 / RG_LRU_C)
    lam = jnp.log(base) - jnp.log1p(-base)

    return {
        'x': nrm(ks[0], (BATCH, SEQ, D_MODEL), 1.0),
        'positions': jnp.broadcast_to(jnp.arange(SEQ, dtype=jnp.int32)[None, :], (BATCH, SEQ)),
        'w_in': nrm(ks[1], (DEPTH, D_MODEL, IN_COLS), D_MODEL ** -0.5),
        'b_merge': nrm(ks[2], (DEPTH, N_BRANCHES, D_MODEL), 0.02),
        'b_forget': FORGET_BIAS_INIT + nrm(ks[3], (DEPTH, FOX_HEADS), 0.5),
        'conv_w': nrm(ks[4], (DEPTH, CONV_WIDTH, RNN_WIDTH), CONV_WIDTH ** -0.5),
        'conv_b': nrm(ks[5], (DEPTH, RNN_WIDTH), 0.02),
        'w_rec_gate': nrm(ks[6], (DEPTH, RNN_BLOCKS, RNN_BLOCK_W, RNN_BLOCK_W), RNN_BLOCK_W ** -0.5),
        'b_rec_gate': nrm(ks[7], (DEPTH, RNN_BLOCKS, RNN_BLOCK_W), 0.02),
        'w_inp_gate': nrm(ks[8], (DEPTH, RNN_BLOCKS, RNN_BLOCK_W, RNN_BLOCK_W), RNN_BLOCK_W ** -0.5),
        'b_inp_gate': nrm(ks[10], (DEPTH, RNN_BLOCKS, RNN_BLOCK_W), 0.02),
        'lru_lambda': lam,
        'g_cq': 1.0 + nrm(ks[11], (DEPTH, MLA_Q_RANK), 0.02),
        'g_ckv': 1.0 + nrm(ks[12], (DEPTH, MLA_KV_RANK), 0.02),
        'w_uq': nrm(ks[13], (DEPTH, MLA_Q_RANK, MLA_HEADS * MLA_QK_DIM), MLA_Q_RANK ** -0.5),
        'w_ukv': nrm(ks[14], (DEPTH, MLA_KV_RANK, MLA_HEADS * (MLA_NOPE_DIM + MLA_V_DIM)),
                     MLA_KV_RANK ** -0.5),
        'w_proj_lru': nrm(ks[15], (DEPTH, RNN_WIDTH, D_MODEL), RNN_WIDTH ** -0.5),
        'w_proj_fox': nrm(ks[16], (DEPTH, FOX_WIDTH, D_MODEL), FOX_WIDTH ** -0.5),
        'w_proj_mla': nrm(ks[17], (DEPTH, MLA_HEADS * MLA_V_DIM, D_MODEL),
                          (MLA_HEADS * MLA_V_DIM) ** -0.5),
        'w_out': nrm(ks[18], (DEPTH, D_MODEL, D_MODEL), D_MODEL ** -0.5 * DEEPNORM_BETA),
        'ln1_g': 1.0 + nrm(ks[19], (DEPTH, D_MODEL), 0.02),
        'ln1_b': nrm(ks[20], (DEPTH, D_MODEL), 0.02),
        'w_router': nrm(ks[21], (DEPTH, D_MODEL, N_EXPERTS), D_MODEL ** -0.5),
        'b_router': nrm(ks[22], (DEPTH, N_EXPERTS), 0.01),
        'w_up': nrm(ks[23], (DEPTH, N_EXPERTS, D_MODEL, 2 * EXPERT_FF), D_MODEL ** -0.5),
        'b_up': nrm(ks[24], (DEPTH, N_EXPERTS, 2 * EXPERT_FF), 0.02),
        'w_down': nrm(ks[25], (DEPTH, N_EXPERTS, EXPERT_FF, D_MODEL),
                      EXPERT_FF ** -0.5 * DEEPNORM_BETA),
        'b_down': nrm(ks[26], (DEPTH, N_EXPERTS, D_MODEL), 0.02),
        'ln2_g': 1.0 + nrm(ks[27], (DEPTH, D_MODEL), 0.02),
        'ln2_b': nrm(ks[28], (DEPTH, D_MODEL), 0.02),
    }


def reference(x, positions, w_in, b_merge, b_forget, conv_w, conv_b, w_rec_gate, b_rec_gate,
              w_inp_gate, b_inp_gate, lru_lambda, g_cq, g_ckv, w_uq, w_ukv, w_proj_lru,
              w_proj_fox, w_proj_mla, w_out, ln1_g, ln1_b, w_router, b_router, w_up, b_up,
              w_down, b_down, ln2_g, ln2_b):
    b, s, _ = x.shape
    cos, sin = rope_cos_sin(positions)
    for l in range(DEPTH):
        u = jnp.einsum('bsd,dp->bsp', x, w_in[l])
        (g_logit, lru_y, lru_x, fq, fk, fv, f_logit,
         c_q, c_kv, k_rope) = jnp.split(u, _IN_SPLITS, axis=-1)
        gates = jax.nn.sigmoid(g_logit.reshape(b, s, N_BRANCHES, D_MODEL) + b_merge[l])
        y_a = jnp.einsum('bsr,rd->bsd',
                         rglru_mixer(lru_y, lru_x, conv_w[l], conv_b[l], w_rec_gate[l],
                                     b_rec_gate[l], w_inp_gate[l], b_inp_gate[l],
                                     lru_lambda[l]),
                         w_proj_lru[l])
        y_b = jnp.einsum('bsr,rd->bsd', fox_mixer(fq, fk, fv, f_logit, b_forget[l]),
                         w_proj_fox[l])
        y_c = jnp.einsum('bsr,rd->bsd',
                         mla_mixer(c_q, c_kv, k_rope, cos, sin, g_cq[l], g_ckv[l],
                                   w_uq[l], w_ukv[l]),
                         w_proj_mla[l])
        mixed = gates[:, :, 0] * y_a + gates[:, :, 1] * y_b + gates[:, :, 2] * y_c
        mix_out = jnp.einsum('bsd,de->bse', mixed, w_out[l])
        x = layer_norm(DEEPNORM_ALPHA * x + mix_out, ln1_g[l], ln1_b[l])
        ffn_out = moe_ffn(x, w_router[l], b_router[l], w_up[l], b_up[l], w_down[l], b_down[l])
        x = layer_norm(DEEPNORM_ALPHA * x + ffn_out, ln2_g[l], ln2_b[l])
    return x
```

```python
import functools

import jax
import jax.numpy as jnp
import numpy as np
from jax import lax
from jax.experimental import pallas as pl
from jax.experimental.pallas import tpu as pltpu

F32 = jnp.float32
BF16 = jnp.bfloat16
I32 = jnp.int32

D_MODEL = 2048
N_BRANCHES = 3
DEEPNORM_ALPHA = (2 * 2) ** 0.25
LN_EPS = 1e-5
RMS_EPS = 1e-6
RNN_WIDTH = 1024
RNN_BLOCKS = 8
RNN_BLOCK_W = RNN_WIDTH // RNN_BLOCKS
CONV_WIDTH = 4
RG_LRU_C = 8.0
HEADS = 8
HEAD_DIM = 128
FOX_WIDTH = HEADS * HEAD_DIM
MLA_RANK = 512
MLA_ROPE_DIM = 64
MLA_QK_DIM = HEAD_DIM + MLA_ROPE_DIM
MLA_PAD_DIM = 256
ROPE_THETA = 10000.0
N_EXPERTS = 32
TOP_K = 4
EXPERT_FF = D_MODEL
SWIGLU_ALPHA = 1.702
SWIGLU_LIMIT = 7.0

LANES = 128
SUBLANES = 8
V7X_VMEM_BYTES = 64 * 1024 * 1024

ROW_BLOCK = 256
ATTN_BLOCK = 512
MASK_VALUE = -1e30


def _params(semantics, vmem_mib):
    assert vmem_mib * 1024 * 1024 < V7X_VMEM_BYTES
    return pltpu.CompilerParams(dimension_semantics=semantics,
                                vmem_limit_bytes=vmem_mib * 1024 * 1024)


def _split3(x):
    hi = x.astype(BF16)
    r = x - hi.astype(F32)
    mid = r.astype(BF16)
    lo = (r - mid.astype(F32)).astype(BF16)
    return hi, mid, lo


def _sigmoid(x):
    return 1.0 / (1.0 + jnp.exp(-x))


def _softplus(z):
    return jnp.maximum(z, 0.0) + jnp.log1p(jnp.exp(-jnp.abs(z)))


def _layer_norm(v, g, b):
    mu = jnp.mean(v, axis=-1, keepdims=True)
    c = v - mu
    var = jnp.mean(c * c, axis=-1, keepdims=True)
    return c * lax.rsqrt(var + LN_EPS) * g + b


def _mm_kernel(x_ref, w_ref, o_ref):
    o_ref[...] = jnp.dot(x_ref[...], w_ref[...],
                         preferred_element_type=F32).astype(o_ref.dtype)


def _matmul(x, w, out_dtype, tm, tn, name):
    m, k = x.shape
    n = w.shape[1]
    tm, tn = min(tm, m), min(tn, n)
    return pl.pallas_call(
        _mm_kernel,
        out_shape=jax.ShapeDtypeStruct((m, n), out_dtype),
        grid=(m // tm, n // tn),
        in_specs=[pl.BlockSpec((tm, k), lambda i, j: (i, 0)),
                  pl.BlockSpec((k, tn), lambda i, j: (0, j))],
        out_specs=pl.BlockSpec((tm, tn), lambda i, j: (i, j)),
        compiler_params=_params(("parallel", "arbitrary"), 48),
        name=name,
    )(x, w)


def _fox_cum_kernel(fl_ref, b_ref, o_ref, *, chunks):
    z = fl_ref[...] + b_ref[...]
    lf = jnp.minimum(z, 0.0) - jnp.log1p(jnp.exp(-jnp.abs(z)))
    rows = lf.shape[0]
    s = lax.broadcasted_iota(I32, (LANES, LANES), 0)
    t = lax.broadcasted_iota(I32, (LANES, LANES), 1)
    tri = (s <= t).astype(BF16)
    incl = None
    for part in _split3(lf):
        d = jnp.dot(part, tri, preferred_element_type=F32)
        incl = d if incl is None else incl + d
    i = lax.broadcasted_iota(I32, (rows, rows), 0)
    j = lax.broadcasted_iota(I32, (rows, rows), 1)
    lower = ((i // chunks == j // chunks) & (j < i)).astype(BF16)
    offs = None
    for part in _split3(incl):
        d = jnp.dot(lower, part, preferred_element_type=F32)
        offs = d if offs is None else offs + d
    o_ref[...] = incl + offs[:, LANES - 1:LANES]


def _fox_cum(f_logit, b_f):
    t = f_logit.shape[0]
    chunks = t // LANES
    fl = f_logit.T.reshape(HEADS * chunks, LANES)
    b = jnp.repeat(b_f.astype(F32), chunks).reshape(HEADS * chunks, 1)
    out = pl.pallas_call(
        functools.partial(_fox_cum_kernel, chunks=chunks),
        out_shape=jax.ShapeDtypeStruct((HEADS * chunks, LANES), F32),
        name="fox_cum",
    )(fl, b)
    return out.reshape(HEADS, t)


def _attn_kernel(qi_ref, ki_ref, *refs, has_bias):
    if has_bias:
        q_ref, k_ref, v_ref, cq_ref, ck_ref, o_ref, m_sc, l_sc, acc_sc = refs
    else:
        q_ref, k_ref, v_ref, o_ref, m_sc, l_sc, acc_sc = refs
    p = pl.program_id(1)
    qi = qi_ref[p]
    ki = ki_ref[p]

    @pl.when(ki == 0)
    def _():
        m_sc[...] = jnp.full(m_sc.shape, MASK_VALUE, F32)
        l_sc[...] = jnp.zeros(l_sc.shape, F32)
        acc_sc[...] = jnp.zeros(acc_sc.shape, F32)

    def logits():
        s = lax.dot_general(q_ref[...], k_ref[...], (((1,), (1,)), ((), ())),
                            preferred_element_type=F32)
        if has_bias:
            s = s + (cq_ref[0] - ck_ref[0])
        return s

    def update(s):
        m_prev = m_sc[...]
        m_new = jnp.maximum(m_prev, jnp.max(s, axis=1, keepdims=True))
        alpha = jnp.exp(m_prev - m_new)
        pr = jnp.exp(s - m_new)
        l_sc[...] = alpha * l_sc[...] + jnp.sum(pr, axis=1, keepdims=True)
        acc_sc[...] = alpha * acc_sc[...] + jnp.dot(
            pr.astype(BF16), v_ref[...], preferred_element_type=F32)
        m_sc[...] = m_new

    @pl.when(ki < qi)
    def _():
        update(logits())

    @pl.when(ki == qi)
    def _():
        s = logits()
        row = lax.broadcasted_iota(I32, s.shape, 0)
        col = lax.broadcasted_iota(I32, s.shape, 1)
        update(jnp.where(col <= row, s, MASK_VALUE))
        o_ref[...] = (acc_sc[...] / l_sc[...]).astype(o_ref.dtype)


def _causal_attention(q, k, v, q_off, k_off, v_off, dqk, cum, name):
    t = q.shape[0]
    blk = min(ATTN_BLOCK, t)
    nq = t // blk
    pairs = [(a, b) for a in range(nq) for b in range(a + 1)]
    qi_tab = jnp.asarray(np.array([a for a, _ in pairs], np.int32))
    ki_tab = jnp.asarray(np.array([b for _, b in pairs], np.int32))
    has_bias = cum is not None
    in_specs = [
        pl.BlockSpec((blk, dqk), lambda h, p, qi, ki: (qi[p], q_off + h)),
        pl.BlockSpec((blk, dqk), lambda h, p, qi, ki: (ki[p], k_off + h)),
        pl.BlockSpec((blk, HEAD_DIM), lambda h, p, qi, ki: (ki[p], v_off + h)),
    ]
    args = [q, k, v]
    if has_bias:
        in_specs += [
            pl.BlockSpec((1, blk, 1), lambda h, p, qi, ki: (h, qi[p], 0)),
            pl.BlockSpec((1, 1, blk), lambda h, p, qi, ki: (h, 0, ki[p])),
        ]
        args += [cum.reshape(HEADS, t, 1), cum.reshape(HEADS, 1, t)]
    return pl.pallas_call(
        functools.partial(_attn_kernel, has_bias=has_bias),
        out_shape=jax.ShapeDtypeStruct((t, HEADS * HEAD_DIM), BF16),
        grid_spec=pltpu.PrefetchScalarGridSpec(
            num_scalar_prefetch=2,
            grid=(HEADS, len(pairs)),
            in_specs=in_specs,
            out_specs=pl.BlockSpec((blk, HEAD_DIM), lambda h, p, qi, ki: (qi[p], h)),
            scratch_shapes=[pltpu.VMEM((blk, 1), F32), pltpu.VMEM((blk, 1), F32),
                            pltpu.VMEM((blk, HEAD_DIM), F32)],
        ),
        compiler_params=_params(("parallel", "arbitrary"), 32),
        name=name,
    )(qi_tab, ki_tab, *args)


def _rope_table_kernel(pos_ref, inv_ref, c_ref, s1_ref, s2_ref):
    half = MLA_ROPE_DIM // 2
    ang = pos_ref[...].astype(F32) * inv_ref[...]
    c = jnp.cos(ang)
    s = jnp.sin(ang)
    lane = lax.broadcasted_iota(I32, ang.shape, 1)
    c_ref[...] = jnp.where(lane < MLA_ROPE_DIM, c, 0.0)
    s1_ref[...] = jnp.where(lane < half, -s, 0.0)
    s2_ref[...] = jnp.where((lane >= half) & (lane < MLA_ROPE_DIM), s, 0.0)


def _rope_tables(positions):
    t = positions.shape[0]
    half = MLA_ROPE_DIM // 2
    inv_freq = ROPE_THETA ** (-jnp.arange(half, dtype=F32) / half)
    inv = jnp.concatenate([inv_freq, inv_freq, jnp.zeros((LANES - 2 * half,), F32)])
    tm = min(1024, t)
    spec = pl.BlockSpec((tm, LANES), lambda i: (i, 0))
    return pl.pallas_call(
        _rope_table_kernel,
        out_shape=[jax.ShapeDtypeStruct((t, LANES), F32)] * 3,
        grid=(t // tm,),
        in_specs=[pl.BlockSpec((tm, 1), lambda i: (i, 0)),
                  pl.BlockSpec((1, LANES), lambda i: (0, 0))],
        out_specs=[spec, spec, spec],
        name="rope_tables",
    )(positions.reshape(t, 1), inv.reshape(1, LANES))


def _rope_group(g, c, s1, s2):
    half = MLA_ROPE_DIM // 2
    return (g * c + pltpu.roll(g, LANES - half, axis=1) * s1
            + pltpu.roll(g, half, axis=1) * s2)


def _mla_prep_kernel(u_ref, gq_ref, gkv_ref, wq_ref, wkv_ref, c_ref, s1_ref, s2_ref,
                     q_ref, k_ref, v_ref):
    def rms(v, g):
        ms = jnp.mean(v * v, axis=-1, keepdims=True)
        return (v * lax.rsqrt(ms + RMS_EPS) * g).astype(BF16)

    c, s1, s2 = c_ref[...], s1_ref[...], s2_ref[...]
    cq = rms(u_ref[:, 0:MLA_RANK], gq_ref[...])
    ckv = rms(u_ref[:, MLA_RANK:2 * MLA_RANK], gkv_ref[...])
    q_pre = jnp.dot(cq, wq_ref[...], preferred_element_type=F32)
    kv = jnp.dot(ckv, wkv_ref[...], preferred_element_type=F32)
    k_rot = _rope_group(u_ref[:, 2 * MLA_RANK:2 * MLA_RANK + LANES], c, s1, s2).astype(BF16)
    for h in range(HEADS):
        lo = h * MLA_PAD_DIM
        mid = lo + HEAD_DIM
        hi = lo + MLA_PAD_DIM
        q_ref[:, lo:mid] = q_pre[:, lo:mid].astype(BF16)
        q_ref[:, mid:hi] = _rope_group(q_pre[:, mid:hi], c, s1, s2).astype(BF16)
        k_ref[:, lo:mid] = kv[:, lo:mid].astype(BF16)
        k_ref[:, mid:hi] = k_rot
        v_ref[:, h * HEAD_DIM:(h + 1) * HEAD_DIM] = kv[:, mid:hi].astype(BF16)


def _mla_prep(u_small, g_cq, g_ckv, w_uq_r, w_ukv, tabs):
    t = u_small.shape[0]
    tm = min(512, t)
    wide = HEADS * MLA_PAD_DIM
    row = lambda w: pl.BlockSpec((tm, w), lambda i: (i, 0))
    const = lambda a: pl.BlockSpec(a.shape, lambda i: (0, 0))
    return pl.pallas_call(
        _mla_prep_kernel,
        out_shape=[jax.ShapeDtypeStruct((t, wide), BF16),
                   jax.ShapeDtypeStruct((t, wide), BF16),
                   jax.ShapeDtypeStruct((t, HEADS * HEAD_DIM), BF16)],
        grid=(t // tm,),
        in_specs=[row(u_small.shape[1]), const(g_cq), const(g_ckv), const(w_uq_r),
                  const(w_ukv), row(LANES), row(LANES), row(LANES)],
        out_specs=[row(wide), row(wide), row(HEADS * HEAD_DIM)],
        compiler_params=_params(("arbitrary",), 48),
        name="mla_prep",
    )(u_small, g_cq, g_ckv, w_uq_r, w_ukv, *tabs)


def _rglru_kernel(y_ref, x_ref, cw_ref, cb_ref, wr_ref, br_ref, wi_ref, bi_ref, lam_ref,
                  o_ref, xprev_sc, h_sc, a_sc, g_sc, hs_sc):
    tt = x_ref.shape[0]

    @pl.when(pl.program_id(0) == 0)
    def _():
        xprev_sc[...] = jnp.zeros(xprev_sc.shape, F32)
        h_sc[...] = jnp.zeros(h_sc.shape, F32)

    x = x_ref[...]
    xext = jnp.concatenate([xprev_sc[...], x], axis=0)
    xprev_sc[...] = x[tt - SUBLANES:, :]
    xc = cb_ref[...]
    for j in range(CONV_WIDTH):
        off = SUBLANES - (CONV_WIDTH - 1) + j
        xc = xc + cw_ref[j:j + 1, :] * xext[off:off + tt, :]

    r_parts, i_parts = [], []
    for n in range(RNN_BLOCKS):
        sl = slice(n * RNN_BLOCK_W, (n + 1) * RNN_BLOCK_W)
        xb = xc[:, sl].astype(BF16)
        r_parts.append(_sigmoid(jnp.dot(xb, wr_ref[n], preferred_element_type=F32)
                                + br_ref[:, sl]))
        i_parts.append(_sigmoid(jnp.dot(xb, wi_ref[n], preferred_element_type=F32)
                                + bi_ref[:, sl]))
    r = jnp.concatenate(r_parts, axis=1)
    gate_i = jnp.concatenate(i_parts, axis=1)
    log_a = (-RG_LRU_C * _softplus(-lam_ref[...])) * r
    a = jnp.exp(log_a)
    a_sc[...] = a
    g_sc[...] = jnp.sqrt(-jnp.tanh(log_a) * (a * a + 1.0)) * (gate_i * xc)

    sub = lax.broadcasted_iota(I32, (SUBLANES, RNN_WIDTH), 0)

    def tile_scan(k, h_in):
        rows = pl.ds(pl.multiple_of(k * SUBLANES, SUBLANES), SUBLANES)
        a = a_sc[rows, :]
        g = g_sc[rows, :]
        for d in (1, 2, 4):
            keep = sub >= d
            g = jnp.where(keep, a * pltpu.roll(g, d, axis=0) + g, g)
            a = jnp.where(keep, a * pltpu.roll(a, d, axis=0), a)
        hs = a * h_in + g
        hs_sc[rows, :] = hs
        return jnp.broadcast_to(hs[SUBLANES - 1:SUBLANES, :], (SUBLANES, RNN_WIDTH))

    h_sc[...] = lax.fori_loop(0, tt // SUBLANES, tile_scan, h_sc[...])

    y = y_ref[...]
    gelu = 0.5 * y * (1.0 + jnp.tanh(np.sqrt(2.0 / np.pi) * (y + 0.044715 * (y * y * y))))
    o_ref[...] = (hs_sc[...] * gelu).astype(o_ref.dtype)


def _rglru(u_a, y_blk, x_blk, conv_w, conv_b, w_rg, b_rg, w_ig, b_ig, lam):
    t = u_a.shape[0]
    tt = min(256, t)
    const2 = lambda a: pl.BlockSpec(a.shape, lambda i: (0, 0))
    const3 = lambda a: pl.BlockSpec(a.shape, lambda i: (0, 0, 0))
    return pl.pallas_call(
        _rglru_kernel,
        out_shape=jax.ShapeDtypeStruct((t, RNN_WIDTH), BF16),
        grid=(t // tt,),
        in_specs=[pl.BlockSpec((tt, RNN_WIDTH), lambda i: (i, y_blk)),
                  pl.BlockSpec((tt, RNN_WIDTH), lambda i: (i, x_blk)),
                  const2(conv_w), const2(conv_b), const3(w_rg), const2(b_rg),
                  const3(w_ig), const2(b_ig), const2(lam)],
        out_specs=pl.BlockSpec((tt, RNN_WIDTH), lambda i: (i, 0)),
        scratch_shapes=[pltpu.VMEM((SUBLANES, RNN_WIDTH), F32),
                        pltpu.VMEM((SUBLANES, RNN_WIDTH), F32),
                        pltpu.VMEM((tt, RNN_WIDTH), F32),
                        pltpu.VMEM((tt, RNN_WIDTH), F32),
                        pltpu.VMEM((tt, RNN_WIDTH), F32)],
        compiler_params=_params(("arbitrary",), 32),
        name="rglru",
    )(u_a, u_a, conv_w, conv_b, w_rg, b_rg, w_ig, b_ig, lam)


def _merge_kernel(a_ref, b_ref, c_ref, wa_ref, wb_ref, wc_ref, g0_ref, g1_ref, g2_ref,
                  bm_ref, o_ref):
    def branch(x_ref, w_ref, g_ref, n):
        y = jnp.dot(x_ref[...], w_ref[...], preferred_element_type=F32)
        return _sigmoid(g_ref[...] + bm_ref[n:n + 1, :]) * y

    mixed = (branch(a_ref, wa_ref, g0_ref, 0) + branch(b_ref, wb_ref, g1_ref, 1)
             + branch(c_ref, wc_ref, g2_ref, 2))
    o_ref[...] = mixed.astype(o_ref.dtype)


def _merge(ya, yb, yc, wa, wb, wc, u_a, b_merge):
    t = ya.shape[0]
    tm, tn = min(512, t), 1024
    nn = D_MODEL // tn
    xin = pl.BlockSpec((tm, ya.shape[1]), lambda j, i: (i, 0))
    win = pl.BlockSpec((ya.shape[1], tn), lambda j, i: (0, j))
    gate = lambda n: pl.BlockSpec((tm, tn), lambda j, i: (i, n * nn + j))
    return pl.pallas_call(
        _merge_kernel,
        out_shape=jax.ShapeDtypeStruct((t, D_MODEL), BF16),
        grid=(nn, t // tm),
        in_specs=[xin, xin, xin, win, win, win, gate(0), gate(1), gate(2),
                  pl.BlockSpec((N_BRANCHES, tn), lambda j, i: (0, j))],
        out_specs=pl.BlockSpec((tm, tn), lambda j, i: (i, j)),
        compiler_params=_params(("arbitrary", "arbitrary"), 48),
        name="merge",
    )(ya, yb, yc, wa, wb, wc, u_a, u_a, u_a, b_merge)


def _outproj_ln_kernel(m_ref, w_ref, x_ref, g_ref, b_ref, of_ref, ob_ref):
    mix = jnp.dot(m_ref[...], w_ref[...], preferred_element_type=F32)
    y = _layer_norm(DEEPNORM_ALPHA * x_ref[...] + mix, g_ref[...], b_ref[...])
    of_ref[...] = y
    ob_ref[...] = y.astype(BF16)


def _outproj_ln(mixed, w_out, x, g, b):
    t = x.shape[0]
    tm = min(256, t)
    row = pl.BlockSpec((tm, D_MODEL), lambda i: (i, 0))
    vec = pl.BlockSpec((1, D_MODEL), lambda i: (0, 0))
    return pl.pallas_call(
        _outproj_ln_kernel,
        out_shape=[jax.ShapeDtypeStruct((t, D_MODEL), F32),
                   jax.ShapeDtypeStruct((t, D_MODEL), BF16)],
        grid=(t // tm,),
        in_specs=[row, pl.BlockSpec((D_MODEL, D_MODEL), lambda i: (0, 0)), row, vec, vec],
        out_specs=[row, row],
        compiler_params=_params(("arbitrary",), 48),
        name="outproj_ln",
    )(mixed, w_out, x, g, b)


def _router_kernel(x_ref, w_ref, b_ref, idx_ref, gate_ref):
    nt = (((1,), (1,)), ((), ()))
    x = x_ref[...]
    x_hi = x.astype(BF16)
    x_lo = (x - x_hi.astype(F32)).astype(BF16)
    w = w_ref[...]
    w_hi = w.astype(BF16)
    w_lo = (w - w_hi.astype(F32)).astype(BF16)
    logits = (lax.dot_general(w_hi, x_hi, nt, preferred_element_type=F32)
              + lax.dot_general(w_hi, x_lo, nt, preferred_element_type=F32)
              + lax.dot_general(w_lo, x_hi, nt, preferred_element_type=F32)
              + b_ref[...])
    eidx = lax.broadcasted_iota(I32, logits.shape, 0)
    vals, idxs = [], []
    for _ in range(TOP_K):
        m = jnp.max(logits, axis=0, keepdims=True)
        idx = jnp.min(jnp.where(logits == m, eidx, N_EXPERTS), axis=0, keepdims=True)
        vals.append(m)
        idxs.append(idx)
        logits = jnp.where(eidx == idx, -jnp.inf, logits)
    exps = [jnp.exp(v - vals[0]) for v in vals]
    denom = exps[0] + exps[1] + exps[2] + exps[3]
    pad = SUBLANES - TOP_K
    tokens = logits.shape[1]
    idx_ref[...] = jnp.concatenate(idxs + [jnp.zeros((pad, tokens), I32)], axis=0)
    gate_ref[...] = jnp.concatenate([e / denom for e in exps]
                                    + [jnp.zeros((pad, tokens), F32)], axis=0)


def _router(x, w_router_t, b_router):
    t = x.shape[0]
    tm = min(1024, t)
    idx, gate = pl.pallas_call(
        _router_kernel,
        out_shape=[jax.ShapeDtypeStruct((SUBLANES, t), I32),
                   jax.ShapeDtypeStruct((SUBLANES, t), F32)],
        grid=(t // tm,),
        in_specs=[pl.BlockSpec((tm, D_MODEL), lambda i: (i, 0)),
                  pl.BlockSpec((N_EXPERTS, D_MODEL), lambda i: (0, 0)),
                  pl.BlockSpec((N_EXPERTS, 1), lambda i: (0, 0))],
        out_specs=[pl.BlockSpec((SUBLANES, tm), lambda i: (0, i)),
                   pl.BlockSpec((SUBLANES, tm), lambda i: (0, i))],
        compiler_params=_params(("arbitrary",), 48),
        name="router",
    )(x, w_router_t, b_router.reshape(N_EXPERTS, 1))
    return idx[:TOP_K].T, gate[:TOP_K].T


def _row_copy(src_hbm, row, dst, dst_row, sem):
    return pltpu.make_async_copy(src_hbm.at[pl.ds(row, 1), :],
                                 dst.at[pl.ds(dst_row, 1), :], sem)


def _gather_kernel(tok_ref, nused_ref, x_hbm, o_ref, buf, sem):
    rb = buf.shape[0]
    b = pl.program_id(0)

    @pl.when(b < nused_ref[0])
    def _():
        def issue(r, c):
            _row_copy(x_hbm, tok_ref[b * rb + r], buf, r, sem).start()
            return c

        def drain(r, c):
            _row_copy(x_hbm, 0, buf, r, sem).wait()
            return c

        lax.fori_loop(0, rb, issue, 0)
        lax.fori_loop(0, rb, drain, 0)
        o_ref[...] = buf[...].astype(o_ref.dtype)

    @pl.when(b >= nused_ref[0])
    def _():
        o_ref[...] = jnp.zeros(o_ref.shape, o_ref.dtype)


def _gather_rows(x, row_tok, n_used):
    n_rows = row_tok.shape[0]
    return pl.pallas_call(
        _gather_kernel,
        out_shape=jax.ShapeDtypeStruct((n_rows, D_MODEL), BF16),
        grid_spec=pltpu.PrefetchScalarGridSpec(
            num_scalar_prefetch=2,
            grid=(n_rows // ROW_BLOCK,),
            in_specs=[pl.BlockSpec(memory_space=pl.ANY)],
            out_specs=pl.BlockSpec((ROW_BLOCK, D_MODEL), lambda b, tok, nu: (b, 0)),
            scratch_shapes=[pltpu.VMEM((ROW_BLOCK, D_MODEL), F32),
                            pltpu.SemaphoreType.DMA(())],
        ),
        compiler_params=_params(("arbitrary",), 32),
        name="dispatch_gather",
    )(row_tok, n_used, x)


def _expert_changed(be_ref, b):
    return (b == 0) | (be_ref[b] != be_ref[jnp.maximum(b - 1, 0)])


def _expert_up_kernel(be_ref, nused_ref, rows_ref, wg_ref, wl_ref, bg_ref, bl_ref, o_ref,
                      wg_sc, wl_sc):
    b = pl.program_id(1)

    @pl.when(_expert_changed(be_ref, b))
    def _():
        wg_sc[...] = wg_ref[0].astype(BF16)
        wl_sc[...] = wl_ref[0].astype(BF16)

    @pl.when(b < nused_ref[0])
    def _():
        x = rows_ref[...]
        h_glu = jnp.dot(x, wg_sc[...], preferred_element_type=F32) + bg_ref[0]
        h_lin = jnp.dot(x, wl_sc[...], preferred_element_type=F32) + bl_ref[0]
        h_glu = jnp.minimum(h_glu, SWIGLU_LIMIT)
        h_lin = jnp.clip(h_lin, -SWIGLU_LIMIT, SWIGLU_LIMIT)
        act = h_glu * _sigmoid(SWIGLU_ALPHA * h_glu) * (h_lin + 1.0)
        o_ref[...] = act.astype(o_ref.dtype)

    @pl.when(b >= nused_ref[0])
    def _():
        o_ref[...] = jnp.zeros(o_ref.shape, o_ref.dtype)


def _expert_up(rows, block_expert, n_used, w_up, b_up):
    n_rows = rows.shape[0]
    tn = 1024
    nf = EXPERT_FF // tn
    b_up3 = b_up.reshape(N_EXPERTS, 1, 2 * EXPERT_FF)
    return pl.pallas_call(
        _expert_up_kernel,
        out_shape=jax.ShapeDtypeStruct((n_rows, EXPERT_FF), BF16),
        grid_spec=pltpu.PrefetchScalarGridSpec(
            num_scalar_prefetch=2,
            grid=(nf, n_rows // ROW_BLOCK),
            in_specs=[
                pl.BlockSpec((ROW_BLOCK, D_MODEL), lambda f, b, be, nu: (b, 0)),
                pl.BlockSpec((1, D_MODEL, tn), lambda f, b, be, nu: (be[b], 0, f)),
                pl.BlockSpec((1, D_MODEL, tn), lambda f, b, be, nu: (be[b], 0, nf + f)),
                pl.BlockSpec((1, 1, tn), lambda f, b, be, nu: (be[b], 0, f)),
                pl.BlockSpec((1, 1, tn), lambda f, b, be, nu: (be[b], 0, nf + f)),
            ],
            out_specs=pl.BlockSpec((ROW_BLOCK, tn), lambda f, b, be, nu: (b, f)),
            scratch_shapes=[pltpu.VMEM((D_MODEL, tn), BF16),
                            pltpu.VMEM((D_MODEL, tn), BF16)],
        ),
        compiler_params=_params(("arbitrary", "arbitrary"), 56),
        name="expert_up",
    )(block_expert, n_used, rows, w_up, w_up, b_up3, b_up3)


def _expert_down_kernel(be_ref, nused_ref, act_ref, w_ref, bias_ref, o_ref, w_sc):
    b = pl.program_id(1)

    @pl.when(_expert_changed(be_ref, b))
    def _():
        w_sc[...] = w_ref[0].astype(BF16)

    @pl.when(b < nused_ref[0])
    def _():
        o_ref[...] = (jnp.dot(act_ref[...], w_sc[...], preferred_element_type=F32)
                      + bias_ref[0])

    @pl.when(b >= nused_ref[0])
    def _():
        o_ref[...] = jnp.zeros(o_ref.shape, o_ref.dtype)


def _expert_down(act, block_expert, n_used, w_down, b_down):
    n_rows = act.shape[0]
    tn = 1024
    b_down3 = b_down.reshape(N_EXPERTS, 1, D_MODEL)
    return pl.pallas_call(
        _expert_down_kernel,
        out_shape=jax.ShapeDtypeStruct((n_rows, D_MODEL), F32),
        grid_spec=pltpu.PrefetchScalarGridSpec(
            num_scalar_prefetch=2,
            grid=(D_MODEL // tn, n_rows // ROW_BLOCK),
            in_specs=[
                pl.BlockSpec((ROW_BLOCK, EXPERT_FF), lambda n, b, be, nu: (b, 0)),
                pl.BlockSpec((1, EXPERT_FF, tn), lambda n, b, be, nu: (be[b], 0, n)),
                pl.BlockSpec((1, 1, tn), lambda n, b, be, nu: (be[b], 0, n)),
            ],
            out_specs=pl.BlockSpec((ROW_BLOCK, tn), lambda n, b, be, nu: (b, n)),
            scratch_shapes=[pltpu.VMEM((EXPERT_FF, tn), BF16)],
        ),
        compiler_params=_params(("arbitrary", "arbitrary"), 48),
        name="expert_down",
    )(block_expert, n_used, act, w_down, b_down3)


def _combine_ln_kernel(pos_ref, y_hbm, gate_ref, x_ref, g_ref, b_ref, of_ref, ob_ref,
                       buf, sem):
    tc = x_ref.shape[0]
    base = pl.program_id(0) * tc

    def issue(r, c):
        for k in range(TOP_K):
            _row_copy(y_hbm, pos_ref[(base + r) * TOP_K + k], buf.at[k], r, sem).start()
        return c

    def drain(r, c):
        for k in range(TOP_K):
            _row_copy(y_hbm, 0, buf.at[k], r, sem).wait()
        return c

    lax.fori_loop(0, tc, issue, 0)
    lax.fori_loop(0, tc, drain, 0)
    gate = gate_ref[...]
    ffn = gate[:, 0:1] * buf[0]
    for k in range(1, TOP_K):
        ffn = ffn + gate[:, k:k + 1] * buf[k]
    y = _layer_norm(DEEPNORM_ALPHA * x_ref[...] + ffn, g_ref[...], b_ref[...])
    of_ref[...] = y
    ob_ref[...] = y.astype(BF16)


def _combine_ln(y_rows, pos, gate, x, g, b):
    t = x.shape[0]
    tc = min(128, t)
    row = lambda b_, p: (b_, 0)
    return pl.pallas_call(
        _combine_ln_kernel,
        out_shape=[jax.ShapeDtypeStruct((t, D_MODEL), F32),
                   jax.ShapeDtypeStruct((t, D_MODEL), BF16)],
        grid_spec=pltpu.PrefetchScalarGridSpec(
            num_scalar_prefetch=1,
            grid=(t // tc,),
            in_specs=[pl.BlockSpec(memory_space=pl.ANY),
                      pl.BlockSpec((tc, TOP_K), row),
                      pl.BlockSpec((tc, D_MODEL), row),
                      pl.BlockSpec((1, D_MODEL), lambda b_, p: (0, 0)),
                      pl.BlockSpec((1, D_MODEL), lambda b_, p: (0, 0))],
            out_specs=[pl.BlockSpec((tc, D_MODEL), row), pl.BlockSpec((tc, D_MODEL), row)],
            scratch_shapes=[pltpu.VMEM((TOP_K, tc, D_MODEL), F32),
                            pltpu.SemaphoreType.DMA(())],
        ),
        compiler_params=_params(("arbitrary",), 32),
        name="combine_ln",
    )(pos, y_rows, gate, x, g, b)


def _dispatch_plan(top_idx):
    t = top_idx.shape[0]
    n_assign = t * TOP_K
    n_blocks = -(-n_assign // ROW_BLOCK) + N_EXPERTS
    e_flat = top_idx.reshape(-1)
    onehot = (e_flat[:, None] == jnp.arange(N_EXPERTS, dtype=I32)[None, :]).astype(I32)
    csum = jnp.cumsum(onehot, axis=0)
    rank = jnp.sum(csum * onehot, axis=1) - 1
    counts = csum[-1]
    padded = (counts + ROW_BLOCK - 1) // ROW_BLOCK * ROW_BLOCK
    pend = jnp.cumsum(padded)
    pstart = pend - padded
    dest = (pstart[e_flat] + rank).astype(I32)
    row_tok = jnp.zeros((n_blocks * ROW_BLOCK,), I32).at[dest].set(
        jnp.arange(n_assign, dtype=I32) // TOP_K)
    block_expert = jnp.minimum(
        jnp.searchsorted(pend, jnp.arange(n_blocks, dtype=I32) * ROW_BLOCK, side='right'),
        N_EXPERTS - 1).astype(I32)
    n_used = (pend[-1:] // ROW_BLOCK).astype(I32)
    return dest, row_tok, block_expert, n_used


def _prep_in_weights(w_in_l):
    d = D_MODEL
    o_gate = N_BRANCHES * d
    o_fq = o_gate + 2 * RNN_WIDTH
    o_fl = o_fq + 3 * FOX_WIDTH
    o_cq = o_fl + HEADS
    o_kr = o_cq + 2 * MLA_RANK
    w_a = w_in_l[:, :o_fq].astype(BF16)
    w_b = jnp.concatenate([w_in_l[:, o_fq:o_fq + FOX_WIDTH] * (HEAD_DIM ** -0.5),
                           w_in_l[:, o_fq + FOX_WIDTH:o_fl]], axis=1).astype(BF16)
    pad = LANES - MLA_ROPE_DIM - HEADS
    w_c = jnp.concatenate([w_in_l[:, o_cq:o_kr], w_in_l[:, o_kr:o_kr + MLA_ROPE_DIM],
                           w_in_l[:, o_fl:o_cq], jnp.zeros((d, pad), F32)],
                          axis=1).astype(BF16)
    return w_a, w_b, w_c


def _prep_uq(w_uq_l):
    w = w_uq_l.reshape(MLA_RANK, HEADS, MLA_QK_DIM) * (MLA_QK_DIM ** -0.5)
    pad = jnp.zeros((MLA_RANK, HEADS, MLA_PAD_DIM - MLA_QK_DIM), F32)
    return jnp.concatenate([w, pad], axis=2).reshape(MLA_RANK, HEADS * MLA_PAD_DIM).astype(BF16)


def _layer(x, x_bf, tabs, p):
    t = x.shape[0]
    w_a, w_b, w_c = _prep_in_weights(p['w_in'])
    u_a = _matmul(x_bf, w_a, F32, 1024, 1024, "in_proj_gates_lru")
    u_b = _matmul(x_bf, w_b, BF16, 1024, 1024, "in_proj_fox")
    u_c = _matmul(x_bf, w_c, F32, 1024, w_c.shape[1], "in_proj_small")

    gate_blocks = N_BRANCHES * D_MODEL // RNN_WIDTH
    y_a = _rglru(u_a, gate_blocks, gate_blocks + 1,
                 p['conv_w'], p['conv_b'].reshape(1, RNN_WIDTH),
                 p['w_rec_gate'].astype(BF16), p['b_rec_gate'].reshape(1, RNN_WIDTH),
                 p['w_inp_gate'].astype(BF16), p['b_inp_gate'].reshape(1, RNN_WIDTH),
                 p['lru_lambda'].reshape(1, RNN_WIDTH))

    fl_col = 2 * MLA_RANK + MLA_ROPE_DIM
    cum = _fox_cum(u_c[:, fl_col:fl_col + HEADS], p['b_forget'])
    y_b = _causal_attention(u_b, u_b, u_b, 0, HEADS, 2 * HEADS, HEAD_DIM, cum, "fox_attention")

    q_f, k_f, v_c = _mla_prep(u_c, p['g_cq'].reshape(1, MLA_RANK),
                              p['g_ckv'].reshape(1, MLA_RANK), _prep_uq(p['w_uq']),
                              p['w_ukv'].astype(BF16), tabs)
    y_c = _causal_attention(q_f, k_f, v_c, 0, 0, 0, MLA_PAD_DIM, None, "mla_attention")

    mixed = _merge(y_a, y_b, y_c, p['w_proj_lru'].astype(BF16), p['w_proj_fox'].astype(BF16),
                   p['w_proj_mla'].astype(BF16), u_a, p['b_merge'])
    x, x_bf = _outproj_ln(mixed, p['w_out'].astype(BF16), x,
                          p['ln1_g'].reshape(1, D_MODEL), p['ln1_b'].reshape(1, D_MODEL))

    top_idx, gate = _router(x, p['w_router'].T, p['b_router'])
    dest, row_tok, block_expert, n_used = _dispatch_plan(top_idx)
    rows = _gather_rows(x, row_tok, n_used)
    act = _expert_up(rows, block_expert, n_used, p['w_up'], p['b_up'])
    y_rows = _expert_down(act, block_expert, n_used, p['w_down'], p['b_down'])
    del t
    return _combine_ln(y_rows, dest, gate, x, p['ln2_g'].reshape(1, D_MODEL),
                       p['ln2_b'].reshape(1, D_MODEL))


_LAYER_PARAMS = ('w_in', 'b_merge', 'b_forget', 'conv_w', 'conv_b', 'w_rec_gate', 'b_rec_gate',
                 'w_inp_gate', 'b_inp_gate', 'lru_lambda', 'g_cq', 'g_ckv', 'w_uq', 'w_ukv',
                 'w_proj_lru', 'w_proj_fox', 'w_proj_mla', 'w_out', 'ln1_g', 'ln1_b',
                 'w_router', 'b_router', 'w_up', 'b_up', 'w_down', 'b_down', 'ln2_g', 'ln2_b')


def kernel(x, positions, w_in, b_merge, b_forget, conv_w, conv_b, w_rec_gate, b_rec_gate,
           w_inp_gate, b_inp_gate, lru_lambda, g_cq, g_ckv, w_uq, w_ukv, w_proj_lru,
           w_proj_fox, w_proj_mla, w_out, ln1_g, ln1_b, w_router, b_router, w_up, b_up,
           w_down, b_down, ln2_g, ln2_b):
    stacked = dict(zip(_LAYER_PARAMS, (
        w_in, b_merge, b_forget, conv_w, conv_b, w_rec_gate, b_rec_gate, w_inp_gate,
        b_inp_gate, lru_lambda, g_cq, g_ckv, w_uq, w_ukv, w_proj_lru, w_proj_fox,
        w_proj_mla, w_out, ln1_g, ln1_b, w_router, b_router, w_up, b_up, w_down, b_down,
        ln2_g, ln2_b)))
    batch, seq, _ = x.shape
    assert batch == 1
    xt = x.reshape(seq, D_MODEL)
    x_bf = xt.astype(BF16)
    tabs = _rope_tables(positions.reshape(seq))
    for l in range(w_in.shape[0]):
        xt, x_bf = _layer(xt, x_bf, tabs, {k: v[l] for k, v in stacked.items()})
    return xt.reshape(batch, seq, D_MODEL)
```

```python
import functools

import jax
import jax.numpy as jnp
import numpy as np
from jax import lax
from jax.experimental import pallas as pl
from jax.experimental.pallas import tpu as pltpu

F32 = jnp.float32
BF16 = jnp.bfloat16
I32 = jnp.int32

D_MODEL = 2048
N_BRANCHES = 3
DEEPNORM_ALPHA = (2 * 2) ** 0.25
LN_EPS = 1e-5
RMS_EPS = 1e-6
RNN_WIDTH = 1024
RNN_BLOCKS = 8
RNN_BLOCK_W = RNN_WIDTH // RNN_BLOCKS
CONV_WIDTH = 4
RG_LRU_C = 8.0
HEADS = 8
HEAD_DIM = 128
FOX_WIDTH = HEADS * HEAD_DIM
MLA_RANK = 512
MLA_ROPE_DIM = 64
MLA_QK_DIM = HEAD_DIM + MLA_ROPE_DIM
ATTN_QK_DIM = 256
MLA_PAD_DIM = ATTN_QK_DIM
LOG2E = float(np.log2(np.e))
ROPE_THETA = 10000.0
N_EXPERTS = 32
TOP_K = 4
EXPERT_FF = D_MODEL
SWIGLU_ALPHA = 1.702
SWIGLU_LIMIT = 7.0

LANES = 128
SUBLANES = 8
V7X_VMEM_BYTES = 64 * 1024 * 1024

ROW_BLOCK = 256
ATTN_BLOCK = 512
GATHER_UNROLL = 8
ATTN_GROUP = 4
MASK_VALUE = -1e30


def _params(semantics, vmem_mib):
    assert vmem_mib * 1024 * 1024 < V7X_VMEM_BYTES
    return pltpu.CompilerParams(dimension_semantics=semantics,
                                vmem_limit_bytes=vmem_mib * 1024 * 1024)


def _split3(x):
    hi = x.astype(BF16)
    r = x - hi.astype(F32)
    mid = r.astype(BF16)
    lo = (r - mid.astype(F32)).astype(BF16)
    return hi, mid, lo


def _sigmoid(x):
    return 1.0 / (1.0 + jnp.exp(-x))


def _softplus(z):
    return jnp.maximum(z, 0.0) + jnp.log1p(jnp.exp(-jnp.abs(z)))


def _layer_norm(v, g, b):
    mu = jnp.mean(v, axis=-1, keepdims=True)
    c = v - mu
    var = jnp.mean(c * c, axis=-1, keepdims=True)
    return c * lax.rsqrt(var + LN_EPS) * g + b


def _mm_kernel(x_ref, w_ref, o_ref):
    o_ref[...] = jnp.dot(x_ref[...], w_ref[...],
                         preferred_element_type=F32).astype(o_ref.dtype)


def _matmul(x, w, out_dtype, tm, tn, name):
    m, k = x.shape
    n = w.shape[1]
    tm, tn = min(tm, m), min(tn, n)
    return pl.pallas_call(
        _mm_kernel,
        out_shape=jax.ShapeDtypeStruct((m, n), out_dtype),
        grid=(m // tm, n // tn),
        in_specs=[pl.BlockSpec((tm, k), lambda i, j: (i, 0)),
                  pl.BlockSpec((k, tn), lambda i, j: (0, j))],
        out_specs=pl.BlockSpec((tm, tn), lambda i, j: (i, j)),
        compiler_params=_params(("parallel", "arbitrary"), 48),
        name=name,
    )(x, w)


def _fox_cum_kernel(fl_ref, b_ref, o_ref, *, chunks):
    z = fl_ref[...] + b_ref[...]
    lf = jnp.minimum(z, 0.0) - jnp.log1p(jnp.exp(-jnp.abs(z)))
    rows = lf.shape[0]
    s = lax.broadcasted_iota(I32, (LANES, LANES), 0)
    t = lax.broadcasted_iota(I32, (LANES, LANES), 1)
    tri = (s <= t).astype(BF16)
    incl = None
    for part in _split3(lf):
        d = jnp.dot(part, tri, preferred_element_type=F32)
        incl = d if incl is None else incl + d
    i = lax.broadcasted_iota(I32, (rows, rows), 0)
    j = lax.broadcasted_iota(I32, (rows, rows), 1)
    lower = ((i // chunks == j // chunks) & (j < i)).astype(BF16)
    offs = None
    for part in _split3(incl):
        d = jnp.dot(lower, part, preferred_element_type=F32)
        offs = d if offs is None else offs + d
    o_ref[...] = incl + offs[:, LANES - 1:LANES]


def _fox_cum(f_logit, b_f):
    t = f_logit.shape[0]
    chunks = t // LANES
    fl = f_logit.T.reshape(HEADS * chunks, LANES)
    b = jnp.repeat(b_f.astype(F32), chunks).reshape(HEADS * chunks, 1)
    out = pl.pallas_call(
        functools.partial(_fox_cum_kernel, chunks=chunks),
        out_shape=jax.ShapeDtypeStruct((HEADS * chunks, LANES), F32),
        name="fox_cum",
    )(fl, b)
    return out.reshape(HEADS, t)


def _attn_kernel(qi_ref, ki_ref, q_ref, k_ref, v_ref, o_ref, m_sc, l_sc, acc_sc, *, group):
    p = pl.program_id(1)
    qi = qi_ref[p]
    ki = ki_ref[p]
    blk = q_ref.shape[0]

    @pl.when(ki == 0)
    def _():
        m_sc[...] = jnp.full(m_sc.shape, MASK_VALUE, F32)
        l_sc[...] = jnp.zeros(l_sc.shape, F32)
        acc_sc[...] = jnp.zeros(acc_sc.shape, F32)

    def step(masked):
        for g in range(group):
            qk = slice(g * ATTN_QK_DIM, (g + 1) * ATTN_QK_DIM)
            vo = slice(g * HEAD_DIM, (g + 1) * HEAD_DIM)
            s = lax.dot_general(q_ref[:, qk], k_ref[:, qk], (((1,), (1,)), ((), ())),
                                preferred_element_type=F32)
            if masked:
                row = lax.broadcasted_iota(I32, s.shape, 0)
                col = lax.broadcasted_iota(I32, s.shape, 1)
                s = jnp.where(col <= row, s, MASK_VALUE)
            chunks = [s[:, c * LANES:(c + 1) * LANES] for c in range(blk // LANES)]
            cmax = functools.reduce(jnp.maximum, chunks)
            m_prev = m_sc[g]
            m_new = jnp.maximum(m_prev, jnp.max(cmax, axis=1, keepdims=True))
            alpha = jnp.exp2(m_prev - m_new)
            probs = [jnp.exp2(c - m_new) for c in chunks]
            l_sc[g] = alpha * l_sc[g] + functools.reduce(jnp.add, probs)
            pb = jnp.concatenate([pr.astype(BF16) for pr in probs], axis=1)
            acc_sc[g] = alpha * acc_sc[g] + jnp.dot(pb, v_ref[:, vo],
                                                    preferred_element_type=F32)
            m_sc[g] = m_new

    @pl.when(ki < qi)
    def _():
        step(False)

    @pl.when(ki == qi)
    def _():
        step(True)
        for g in range(group):
            denom = jnp.sum(l_sc[g], axis=1, keepdims=True)
            o_ref[:, g * HEAD_DIM:(g + 1) * HEAD_DIM] = (acc_sc[g] / denom).astype(o_ref.dtype)


def _causal_attention(q, k, v, v_col, name):
    t = q.shape[0]
    blk = min(ATTN_BLOCK, t)
    nq = t // blk
    group = ATTN_GROUP
    pairs = [(a, b) for a in range(nq) for b in range(a + 1)]
    qi_tab = jnp.asarray(np.array([a for a, _ in pairs], np.int32))
    ki_tab = jnp.asarray(np.array([b for _, b in pairs], np.int32))
    v_blk = v_col // (group * HEAD_DIM)
    return pl.pallas_call(
        functools.partial(_attn_kernel, group=group),
        out_shape=jax.ShapeDtypeStruct((t, HEADS * HEAD_DIM), BF16),
        grid_spec=pltpu.PrefetchScalarGridSpec(
            num_scalar_prefetch=2,
            grid=(HEADS // group, len(pairs)),
            in_specs=[
                pl.BlockSpec((blk, group * ATTN_QK_DIM), lambda h, p, qi, ki: (qi[p], h)),
                pl.BlockSpec((blk, group * ATTN_QK_DIM), lambda h, p, qi, ki: (ki[p], h)),
                pl.BlockSpec((blk, group * HEAD_DIM), lambda h, p, qi, ki: (ki[p], v_blk + h)),
            ],
            out_specs=pl.BlockSpec((blk, group * HEAD_DIM), lambda h, p, qi, ki: (qi[p], h)),
            scratch_shapes=[pltpu.VMEM((group, blk, LANES), F32),
                            pltpu.VMEM((group, blk, LANES), F32),
                            pltpu.VMEM((group, blk, HEAD_DIM), F32)],
        ),
        compiler_params=_params(("parallel", "arbitrary"), 32),
        name=name,
    )(qi_tab, ki_tab, q, k, v)


def _fox_prep_kernel(q_ref, k_ref, c_ref, sel_ref, qo_ref, ko_ref):
    bias = None
    for j, part in enumerate(_split3(c_ref[...])):
        d = jnp.dot(part, sel_ref[j], preferred_element_type=F32)
        bias = d if bias is None else bias + d
    lane = lax.broadcasted_iota(I32, (q_ref.shape[0], LANES), 1)
    for h in range(HEADS):
        src = slice(h * HEAD_DIM, (h + 1) * HEAD_DIM)
        lo = h * ATTN_QK_DIM
        g = bias[:, src]
        qo_ref[:, lo:lo + HEAD_DIM] = q_ref[:, src]
        ko_ref[:, lo:lo + HEAD_DIM] = k_ref[:, src]
        qo_ref[:, lo + HEAD_DIM:lo + ATTN_QK_DIM] = jnp.where(
            (lane >= 3) & (lane < 6), 1.0, g).astype(BF16)
        ko_ref[:, lo + HEAD_DIM:lo + ATTN_QK_DIM] = jnp.where(
            lane < 3, 1.0, -pltpu.roll(g, 3, axis=1)).astype(BF16)


def _fox_prep(u_b, cum):
    t = u_b.shape[0]
    tm = min(512, t)
    c = jnp.pad(cum.T * LOG2E, ((0, 0), (0, LANES - HEADS)))
    sel = np.zeros((3, LANES, FOX_WIDTH), np.float32)
    for j in range(3):
        for h in range(HEADS):
            sel[j, h, h * HEAD_DIM + j] = 1.0
    wide = HEADS * ATTN_QK_DIM
    return pl.pallas_call(
        _fox_prep_kernel,
        out_shape=[jax.ShapeDtypeStruct((t, wide), BF16)] * 2,
        grid=(t // tm,),
        in_specs=[pl.BlockSpec((tm, FOX_WIDTH), lambda i: (i, 0)),
                  pl.BlockSpec((tm, FOX_WIDTH), lambda i: (i, 1)),
                  pl.BlockSpec((tm, LANES), lambda i: (i, 0)),
                  pl.BlockSpec((3, LANES, FOX_WIDTH), lambda i: (0, 0, 0))],
        out_specs=[pl.BlockSpec((tm, wide), lambda i: (i, 0))] * 2,
        compiler_params=_params(("arbitrary",), 32),
        name="fox_prep",
    )(u_b, u_b, c, jnp.asarray(sel, BF16))


def _rope_table_kernel(pos_ref, inv_ref, c_ref, s1_ref, s2_ref):
    half = MLA_ROPE_DIM // 2
    ang = pos_ref[...].astype(F32) * inv_ref[...]
    c = jnp.cos(ang)
    s = jnp.sin(ang)
    lane = lax.broadcasted_iota(I32, ang.shape, 1)
    c_ref[...] = jnp.where(lane < MLA_ROPE_DIM, c, 0.0)
    s1_ref[...] = jnp.where(lane < half, -s, 0.0)
    s2_ref[...] = jnp.where((lane >= half) & (lane < MLA_ROPE_DIM), s, 0.0)


def _rope_tables(positions):
    t = positions.shape[0]
    half = MLA_ROPE_DIM // 2
    inv_freq = ROPE_THETA ** (-jnp.arange(half, dtype=F32) / half)
    inv = jnp.concatenate([inv_freq, inv_freq, jnp.zeros((LANES - 2 * half,), F32)])
    tm = min(1024, t)
    spec = pl.BlockSpec((tm, LANES), lambda i: (i, 0))
    return pl.pallas_call(
        _rope_table_kernel,
        out_shape=[jax.ShapeDtypeStruct((t, LANES), F32)] * 3,
        grid=(t // tm,),
        in_specs=[pl.BlockSpec((tm, 1), lambda i: (i, 0)),
                  pl.BlockSpec((1, LANES), lambda i: (0, 0))],
        out_specs=[spec, spec, spec],
        name="rope_tables",
    )(positions.reshape(t, 1), inv.reshape(1, LANES))


def _rope_group(g, c, s1, s2):
    half = MLA_ROPE_DIM // 2
    return (g * c + pltpu.roll(g, LANES - half, axis=1) * s1
            + pltpu.roll(g, half, axis=1) * s2)


def _mla_prep_kernel(u_ref, gq_ref, gkv_ref, wq_ref, wkv_ref, c_ref, s1_ref, s2_ref,
                     q_ref, k_ref, v_ref):
    def rms(v, g):
        ms = jnp.mean(v * v, axis=-1, keepdims=True)
        return (v * lax.rsqrt(ms + RMS_EPS) * g).astype(BF16)

    c, s1, s2 = c_ref[...], s1_ref[...], s2_ref[...]
    cq = rms(u_ref[:, 0:MLA_RANK], gq_ref[...])
    ckv = rms(u_ref[:, MLA_RANK:2 * MLA_RANK], gkv_ref[...])
    q_pre = jnp.dot(cq, wq_ref[...], preferred_element_type=F32)
    kv = jnp.dot(ckv, wkv_ref[...], preferred_element_type=F32)
    k_rot = _rope_group(u_ref[:, 2 * MLA_RANK:2 * MLA_RANK + LANES], c, s1, s2).astype(BF16)
    for h in range(HEADS):
        lo = h * MLA_PAD_DIM
        mid = lo + HEAD_DIM
        hi = lo + MLA_PAD_DIM
        q_ref[:, lo:mid] = q_pre[:, lo:mid].astype(BF16)
        q_ref[:, mid:hi] = _rope_group(q_pre[:, mid:hi], c, s1, s2).astype(BF16)
        k_ref[:, lo:mid] = kv[:, lo:mid].astype(BF16)
        k_ref[:, mid:hi] = k_rot
        v_ref[:, h * HEAD_DIM:(h + 1) * HEAD_DIM] = kv[:, mid:hi].astype(BF16)


def _mla_prep(u_small, g_cq, g_ckv, w_uq_r, w_ukv, tabs):
    t = u_small.shape[0]
    tm = min(512, t)
    wide = HEADS * MLA_PAD_DIM
    row = lambda w: pl.BlockSpec((tm, w), lambda i: (i, 0))
    const = lambda a: pl.BlockSpec(a.shape, lambda i: (0, 0))
    return pl.pallas_call(
        _mla_prep_kernel,
        out_shape=[jax.ShapeDtypeStruct((t, wide), BF16),
                   jax.ShapeDtypeStruct((t, wide), BF16),
                   jax.ShapeDtypeStruct((t, HEADS * HEAD_DIM), BF16)],
        grid=(t // tm,),
        in_specs=[row(u_small.shape[1]), const(g_cq), const(g_ckv), const(w_uq_r),
                  const(w_ukv), row(LANES), row(LANES), row(LANES)],
        out_specs=[row(wide), row(wide), row(HEADS * HEAD_DIM)],
        compiler_params=_params(("arbitrary",), 48),
        name="mla_prep",
    )(u_small, g_cq, g_ckv, w_uq_r, w_ukv, *tabs)


def _rglru_kernel(y_ref, x_ref, cw_ref, cb_ref, wr_ref, br_ref, wi_ref, bi_ref, lam_ref,
                  o_ref, xprev_sc, h_sc, a_sc, g_sc, hs_sc):
    tt = x_ref.shape[0]

    @pl.when(pl.program_id(0) == 0)
    def _():
        xprev_sc[...] = jnp.zeros(xprev_sc.shape, F32)
        h_sc[...] = jnp.zeros(h_sc.shape, F32)

    x = x_ref[...]
    xext = jnp.concatenate([xprev_sc[...], x], axis=0)
    xprev_sc[...] = x[tt - SUBLANES:, :]
    xc = cb_ref[...]
    for j in range(CONV_WIDTH):
        off = SUBLANES - (CONV_WIDTH - 1) + j
        xc = xc + cw_ref[j:j + 1, :] * xext[off:off + tt, :]

    r_parts, i_parts = [], []
    for n in range(RNN_BLOCKS):
        sl = slice(n * RNN_BLOCK_W, (n + 1) * RNN_BLOCK_W)
        xb = xc[:, sl].astype(BF16)
        r_parts.append(_sigmoid(jnp.dot(xb, wr_ref[n], preferred_element_type=F32)
                                + br_ref[:, sl]))
        i_parts.append(_sigmoid(jnp.dot(xb, wi_ref[n], preferred_element_type=F32)
                                + bi_ref[:, sl]))
    r = jnp.concatenate(r_parts, axis=1)
    gate_i = jnp.concatenate(i_parts, axis=1)
    log_a = (-RG_LRU_C * _softplus(-lam_ref[...])) * r
    a = jnp.exp(log_a)
    a_sc[...] = a
    g_sc[...] = jnp.sqrt(-jnp.tanh(log_a) * (a * a + 1.0)) * (gate_i * xc)

    sub = lax.broadcasted_iota(I32, (SUBLANES, RNN_WIDTH), 0)

    def tile_scan(k, h_in):
        rows = pl.ds(pl.multiple_of(k * SUBLANES, SUBLANES), SUBLANES)
        a = a_sc[rows, :]
        g = g_sc[rows, :]
        for d in (1, 2, 4):
            keep = sub >= d
            g = jnp.where(keep, a * pltpu.roll(g, d, axis=0) + g, g)
            a = jnp.where(keep, a * pltpu.roll(a, d, axis=0), a)
        hs = a * h_in + g
        hs_sc[rows, :] = hs
        return jnp.broadcast_to(hs[SUBLANES - 1:SUBLANES, :], (SUBLANES, RNN_WIDTH))

    h_sc[...] = lax.fori_loop(0, tt // SUBLANES, tile_scan, h_sc[...])

    y = y_ref[...]
    gelu = 0.5 * y * (1.0 + jnp.tanh(np.sqrt(2.0 / np.pi) * (y + 0.044715 * (y * y * y))))
    o_ref[...] = (hs_sc[...] * gelu).astype(o_ref.dtype)


def _rglru(u_a, y_blk, x_blk, conv_w, conv_b, w_rg, b_rg, w_ig, b_ig, lam):
    t = u_a.shape[0]
    tt = min(256, t)
    const2 = lambda a: pl.BlockSpec(a.shape, lambda i: (0, 0))
    const3 = lambda a: pl.BlockSpec(a.shape, lambda i: (0, 0, 0))
    return pl.pallas_call(
        _rglru_kernel,
        out_shape=jax.ShapeDtypeStruct((t, RNN_WIDTH), BF16),
        grid=(t // tt,),
        in_specs=[pl.BlockSpec((tt, RNN_WIDTH), lambda i: (i, y_blk)),
                  pl.BlockSpec((tt, RNN_WIDTH), lambda i: (i, x_blk)),
                  const2(conv_w), const2(conv_b), const3(w_rg), const2(b_rg),
                  const3(w_ig), const2(b_ig), const2(lam)],
        out_specs=pl.BlockSpec((tt, RNN_WIDTH), lambda i: (i, 0)),
        scratch_shapes=[pltpu.VMEM((SUBLANES, RNN_WIDTH), F32),
                        pltpu.VMEM((SUBLANES, RNN_WIDTH), F32),
                        pltpu.VMEM((tt, RNN_WIDTH), F32),
                        pltpu.VMEM((tt, RNN_WIDTH), F32),
                        pltpu.VMEM((tt, RNN_WIDTH), F32)],
        compiler_params=_params(("arbitrary",), 32),
        name="rglru",
    )(u_a, u_a, conv_w, conv_b, w_rg, b_rg, w_ig, b_ig, lam)


def _merge_kernel(a_ref, b_ref, c_ref, wa_ref, wb_ref, wc_ref, g0_ref, g1_ref, g2_ref,
                  bm_ref, o_ref):
    def branch(x_ref, w_ref, g_ref, n):
        y = jnp.dot(x_ref[...], w_ref[...], preferred_element_type=F32)
        return _sigmoid(g_ref[...] + bm_ref[n:n + 1, :]) * y

    mixed = (branch(a_ref, wa_ref, g0_ref, 0) + branch(b_ref, wb_ref, g1_ref, 1)
             + branch(c_ref, wc_ref, g2_ref, 2))
    o_ref[...] = mixed.astype(o_ref.dtype)


def _merge(ya, yb, yc, wa, wb, wc, u_a, b_merge):
    t = ya.shape[0]
    tm, tn = min(512, t), 1024
    nn = D_MODEL // tn
    xin = pl.BlockSpec((tm, ya.shape[1]), lambda j, i: (i, 0))
    win = pl.BlockSpec((ya.shape[1], tn), lambda j, i: (0, j))
    gate = lambda n: pl.BlockSpec((tm, tn), lambda j, i: (i, n * nn + j))
    return pl.pallas_call(
        _merge_kernel,
        out_shape=jax.ShapeDtypeStruct((t, D_MODEL), BF16),
        grid=(nn, t // tm),
        in_specs=[xin, xin, xin, win, win, win, gate(0), gate(1), gate(2),
                  pl.BlockSpec((N_BRANCHES, tn), lambda j, i: (0, j))],
        out_specs=pl.BlockSpec((tm, tn), lambda j, i: (i, j)),
        compiler_params=_params(("arbitrary", "arbitrary"), 48),
        name="merge",
    )(ya, yb, yc, wa, wb, wc, u_a, u_a, u_a, b_merge)


def _outproj_ln_kernel(m_ref, w_ref, x_ref, g_ref, b_ref, of_ref, ob_ref):
    mix = jnp.dot(m_ref[...], w_ref[...], preferred_element_type=F32)
    y = _layer_norm(DEEPNORM_ALPHA * x_ref[...] + mix, g_ref[...], b_ref[...])
    of_ref[...] = y
    ob_ref[...] = y.astype(BF16)


def _outproj_ln(mixed, w_out, x, g, b):
    t = x.shape[0]
    tm = min(256, t)
    row = pl.BlockSpec((tm, D_MODEL), lambda i: (i, 0))
    vec = pl.BlockSpec((1, D_MODEL), lambda i: (0, 0))
    return pl.pallas_call(
        _outproj_ln_kernel,
        out_shape=[jax.ShapeDtypeStruct((t, D_MODEL), F32),
                   jax.ShapeDtypeStruct((t, D_MODEL), BF16)],
        grid=(t // tm,),
        in_specs=[row, pl.BlockSpec((D_MODEL, D_MODEL), lambda i: (0, 0)), row, vec, vec],
        out_specs=[row, row],
        compiler_params=_params(("arbitrary",), 48),
        name="outproj_ln",
    )(mixed, w_out, x, g, b)


def _router_kernel(x_ref, w_ref, b_ref, idx_ref, gate_ref, rank_ref, count_ref, count_sc):
    nt = (((1,), (1,)), ((), ()))
    x = x_ref[...]
    x_hi = x.astype(BF16)
    x_lo = (x - x_hi.astype(F32)).astype(BF16)
    w = w_ref[...]
    w_hi = w.astype(BF16)
    w_lo = (w - w_hi.astype(F32)).astype(BF16)
    logits = (lax.dot_general(w_hi, x_hi, nt, preferred_element_type=F32)
              + lax.dot_general(w_hi, x_lo, nt, preferred_element_type=F32)
              + lax.dot_general(w_lo, x_hi, nt, preferred_element_type=F32)
              + b_ref[...])
    eidx = lax.broadcasted_iota(I32, logits.shape, 0)
    vals, idxs = [], []
    for _ in range(TOP_K):
        m = jnp.max(logits, axis=0, keepdims=True)
        idx = jnp.min(jnp.where(logits == m, eidx, N_EXPERTS), axis=0, keepdims=True)
        vals.append(m)
        idxs.append(idx)
        logits = jnp.where(eidx == idx, -jnp.inf, logits)
    exps = [jnp.exp(v - vals[0]) for v in vals]
    denom = exps[0] + exps[1] + exps[2] + exps[3]
    pad = SUBLANES - TOP_K
    tokens = logits.shape[1]
    idx_ref[...] = jnp.concatenate(idxs + [jnp.zeros((pad, tokens), I32)], axis=0)
    gate_ref[...] = jnp.concatenate([e / denom for e in exps]
                                    + [jnp.zeros((pad, tokens), F32)], axis=0)

    @pl.when(pl.program_id(0) == 0)
    def _():
        count_sc[...] = jnp.zeros(count_sc.shape, F32)

    src = lax.broadcasted_iota(I32, (tokens, tokens), 0)
    dst = lax.broadcasted_iota(I32, (tokens, tokens), 1)
    before = (src < dst).astype(BF16)
    seen = count_sc[...]
    ranks = []
    for idx in idxs:
        hit = eidx == idx
        prefix = jnp.dot(hit.astype(BF16), before, preferred_element_type=F32)
        ranks.append(jnp.sum(jnp.where(hit, seen + prefix, 0.0), axis=0, keepdims=True))
        seen = seen + jnp.sum(hit.astype(F32), axis=1, keepdims=True)
    count_sc[...] = seen
    rank_ref[...] = jnp.concatenate(ranks + [jnp.zeros((pad, tokens), F32)],
                                    axis=0).astype(I32)
    count_ref[...] = seen.astype(I32)


def _router(x, w_router_t, b_router):
    t = x.shape[0]
    tm = min(1024, t)
    tok = pl.BlockSpec((SUBLANES, tm), lambda i: (0, i))
    idx, gate, rank, count = pl.pallas_call(
        _router_kernel,
        out_shape=[jax.ShapeDtypeStruct((SUBLANES, t), I32),
                   jax.ShapeDtypeStruct((SUBLANES, t), F32),
                   jax.ShapeDtypeStruct((SUBLANES, t), I32),
                   jax.ShapeDtypeStruct((N_EXPERTS, 1), I32)],
        grid=(t // tm,),
        in_specs=[pl.BlockSpec((tm, D_MODEL), lambda i: (i, 0)),
                  pl.BlockSpec((N_EXPERTS, D_MODEL), lambda i: (0, 0)),
                  pl.BlockSpec((N_EXPERTS, 1), lambda i: (0, 0))],
        out_specs=[tok, tok, tok, pl.BlockSpec((N_EXPERTS, 1), lambda i: (0, 0))],
        scratch_shapes=[pltpu.VMEM((N_EXPERTS, 1), F32)],
        compiler_params=_params(("arbitrary",), 48),
        name="router",
    )(x, w_router_t, b_router.reshape(N_EXPERTS, 1))
    return idx[:TOP_K].T, gate[:TOP_K].T, rank[:TOP_K].T, count[:, 0]


def _row_copy(src_hbm, row, dst, dst_row, sem):
    return pltpu.make_async_copy(src_hbm.at[pl.ds(row, 1), :],
                                 dst.at[pl.ds(dst_row, 1), :], sem)


def _gather_kernel(tok_ref, nused_ref, x_hbm, o_ref, buf, sem):
    rb = buf.shape[1]
    b = pl.program_id(0)
    n_used = nused_ref[0]

    def issue(blk):
        slot = blk % 2

        def body(i, c):
            for u in range(GATHER_UNROLL):
                r = i * GATHER_UNROLL + u
                _row_copy(x_hbm, tok_ref[blk * rb + r], buf.at[slot], r, sem.at[slot]).start()
            return c

        lax.fori_loop(0, rb // GATHER_UNROLL, body, 0)

    @pl.when((b == 0) & (n_used > 0))
    def _():
        issue(b)

    @pl.when(b + 1 < n_used)
    def _():
        issue(b + 1)

    @pl.when(b < n_used)
    def _():
        slot = b % 2

        def drain(i, c):
            for u in range(GATHER_UNROLL):
                _row_copy(x_hbm, 0, buf.at[slot], i * GATHER_UNROLL + u, sem.at[slot]).wait()
            return c

        lax.fori_loop(0, rb // GATHER_UNROLL, drain, 0)
        o_ref[...] = buf[slot].astype(o_ref.dtype)

    @pl.when(b >= n_used)
    def _():
        o_ref[...] = jnp.zeros(o_ref.shape, o_ref.dtype)


def _gather_rows(x, row_tok, n_used):
    n_rows = row_tok.shape[0]
    return pl.pallas_call(
        _gather_kernel,
        out_shape=jax.ShapeDtypeStruct((n_rows, D_MODEL), BF16),
        grid_spec=pltpu.PrefetchScalarGridSpec(
            num_scalar_prefetch=2,
            grid=(n_rows // ROW_BLOCK,),
            in_specs=[pl.BlockSpec(memory_space=pl.ANY)],
            out_specs=pl.BlockSpec((ROW_BLOCK, D_MODEL), lambda b, tok, nu: (b, 0)),
            scratch_shapes=[pltpu.VMEM((2, ROW_BLOCK, D_MODEL), F32),
                            pltpu.SemaphoreType.DMA((2,))],
        ),
        compiler_params=_params(("arbitrary",), 32),
        name="dispatch_gather",
    )(row_tok, n_used, x)


def _expert_changed(be_ref, b):
    return (b == 0) | (be_ref[b] != be_ref[jnp.maximum(b - 1, 0)])


def _expert_up_kernel(be_ref, nused_ref, rows_ref, wg_ref, wl_ref, bg_ref, bl_ref, o_ref,
                      wg_sc, wl_sc):
    b = pl.program_id(1)

    @pl.when(_expert_changed(be_ref, b))
    def _():
        wg_sc[...] = wg_ref[0].astype(BF16)
        wl_sc[...] = wl_ref[0].astype(BF16)

    @pl.when(b < nused_ref[0])
    def _():
        x = rows_ref[...]
        h_glu = jnp.dot(x, wg_sc[...], preferred_element_type=F32) + bg_ref[0]
        h_lin = jnp.dot(x, wl_sc[...], preferred_element_type=F32) + bl_ref[0]
        h_glu = jnp.minimum(h_glu, SWIGLU_LIMIT)
        h_lin = jnp.clip(h_lin, -SWIGLU_LIMIT, SWIGLU_LIMIT)
        act = h_glu * _sigmoid(SWIGLU_ALPHA * h_glu) * (h_lin + 1.0)
        o_ref[...] = act.astype(o_ref.dtype)

    @pl.when(b >= nused_ref[0])
    def _():
        o_ref[...] = jnp.zeros(o_ref.shape, o_ref.dtype)


def _expert_up(rows, block_expert, n_used, w_up, b_up):
    n_rows = rows.shape[0]
    tn = 1024
    nf = EXPERT_FF // tn
    b_up3 = b_up.reshape(b_up.shape[0], 1, 2 * EXPERT_FF)
    return pl.pallas_call(
        _expert_up_kernel,
        out_shape=jax.ShapeDtypeStruct((n_rows, EXPERT_FF), BF16),
        grid_spec=pltpu.PrefetchScalarGridSpec(
            num_scalar_prefetch=2,
            grid=(nf, n_rows // ROW_BLOCK),
            in_specs=[
                pl.BlockSpec((ROW_BLOCK, D_MODEL), lambda f, b, be, nu: (b, 0)),
                pl.BlockSpec((1, D_MODEL, tn), lambda f, b, be, nu: (be[b], 0, f)),
                pl.BlockSpec((1, D_MODEL, tn), lambda f, b, be, nu: (be[b], 0, nf + f)),
                pl.BlockSpec((1, 1, tn), lambda f, b, be, nu: (be[b], 0, f)),
                pl.BlockSpec((1, 1, tn), lambda f, b, be, nu: (be[b], 0, nf + f)),
            ],
            out_specs=pl.BlockSpec((ROW_BLOCK, tn), lambda f, b, be, nu: (b, f)),
            scratch_shapes=[pltpu.VMEM((D_MODEL, tn), BF16),
                            pltpu.VMEM((D_MODEL, tn), BF16)],
        ),
        compiler_params=_params(("arbitrary", "arbitrary"), 56),
        name="expert_up",
    )(block_expert, n_used, rows, w_up, w_up, b_up3, b_up3)


def _expert_down_kernel(be_ref, nused_ref, act_ref, w_ref, bias_ref, o_ref, w_sc):
    b = pl.program_id(1)

    @pl.when(_expert_changed(be_ref, b))
    def _():
        w_sc[...] = w_ref[0].astype(BF16)

    @pl.when(b < nused_ref[0])
    def _():
        o_ref[...] = (jnp.dot(act_ref[...], w_sc[...], preferred_element_type=F32)
                      + bias_ref[0])

    @pl.when(b >= nused_ref[0])
    def _():
        o_ref[...] = jnp.zeros(o_ref.shape, o_ref.dtype)


def _expert_down(act, block_expert, n_used, w_down, b_down):
    n_rows = act.shape[0]
    tn = 1024
    b_down3 = b_down.reshape(b_down.shape[0], 1, D_MODEL)
    return pl.pallas_call(
        _expert_down_kernel,
        out_shape=jax.ShapeDtypeStruct((n_rows, D_MODEL), F32),
        grid_spec=pltpu.PrefetchScalarGridSpec(
            num_scalar_prefetch=2,
            grid=(D_MODEL // tn, n_rows // ROW_BLOCK),
            in_specs=[
                pl.BlockSpec((ROW_BLOCK, EXPERT_FF), lambda n, b, be, nu: (b, 0)),
                pl.BlockSpec((1, EXPERT_FF, tn), lambda n, b, be, nu: (be[b], 0, n)),
                pl.BlockSpec((1, 1, tn), lambda n, b, be, nu: (be[b], 0, n)),
            ],
            out_specs=pl.BlockSpec((ROW_BLOCK, tn), lambda n, b, be, nu: (b, n)),
            scratch_shapes=[pltpu.VMEM((EXPERT_FF, tn), BF16)],
        ),
        compiler_params=_params(("arbitrary", "arbitrary"), 48),
        name="expert_down",
    )(block_expert, n_used, act, w_down, b_down3)


def _combine_ln_kernel(pos_ref, y_hbm, gate_ref, x_ref, g_ref, b_ref, of_ref, ob_ref,
                       buf, sem):
    tc = x_ref.shape[0]
    i = pl.program_id(0)
    unroll = GATHER_UNROLL // TOP_K

    def issue(tile):
        slot = tile % 2

        def body(j, c):
            for u in range(unroll):
                r = j * unroll + u
                for k in range(TOP_K):
                    _row_copy(y_hbm, pos_ref[(tile * tc + r) * TOP_K + k], buf.at[slot, k], r,
                              sem.at[slot]).start()
            return c

        lax.fori_loop(0, tc // unroll, body, 0)

    @pl.when(i == 0)
    def _():
        issue(i)

    @pl.when(i + 1 < pl.num_programs(0))
    def _():
        issue(i + 1)

    slot = i % 2

    def drain(j, c):
        for u in range(unroll):
            for k in range(TOP_K):
                _row_copy(y_hbm, 0, buf.at[slot, k], j * unroll + u, sem.at[slot]).wait()
        return c

    lax.fori_loop(0, tc // unroll, drain, 0)
    gate = gate_ref[...]
    ffn = gate[:, 0:1] * buf[slot, 0]
    for k in range(1, TOP_K):
        ffn = ffn + gate[:, k:k + 1] * buf[slot, k]
    y = _layer_norm(DEEPNORM_ALPHA * x_ref[...] + ffn, g_ref[...], b_ref[...])
    of_ref[...] = y
    ob_ref[...] = y.astype(BF16)


def _combine_ln(y_rows, pos, gate, x, g, b):
    t = x.shape[0]
    tc = min(128, t)
    row = lambda b_, p: (b_, 0)
    return pl.pallas_call(
        _combine_ln_kernel,
        out_shape=[jax.ShapeDtypeStruct((t, D_MODEL), F32),
                   jax.ShapeDtypeStruct((t, D_MODEL), BF16)],
        grid_spec=pltpu.PrefetchScalarGridSpec(
            num_scalar_prefetch=1,
            grid=(t // tc,),
            in_specs=[pl.BlockSpec(memory_space=pl.ANY),
                      pl.BlockSpec((tc, TOP_K), row),
                      pl.BlockSpec((tc, D_MODEL), row),
                      pl.BlockSpec((1, D_MODEL), lambda b_, p: (0, 0)),
                      pl.BlockSpec((1, D_MODEL), lambda b_, p: (0, 0))],
            out_specs=[pl.BlockSpec((tc, D_MODEL), row), pl.BlockSpec((tc, D_MODEL), row)],
            scratch_shapes=[pltpu.VMEM((2, TOP_K, tc, D_MODEL), F32),
                            pltpu.SemaphoreType.DMA((2,))],
        ),
        compiler_params=_params(("arbitrary",), 32),
        name="combine_ln",
    )(pos, y_rows, gate, x, g, b)


def _dispatch_plan(top_idx, rank, counts):
    t = top_idx.shape[0]
    n_assign = t * TOP_K
    n_blocks = -(-n_assign // ROW_BLOCK) + N_EXPERTS
    e_flat = top_idx.reshape(-1)
    padded = (counts + ROW_BLOCK - 1) // ROW_BLOCK * ROW_BLOCK
    pend = jnp.cumsum(padded)
    pstart = pend - padded
    dest = (pstart[e_flat] + rank.reshape(-1)).astype(I32)
    row_tok = jnp.zeros((n_blocks * ROW_BLOCK,), I32).at[dest].set(
        jnp.arange(n_assign, dtype=I32) // TOP_K)
    block_expert = jnp.minimum(
        jnp.searchsorted(pend, jnp.arange(n_blocks, dtype=I32) * ROW_BLOCK, side='right'),
        N_EXPERTS - 1).astype(I32)
    n_used = (pend[-1:] // ROW_BLOCK).astype(I32)
    return dest, row_tok, block_expert, n_used


def _prep_in_weights(w_in_l):
    d = D_MODEL
    o_gate = N_BRANCHES * d
    o_fq = o_gate + 2 * RNN_WIDTH
    o_fl = o_fq + 3 * FOX_WIDTH
    o_cq = o_fl + HEADS
    o_kr = o_cq + 2 * MLA_RANK
    w_a = w_in_l[:, :o_fq].astype(BF16)
    w_b = jnp.concatenate([w_in_l[:, o_fq:o_fq + FOX_WIDTH] * (HEAD_DIM ** -0.5 * LOG2E),
                           w_in_l[:, o_fq + FOX_WIDTH:o_fl]], axis=1).astype(BF16)
    pad = LANES - MLA_ROPE_DIM - HEADS
    w_c = jnp.concatenate([w_in_l[:, o_cq:o_kr], w_in_l[:, o_kr:o_kr + MLA_ROPE_DIM],
                           w_in_l[:, o_fl:o_cq], jnp.zeros((d, pad), F32)],
                          axis=1).astype(BF16)
    return w_a, w_b, w_c


def _prep_uq(w_uq_l):
    w = w_uq_l.reshape(MLA_RANK, HEADS, MLA_QK_DIM) * (MLA_QK_DIM ** -0.5 * LOG2E)
    pad = jnp.zeros((MLA_RANK, HEADS, MLA_PAD_DIM - MLA_QK_DIM), F32)
    return jnp.concatenate([w, pad], axis=2).reshape(MLA_RANK, HEADS * MLA_PAD_DIM).astype(BF16)


def _layer(x, x_bf, tabs, p):
    t = x.shape[0]
    w_a, w_b, w_c = _prep_in_weights(p['w_in'])
    u_a = _matmul(x_bf, w_a, F32, 1024, 1024, "in_proj_gates_lru")
    u_b = _matmul(x_bf, w_b, BF16, 1024, 1024, "in_proj_fox")
    u_c = _matmul(x_bf, w_c, F32, 1024, w_c.shape[1], "in_proj_small")

    gate_blocks = N_BRANCHES * D_MODEL // RNN_WIDTH
    y_a = _rglru(u_a, gate_blocks, gate_blocks + 1,
                 p['conv_w'], p['conv_b'].reshape(1, RNN_WIDTH),
                 p['w_rec_gate'].astype(BF16), p['b_rec_gate'].reshape(1, RNN_WIDTH),
                 p['w_inp_gate'].astype(BF16), p['b_inp_gate'].reshape(1, RNN_WIDTH),
                 p['lru_lambda'].reshape(1, RNN_WIDTH))

    fl_col = 2 * MLA_RANK + MLA_ROPE_DIM
    cum = _fox_cum(u_c[:, fl_col:fl_col + HEADS], p['b_forget'])
    q_x, k_x = _fox_prep(u_b, cum)
    y_b = _causal_attention(q_x, k_x, u_b, 2 * FOX_WIDTH, "fox_attention")

    q_f, k_f, v_c = _mla_prep(u_c, p['g_cq'].reshape(1, MLA_RANK),
                              p['g_ckv'].reshape(1, MLA_RANK), _prep_uq(p['w_uq']),
                              p['w_ukv'].astype(BF16), tabs)
    y_c = _causal_attention(q_f, k_f, v_c, 0, "mla_attention")

    mixed = _merge(y_a, y_b, y_c, p['w_proj_lru'].astype(BF16), p['w_proj_fox'].astype(BF16),
                   p['w_proj_mla'].astype(BF16), u_a, p['b_merge'])
    x, x_bf = _outproj_ln(mixed, p['w_out'].astype(BF16), x,
                          p['ln1_g'].reshape(1, D_MODEL), p['ln1_b'].reshape(1, D_MODEL))

    top_idx, gate, rank, counts = _router(x, p['w_router'].T, p['b_router'])
    dest, row_tok, block_expert, n_used = _dispatch_plan(top_idx, rank, counts)
    rows = _gather_rows(x, row_tok, n_used)
    block_expert = block_expert + p['expert_base']
    act = _expert_up(rows, block_expert, n_used, p['w_up_all'], p['b_up_all'])
    y_rows = _expert_down(act, block_expert, n_used, p['w_down_all'], p['b_down_all'])
    del t
    return _combine_ln(y_rows, dest, gate, x, p['ln2_g'].reshape(1, D_MODEL),
                       p['ln2_b'].reshape(1, D_MODEL))


_LAYER_PARAMS = ('w_in', 'b_merge', 'b_forget', 'conv_w', 'conv_b', 'w_rec_gate', 'b_rec_gate',
                 'w_inp_gate', 'b_inp_gate', 'lru_lambda', 'g_cq', 'g_ckv', 'w_uq', 'w_ukv',
                 'w_proj_lru', 'w_proj_fox', 'w_proj_mla', 'w_out', 'ln1_g', 'ln1_b',
                 'w_router', 'b_router', 'w_up', 'b_up', 'w_down', 'b_down', 'ln2_g', 'ln2_b')


def kernel(x, positions, w_in, b_merge, b_forget, conv_w, conv_b, w_rec_gate, b_rec_gate,
           w_inp_gate, b_inp_gate, lru_lambda, g_cq, g_ckv, w_uq, w_ukv, w_proj_lru,
           w_proj_fox, w_proj_mla, w_out, ln1_g, ln1_b, w_router, b_router, w_up, b_up,
           w_down, b_down, ln2_g, ln2_b):
    stacked = dict(zip(_LAYER_PARAMS, (
        w_in, b_merge, b_forget, conv_w, conv_b, w_rec_gate, b_rec_gate, w_inp_gate,
        b_inp_gate, lru_lambda, g_cq, g_ckv, w_uq, w_ukv, w_proj_lru, w_proj_fox,
        w_proj_mla, w_out, ln1_g, ln1_b, w_router, b_router, w_up, b_up, w_down, b_down,
        ln2_g, ln2_b)))
    batch, seq, _ = x.shape
    assert batch == 1
    xt = x.reshape(seq, D_MODEL)
    x_bf = xt.astype(BF16)
    tabs = _rope_tables(positions.reshape(seq))
    expert_stack = {'w_up', 'b_up', 'w_down', 'b_down'}
    shared = {k + '_all': stacked[k].reshape((-1,) + stacked[k].shape[2:]) for k in expert_stack}
    for l in range(w_in.shape[0]):
        p = {k: v[l] for k, v in stacked.items() if k not in expert_stack}
        xt, x_bf = _layer(xt, x_bf, tabs, dict(p, expert_base=l * N_EXPERTS, **shared))
    return xt.reshape(batch, seq, D_MODEL)
```

```python
import functools

import jax
import jax.numpy as jnp
import numpy as np
from jax import lax
from jax.experimental import pallas as pl
from jax.experimental.pallas import tpu as pltpu

F32 = jnp.float32
BF16 = jnp.bfloat16
I32 = jnp.int32

D_MODEL = 2048
N_BRANCHES = 3
DEEPNORM_ALPHA = (2 * 2) ** 0.25
LN_EPS = 1e-5
RMS_EPS = 1e-6
RNN_WIDTH = 1024
RNN_BLOCKS = 8
RNN_BLOCK_W = RNN_WIDTH // RNN_BLOCKS
CONV_WIDTH = 4
RG_LRU_C = 8.0
HEADS = 8
HEAD_DIM = 128
FOX_WIDTH = HEADS * HEAD_DIM
MLA_RANK = 512
MLA_ROPE_DIM = 64
MLA_QK_DIM = HEAD_DIM + MLA_ROPE_DIM
ATTN_QK_DIM = 256
MLA_PAD_DIM = ATTN_QK_DIM
LOG2E = float(np.log2(np.e))
ROPE_THETA = 10000.0
N_EXPERTS = 32
TOP_K = 4
EXPERT_FF = D_MODEL
SWIGLU_ALPHA = 1.702
SWIGLU_LIMIT = 7.0

LANES = 128
SUBLANES = 8
V7X_VMEM_BYTES = 64 * 1024 * 1024

ROW_BLOCK = 256
ATTN_BLOCK = 512
GATHER_UNROLL = 8
ATTN_GROUP = 4
MASK_VALUE = -1e30


def _params(semantics, vmem_mib):
    assert vmem_mib * 1024 * 1024 < V7X_VMEM_BYTES
    return pltpu.CompilerParams(dimension_semantics=semantics,
                                vmem_limit_bytes=vmem_mib * 1024 * 1024)


def _split3(x):
    hi = x.astype(BF16)
    r = x - hi.astype(F32)
    mid = r.astype(BF16)
    lo = (r - mid.astype(F32)).astype(BF16)
    return hi, mid, lo


def _sigmoid(x):
    return 1.0 / (1.0 + jnp.exp(-x))


def _softplus(z):
    return jnp.maximum(z, 0.0) + jnp.log1p(jnp.exp(-jnp.abs(z)))


def _layer_norm(v, g, b):
    mu = jnp.mean(v, axis=-1, keepdims=True)
    c = v - mu
    var = jnp.mean(c * c, axis=-1, keepdims=True)
    return c * lax.rsqrt(var + LN_EPS) * g + b


def _mm_kernel(x_ref, w_ref, o_ref):
    o_ref[...] = jnp.dot(x_ref[...], w_ref[...],
                         preferred_element_type=F32).astype(o_ref.dtype)


def _matmul(x, w, out_dtype, tm, tn, name):
    m, k = x.shape
    n = w.shape[1]
    tm, tn = min(tm, m), min(tn, n)
    return pl.pallas_call(
        _mm_kernel,
        out_shape=jax.ShapeDtypeStruct((m, n), out_dtype),
        grid=(m // tm, n // tn),
        in_specs=[pl.BlockSpec((tm, k), lambda i, j: (i, 0)),
                  pl.BlockSpec((k, tn), lambda i, j: (0, j))],
        out_specs=pl.BlockSpec((tm, tn), lambda i, j: (i, j)),
        compiler_params=_params(("parallel", "arbitrary"), 48),
        name=name,
    )(x, w)


def _in_proj_kernel(x_ref, w_ref, s_ref, o_ref, w_sc):
    @pl.when(pl.program_id(1) == 0)
    def _():
        w_sc[...] = (w_ref[0] * s_ref[...]).astype(BF16)

    o_ref[...] = jnp.dot(x_ref[...], w_sc[...],
                         preferred_element_type=F32).astype(o_ref.dtype)


def _in_proj(x, w_stack, layer, col0, ncols, col_scale, out_dtype, name):
    m, k = x.shape
    tm, tn = min(1024, m), 1024
    assert col0 % tn == 0 and ncols % tn == 0
    blk0 = col0 // tn
    return pl.pallas_call(
        _in_proj_kernel,
        out_shape=jax.ShapeDtypeStruct((m, ncols), out_dtype),
        grid=(ncols // tn, m // tm),
        in_specs=[pl.BlockSpec((tm, k), lambda j, i: (i, 0)),
                  pl.BlockSpec((1, k, tn), lambda j, i: (layer, 0, blk0 + j)),
                  pl.BlockSpec((1, tn), lambda j, i: (0, j))],
        out_specs=pl.BlockSpec((tm, tn), lambda j, i: (i, j)),
        scratch_shapes=[pltpu.VMEM((k, tn), BF16)],
        compiler_params=_params(("arbitrary", "arbitrary"), 48),
        name=name,
    )(x, w_stack, col_scale.reshape(1, ncols))


def _fox_cum_kernel(fl_ref, b_ref, o_ref, *, chunks):
    z = fl_ref[...] + b_ref[...]
    lf = jnp.minimum(z, 0.0) - jnp.log1p(jnp.exp(-jnp.abs(z)))
    rows = lf.shape[0]
    s = lax.broadcasted_iota(I32, (LANES, LANES), 0)
    t = lax.broadcasted_iota(I32, (LANES, LANES), 1)
    tri = (s <= t).astype(BF16)
    incl = None
    for part in _split3(lf):
        d = jnp.dot(part, tri, preferred_element_type=F32)
        incl = d if incl is None else incl + d
    i = lax.broadcasted_iota(I32, (rows, rows), 0)
    j = lax.broadcasted_iota(I32, (rows, rows), 1)
    lower = ((i // chunks == j // chunks) & (j < i)).astype(BF16)
    offs = None
    for part in _split3(incl):
        d = jnp.dot(lower, part, preferred_element_type=F32)
        offs = d if offs is None else offs + d
    o_ref[...] = incl + offs[:, LANES - 1:LANES]


def _fox_cum(f_logit, b_f):
    t = f_logit.shape[0]
    chunks = t // LANES
    fl = f_logit.T.reshape(HEADS * chunks, LANES)
    b = jnp.repeat(b_f.astype(F32), chunks).reshape(HEADS * chunks, 1)
    out = pl.pallas_call(
        functools.partial(_fox_cum_kernel, chunks=chunks),
        out_shape=jax.ShapeDtypeStruct((HEADS * chunks, LANES), F32),
        name="fox_cum",
    )(fl, b)
    return out.reshape(HEADS, t)


def _attn_kernel(qi_ref, ki_ref, q_ref, k_ref, v_ref, o_ref, m_sc, l_sc, acc_sc, *, group):
    p = pl.program_id(1)
    qi = qi_ref[p]
    ki = ki_ref[p]
    blk = q_ref.shape[0]

    @pl.when(ki == 0)
    def _():
        m_sc[...] = jnp.full(m_sc.shape, MASK_VALUE, F32)
        l_sc[...] = jnp.zeros(l_sc.shape, F32)
        acc_sc[...] = jnp.zeros(acc_sc.shape, F32)

    def step(masked):
        for g in range(group):
            qk = slice(g * ATTN_QK_DIM, (g + 1) * ATTN_QK_DIM)
            vo = slice(g * HEAD_DIM, (g + 1) * HEAD_DIM)
            s = lax.dot_general(q_ref[:, qk], k_ref[:, qk], (((1,), (1,)), ((), ())),
                                preferred_element_type=F32)
            if masked:
                row = lax.broadcasted_iota(I32, s.shape, 0)
                col = lax.broadcasted_iota(I32, s.shape, 1)
                s = jnp.where(col <= row, s, MASK_VALUE)
            chunks = [s[:, c * LANES:(c + 1) * LANES] for c in range(blk // LANES)]
            cmax = functools.reduce(jnp.maximum, chunks)
            m_prev = m_sc[g]
            m_new = jnp.maximum(m_prev, jnp.max(cmax, axis=1, keepdims=True))
            alpha = jnp.exp2(m_prev - m_new)
            probs = [jnp.exp2(c - m_new) for c in chunks]
            l_sc[g] = alpha * l_sc[g] + functools.reduce(jnp.add, probs)
            pb = jnp.concatenate([pr.astype(BF16) for pr in probs], axis=1)
            acc_sc[g] = alpha * acc_sc[g] + jnp.dot(pb, v_ref[:, vo],
                                                    preferred_element_type=F32)
            m_sc[g] = m_new

    @pl.when(ki < qi)
    def _():
        step(False)

    @pl.when(ki == qi)
    def _():
        step(True)
        for g in range(group):
            denom = jnp.sum(l_sc[g], axis=1, keepdims=True)
            o_ref[:, g * HEAD_DIM:(g + 1) * HEAD_DIM] = (acc_sc[g] / denom).astype(o_ref.dtype)


def _causal_attention(q, k, v, v_col, name):
    t = q.shape[0]
    blk = min(ATTN_BLOCK, t)
    nq = t // blk
    group = ATTN_GROUP
    pairs = [(a, b) for a in range(nq) for b in range(a + 1)]
    qi_tab = jnp.asarray(np.array([a for a, _ in pairs], np.int32))
    ki_tab = jnp.asarray(np.array([b for _, b in pairs], np.int32))
    v_blk = v_col // (group * HEAD_DIM)
    return pl.pallas_call(
        functools.partial(_attn_kernel, group=group),
        out_shape=jax.ShapeDtypeStruct((t, HEADS * HEAD_DIM), BF16),
        grid_spec=pltpu.PrefetchScalarGridSpec(
            num_scalar_prefetch=2,
            grid=(HEADS // group, len(pairs)),
            in_specs=[
                pl.BlockSpec((blk, group * ATTN_QK_DIM), lambda h, p, qi, ki: (qi[p], h)),
                pl.BlockSpec((blk, group * ATTN_QK_DIM), lambda h, p, qi, ki: (ki[p], h)),
                pl.BlockSpec((blk, group * HEAD_DIM), lambda h, p, qi, ki: (ki[p], v_blk + h)),
            ],
            out_specs=pl.BlockSpec((blk, group * HEAD_DIM), lambda h, p, qi, ki: (qi[p], h)),
            scratch_shapes=[pltpu.VMEM((group, blk, LANES), F32),
                            pltpu.VMEM((group, blk, LANES), F32),
                            pltpu.VMEM((group, blk, HEAD_DIM), F32)],
        ),
        compiler_params=_params(("parallel", "arbitrary"), 32),
        name=name,
    )(qi_tab, ki_tab, q, k, v)


def _fox_prep_kernel(q_ref, k_ref, c_ref, sel_ref, qo_ref, ko_ref):
    bias = None
    for j, part in enumerate(_split3(c_ref[...])):
        d = jnp.dot(part, sel_ref[j], preferred_element_type=F32)
        bias = d if bias is None else bias + d
    lane = lax.broadcasted_iota(I32, (q_ref.shape[0], LANES), 1)
    for h in range(HEADS):
        src = slice(h * HEAD_DIM, (h + 1) * HEAD_DIM)
        lo = h * ATTN_QK_DIM
        g = bias[:, src]
        qo_ref[:, lo:lo + HEAD_DIM] = q_ref[:, src]
        ko_ref[:, lo:lo + HEAD_DIM] = k_ref[:, src]
        qo_ref[:, lo + HEAD_DIM:lo + ATTN_QK_DIM] = jnp.where(
            (lane >= 3) & (lane < 6), 1.0, g).astype(BF16)
        ko_ref[:, lo + HEAD_DIM:lo + ATTN_QK_DIM] = jnp.where(
            lane < 3, 1.0, -pltpu.roll(g, 3, axis=1)).astype(BF16)


def _fox_prep(u_b, cum):
    t = u_b.shape[0]
    tm = min(512, t)
    c = jnp.pad(cum.T * LOG2E, ((0, 0), (0, LANES - HEADS)))
    sel = np.zeros((3, LANES, FOX_WIDTH), np.float32)
    for j in range(3):
        for h in range(HEADS):
            sel[j, h, h * HEAD_DIM + j] = 1.0
    wide = HEADS * ATTN_QK_DIM
    return pl.pallas_call(
        _fox_prep_kernel,
        out_shape=[jax.ShapeDtypeStruct((t, wide), BF16)] * 2,
        grid=(t // tm,),
        in_specs=[pl.BlockSpec((tm, FOX_WIDTH), lambda i: (i, 0)),
                  pl.BlockSpec((tm, FOX_WIDTH), lambda i: (i, 1)),
                  pl.BlockSpec((tm, LANES), lambda i: (i, 0)),
                  pl.BlockSpec((3, LANES, FOX_WIDTH), lambda i: (0, 0, 0))],
        out_specs=[pl.BlockSpec((tm, wide), lambda i: (i, 0))] * 2,
        compiler_params=_params(("arbitrary",), 32),
        name="fox_prep",
    )(u_b, u_b, c, jnp.asarray(sel, BF16))


def _rope_table_kernel(pos_ref, inv_ref, c_ref, s1_ref, s2_ref):
    half = MLA_ROPE_DIM // 2
    ang = pos_ref[...].astype(F32) * inv_ref[...]
    c = jnp.cos(ang)
    s = jnp.sin(ang)
    lane = lax.broadcasted_iota(I32, ang.shape, 1)
    c_ref[...] = jnp.where(lane < MLA_ROPE_DIM, c, 0.0)
    s1_ref[...] = jnp.where(lane < half, -s, 0.0)
    s2_ref[...] = jnp.where((lane >= half) & (lane < MLA_ROPE_DIM), s, 0.0)


def _rope_tables(positions):
    t = positions.shape[0]
    half = MLA_ROPE_DIM // 2
    inv_freq = ROPE_THETA ** (-jnp.arange(half, dtype=F32) / half)
    inv = jnp.concatenate([inv_freq, inv_freq, jnp.zeros((LANES - 2 * half,), F32)])
    tm = min(1024, t)
    spec = pl.BlockSpec((tm, LANES), lambda i: (i, 0))
    return pl.pallas_call(
        _rope_table_kernel,
        out_shape=[jax.ShapeDtypeStruct((t, LANES), F32)] * 3,
        grid=(t // tm,),
        in_specs=[pl.BlockSpec((tm, 1), lambda i: (i, 0)),
                  pl.BlockSpec((1, LANES), lambda i: (0, 0))],
        out_specs=[spec, spec, spec],
        name="rope_tables",
    )(positions.reshape(t, 1), inv.reshape(1, LANES))


def _rope_group(g, c, s1, s2):
    half = MLA_ROPE_DIM // 2
    return (g * c + pltpu.roll(g, LANES - half, axis=1) * s1
            + pltpu.roll(g, half, axis=1) * s2)


def _mla_prep_kernel(u_ref, gq_ref, gkv_ref, wq_ref, wkv_ref, c_ref, s1_ref, s2_ref,
                     q_ref, k_ref, v_ref):
    def rms(v, g):
        ms = jnp.mean(v * v, axis=-1, keepdims=True)
        return (v * lax.rsqrt(ms + RMS_EPS) * g).astype(BF16)

    c, s1, s2 = c_ref[...], s1_ref[...], s2_ref[...]
    cq = rms(u_ref[:, 0:MLA_RANK], gq_ref[...])
    ckv = rms(u_ref[:, MLA_RANK:2 * MLA_RANK], gkv_ref[...])
    q_pre = jnp.dot(cq, wq_ref[...], preferred_element_type=F32)
    kv = jnp.dot(ckv, wkv_ref[...], preferred_element_type=F32)
    k_rot = _rope_group(u_ref[:, 2 * MLA_RANK:2 * MLA_RANK + LANES], c, s1, s2).astype(BF16)
    for h in range(HEADS):
        lo = h * MLA_PAD_DIM
        mid = lo + HEAD_DIM
        hi = lo + MLA_PAD_DIM
        q_ref[:, lo:mid] = q_pre[:, lo:mid].astype(BF16)
        q_ref[:, mid:hi] = _rope_group(q_pre[:, mid:hi], c, s1, s2).astype(BF16)
        k_ref[:, lo:mid] = kv[:, lo:mid].astype(BF16)
        k_ref[:, mid:hi] = k_rot
        v_ref[:, h * HEAD_DIM:(h + 1) * HEAD_DIM] = kv[:, mid:hi].astype(BF16)


def _mla_prep(u_small, g_cq, g_ckv, w_uq_r, w_ukv, tabs):
    t = u_small.shape[0]
    tm = min(512, t)
    wide = HEADS * MLA_PAD_DIM
    row = lambda w: pl.BlockSpec((tm, w), lambda i: (i, 0))
    const = lambda a: pl.BlockSpec(a.shape, lambda i: (0, 0))
    return pl.pallas_call(
        _mla_prep_kernel,
        out_shape=[jax.ShapeDtypeStruct((t, wide), BF16),
                   jax.ShapeDtypeStruct((t, wide), BF16),
                   jax.ShapeDtypeStruct((t, HEADS * HEAD_DIM), BF16)],
        grid=(t // tm,),
        in_specs=[row(u_small.shape[1]), const(g_cq), const(g_ckv), const(w_uq_r),
                  const(w_ukv), row(LANES), row(LANES), row(LANES)],
        out_specs=[row(wide), row(wide), row(HEADS * HEAD_DIM)],
        compiler_params=_params(("arbitrary",), 48),
        name="mla_prep",
    )(u_small, g_cq, g_ckv, w_uq_r, w_ukv, *tabs)


def _rglru_kernel(y_ref, x_ref, cw_ref, cb_ref, wr_ref, br_ref, wi_ref, bi_ref, lam_ref,
                  o_ref, xprev_sc, h_sc, a_sc, g_sc, hs_sc):
    tt = x_ref.shape[0]

    @pl.when(pl.program_id(0) == 0)
    def _():
        xprev_sc[...] = jnp.zeros(xprev_sc.shape, F32)
        h_sc[...] = jnp.zeros(h_sc.shape, F32)

    x = x_ref[...]
    xext = jnp.concatenate([xprev_sc[...], x], axis=0)
    xprev_sc[...] = x[tt - SUBLANES:, :]
    xc = cb_ref[...]
    for j in range(CONV_WIDTH):
        off = SUBLANES - (CONV_WIDTH - 1) + j
        xc = xc + cw_ref[j:j + 1, :] * xext[off:off + tt, :]

    r_parts, i_parts = [], []
    for n in range(RNN_BLOCKS):
        sl = slice(n * RNN_BLOCK_W, (n + 1) * RNN_BLOCK_W)
        xb = xc[:, sl].astype(BF16)
        r_parts.append(_sigmoid(jnp.dot(xb, wr_ref[n], preferred_element_type=F32)
                                + br_ref[:, sl]))
        i_parts.append(_sigmoid(jnp.dot(xb, wi_ref[n], preferred_element_type=F32)
                                + bi_ref[:, sl]))
    r = jnp.concatenate(r_parts, axis=1)
    gate_i = jnp.concatenate(i_parts, axis=1)
    log_a = (-RG_LRU_C * _softplus(-lam_ref[...])) * r
    a = jnp.exp(log_a)
    a_sc[...] = a
    g_sc[...] = jnp.sqrt(-jnp.tanh(log_a) * (a * a + 1.0)) * (gate_i * xc)

    sub = lax.broadcasted_iota(I32, (SUBLANES, RNN_WIDTH), 0)

    def tile_scan(k, h_in):
        rows = pl.ds(pl.multiple_of(k * SUBLANES, SUBLANES), SUBLANES)
        a = a_sc[rows, :]
        g = g_sc[rows, :]
        for d in (1, 2, 4):
            keep = sub >= d
            g = jnp.where(keep, a * pltpu.roll(g, d, axis=0) + g, g)
            a = jnp.where(keep, a * pltpu.roll(a, d, axis=0), a)
        hs = a * h_in + g
        hs_sc[rows, :] = hs
        return jnp.broadcast_to(hs[SUBLANES - 1:SUBLANES, :], (SUBLANES, RNN_WIDTH))

    h_sc[...] = lax.fori_loop(0, tt // SUBLANES, tile_scan, h_sc[...])

    y = y_ref[...]
    gelu = 0.5 * y * (1.0 + jnp.tanh(np.sqrt(2.0 / np.pi) * (y + 0.044715 * (y * y * y))))
    o_ref[...] = (hs_sc[...] * gelu).astype(o_ref.dtype)


def _rglru(u_a, y_blk, x_blk, conv_w, conv_b, w_rg, b_rg, w_ig, b_ig, lam):
    t = u_a.shape[0]
    tt = min(256, t)
    const2 = lambda a: pl.BlockSpec(a.shape, lambda i: (0, 0))
    const3 = lambda a: pl.BlockSpec(a.shape, lambda i: (0, 0, 0))
    return pl.pallas_call(
        _rglru_kernel,
        out_shape=jax.ShapeDtypeStruct((t, RNN_WIDTH), BF16),
        grid=(t // tt,),
        in_specs=[pl.BlockSpec((tt, RNN_WIDTH), lambda i: (i, y_blk)),
                  pl.BlockSpec((tt, RNN_WIDTH), lambda i: (i, x_blk)),
                  const2(conv_w), const2(conv_b), const3(w_rg), const2(b_rg),
                  const3(w_ig), const2(b_ig), const2(lam)],
        out_specs=pl.BlockSpec((tt, RNN_WIDTH), lambda i: (i, 0)),
        scratch_shapes=[pltpu.VMEM((SUBLANES, RNN_WIDTH), F32),
                        pltpu.VMEM((SUBLANES, RNN_WIDTH), F32),
                        pltpu.VMEM((tt, RNN_WIDTH), F32),
                        pltpu.VMEM((tt, RNN_WIDTH), F32),
                        pltpu.VMEM((tt, RNN_WIDTH), F32)],
        compiler_params=_params(("arbitrary",), 32),
        name="rglru",
    )(u_a, u_a, conv_w, conv_b, w_rg, b_rg, w_ig, b_ig, lam)


def _merge_kernel(a_ref, b_ref, c_ref, wa_ref, wb_ref, wc_ref, g0_ref, g1_ref, g2_ref,
                  bm_ref, o_ref):
    def branch(x_ref, w_ref, g_ref, n):
        y = jnp.dot(x_ref[...], w_ref[...], preferred_element_type=F32)
        return _sigmoid(g_ref[...] + bm_ref[n:n + 1, :]) * y

    mixed = (branch(a_ref, wa_ref, g0_ref, 0) + branch(b_ref, wb_ref, g1_ref, 1)
             + branch(c_ref, wc_ref, g2_ref, 2))
    o_ref[...] = mixed.astype(o_ref.dtype)


def _merge(ya, yb, yc, wa, wb, wc, u_a, b_merge):
    t = ya.shape[0]
    tm, tn = min(512, t), 1024
    nn = D_MODEL // tn
    xin = pl.BlockSpec((tm, ya.shape[1]), lambda j, i: (i, 0))
    win = pl.BlockSpec((ya.shape[1], tn), lambda j, i: (0, j))
    gate = lambda n: pl.BlockSpec((tm, tn), lambda j, i: (i, n * nn + j))
    return pl.pallas_call(
        _merge_kernel,
        out_shape=jax.ShapeDtypeStruct((t, D_MODEL), BF16),
        grid=(nn, t // tm),
        in_specs=[xin, xin, xin, win, win, win, gate(0), gate(1), gate(2),
                  pl.BlockSpec((N_BRANCHES, tn), lambda j, i: (0, j))],
        out_specs=pl.BlockSpec((tm, tn), lambda j, i: (i, j)),
        compiler_params=_params(("arbitrary", "arbitrary"), 48),
        name="merge",
    )(ya, yb, yc, wa, wb, wc, u_a, u_a, u_a, b_merge)


def _outproj_ln_kernel(m_ref, w_ref, x_ref, g_ref, b_ref, of_ref, ob_ref):
    mix = jnp.dot(m_ref[...], w_ref[...], preferred_element_type=F32)
    y = _layer_norm(DEEPNORM_ALPHA * x_ref[...] + mix, g_ref[...], b_ref[...])
    of_ref[...] = y
    ob_ref[...] = y.astype(BF16)


def _outproj_ln(mixed, w_out, x, g, b):
    t = x.shape[0]
    tm = min(256, t)
    row = pl.BlockSpec((tm, D_MODEL), lambda i: (i, 0))
    vec = pl.BlockSpec((1, D_MODEL), lambda i: (0, 0))
    return pl.pallas_call(
        _outproj_ln_kernel,
        out_shape=[jax.ShapeDtypeStruct((t, D_MODEL), F32),
                   jax.ShapeDtypeStruct((t, D_MODEL), BF16)],
        grid=(t // tm,),
        in_specs=[row, pl.BlockSpec((D_MODEL, D_MODEL), lambda i: (0, 0)), row, vec, vec],
        out_specs=[row, row],
        compiler_params=_params(("arbitrary",), 48),
        name="outproj_ln",
    )(mixed, w_out, x, g, b)


def _router_kernel(x_ref, w_ref, b_ref, idx_ref, gate_ref, rank_ref, count_ref, count_sc):
    nt = (((1,), (1,)), ((), ()))
    x = x_ref[...]
    x_hi = x.astype(BF16)
    x_lo = (x - x_hi.astype(F32)).astype(BF16)
    w = w_ref[...]
    w_hi = w.astype(BF16)
    w_lo = (w - w_hi.astype(F32)).astype(BF16)
    logits = (lax.dot_general(w_hi, x_hi, nt, preferred_element_type=F32)
              + lax.dot_general(w_hi, x_lo, nt, preferred_element_type=F32)
              + lax.dot_general(w_lo, x_hi, nt, preferred_element_type=F32)
              + b_ref[...])
    eidx = lax.broadcasted_iota(I32, logits.shape, 0)
    vals, idxs = [], []
    for _ in range(TOP_K):
        m = jnp.max(logits, axis=0, keepdims=True)
        idx = jnp.min(jnp.where(logits == m, eidx, N_EXPERTS), axis=0, keepdims=True)
        vals.append(m)
        idxs.append(idx)
        logits = jnp.where(eidx == idx, -jnp.inf, logits)
    exps = [jnp.exp(v - vals[0]) for v in vals]
    denom = exps[0] + exps[1] + exps[2] + exps[3]
    pad = SUBLANES - TOP_K
    tokens = logits.shape[1]
    idx_ref[...] = jnp.concatenate(idxs + [jnp.zeros((pad, tokens), I32)], axis=0)
    gate_ref[...] = jnp.concatenate([e / denom for e in exps]
                                    + [jnp.zeros((pad, tokens), F32)], axis=0)

    @pl.when(pl.program_id(0) == 0)
    def _():
        count_sc[...] = jnp.zeros(count_sc.shape, F32)

    src = lax.broadcasted_iota(I32, (tokens, tokens), 0)
    dst = lax.broadcasted_iota(I32, (tokens, tokens), 1)
    before = (src < dst).astype(BF16)
    seen = count_sc[...]
    ranks = []
    for idx in idxs:
        hit = eidx == idx
        prefix = jnp.dot(hit.astype(BF16), before, preferred_element_type=F32)
        ranks.append(jnp.sum(jnp.where(hit, seen + prefix, 0.0), axis=0, keepdims=True))
        seen = seen + jnp.sum(hit.astype(F32), axis=1, keepdims=True)
    count_sc[...] = seen
    rank_ref[...] = jnp.concatenate(ranks + [jnp.zeros((pad, tokens), F32)],
                                    axis=0).astype(I32)
    count_ref[...] = seen.astype(I32)


def _router(x, w_router_t, b_router):
    t = x.shape[0]
    tm = min(1024, t)
    tok = pl.BlockSpec((SUBLANES, tm), lambda i: (0, i))
    idx, gate, rank, count = pl.pallas_call(
        _router_kernel,
        out_shape=[jax.ShapeDtypeStruct((SUBLANES, t), I32),
                   jax.ShapeDtypeStruct((SUBLANES, t), F32),
                   jax.ShapeDtypeStruct((SUBLANES, t), I32),
                   jax.ShapeDtypeStruct((N_EXPERTS, 1), I32)],
        grid=(t // tm,),
        in_specs=[pl.BlockSpec((tm, D_MODEL), lambda i: (i, 0)),
                  pl.BlockSpec((N_EXPERTS, D_MODEL), lambda i: (0, 0)),
                  pl.BlockSpec((N_EXPERTS, 1), lambda i: (0, 0))],
        out_specs=[tok, tok, tok, pl.BlockSpec((N_EXPERTS, 1), lambda i: (0, 0))],
        scratch_shapes=[pltpu.VMEM((N_EXPERTS, 1), F32)],
        compiler_params=_params(("arbitrary",), 48),
        name="router",
    )(x, w_router_t, b_router.reshape(N_EXPERTS, 1))
    return idx[:TOP_K].T, gate[:TOP_K].T, rank[:TOP_K].T, count[:, 0]


def _row_copy(src_hbm, row, dst, dst_row, sem):
    return pltpu.make_async_copy(src_hbm.at[pl.ds(row, 1), :],
                                 dst.at[pl.ds(dst_row, 1), :], sem)


def _start_rows(src_hbm, idx_ref, idx0, stride, dst, sem):
    for r in range(dst.shape[0]):
        _row_copy(src_hbm, idx_ref[idx0 + r * stride], dst, r, sem).start()


def _wait_rows(src_hbm, dst, sem):
    pltpu.make_async_copy(src_hbm.at[pl.ds(0, dst.shape[0]), :], dst, sem).wait()


def _expert_changed(be_ref, b):
    return (b == 0) | (be_ref[b] != be_ref[jnp.maximum(b - 1, 0)])


def _expert_up_kernel(be_ref, nused_ref, tok_ref, x_hbm, wg_ref, wl_ref, bg_ref, bl_ref, o_ref,
                      wg_sc, wl_sc, rows_a, rows_b, sem):
    f = pl.program_id(0)
    b = pl.program_id(1)
    n_used = nused_ref[0]
    rb = rows_a.shape[0]
    parity = (f * n_used + b) % 2

    @pl.when((f == 0) & (b == 0))
    def _():
        _start_rows(x_hbm, tok_ref, 0, 1, rows_a, sem.at[0])

    @pl.when(_expert_changed(be_ref, b))
    def _():
        wg_sc[...] = wg_ref[0].astype(BF16)
        wl_sc[...] = wl_ref[0].astype(BF16)

    def block(cur, cur_sem, nxt, nxt_sem):
        _wait_rows(x_hbm, cur, cur_sem)
        nxt_blk = jnp.where(b + 1 < n_used, b + 1, 0)
        _start_rows(x_hbm, tok_ref, nxt_blk * rb, 1, nxt, nxt_sem)
        x = cur[...].astype(BF16)
        h_glu = jnp.dot(x, wg_sc[...], preferred_element_type=F32) + bg_ref[0]
        h_lin = jnp.dot(x, wl_sc[...], preferred_element_type=F32) + bl_ref[0]
        h_glu = jnp.minimum(h_glu, SWIGLU_LIMIT)
        h_lin = jnp.clip(h_lin, -SWIGLU_LIMIT, SWIGLU_LIMIT)
        act = h_glu * _sigmoid(SWIGLU_ALPHA * h_glu) * (h_lin + 1.0)
        o_ref[...] = act.astype(o_ref.dtype)

    @pl.when((b < n_used) & (parity == 0))
    def _():
        block(rows_a, sem.at[0], rows_b, sem.at[1])

    @pl.when((b < n_used) & (parity == 1))
    def _():
        block(rows_b, sem.at[1], rows_a, sem.at[0])

    @pl.when(b >= n_used)
    def _():
        o_ref[...] = jnp.zeros(o_ref.shape, o_ref.dtype)

    @pl.when((f == pl.num_programs(0) - 1) & (b == pl.num_programs(1) - 1))
    def _():
        spare = (pl.num_programs(0) * n_used) % 2

        @pl.when(spare == 0)
        def _():
            _wait_rows(x_hbm, rows_a, sem.at[0])

        @pl.when(spare == 1)
        def _():
            _wait_rows(x_hbm, rows_b, sem.at[1])


def _expert_up(x, row_tok, block_expert, n_used, w_up, b_up):
    n_rows = row_tok.shape[0]
    tn = 1024
    nf = EXPERT_FF // tn
    b_up3 = b_up.reshape(b_up.shape[0], 1, 2 * EXPERT_FF)
    return pl.pallas_call(
        _expert_up_kernel,
        out_shape=jax.ShapeDtypeStruct((n_rows, EXPERT_FF), BF16),
        grid_spec=pltpu.PrefetchScalarGridSpec(
            num_scalar_prefetch=3,
            grid=(nf, n_rows // ROW_BLOCK),
            in_specs=[
                pl.BlockSpec(memory_space=pl.ANY),
                pl.BlockSpec((1, D_MODEL, tn), lambda f, b, be, nu, tok: (be[b], 0, f)),
                pl.BlockSpec((1, D_MODEL, tn), lambda f, b, be, nu, tok: (be[b], 0, nf + f)),
                pl.BlockSpec((1, 1, tn), lambda f, b, be, nu, tok: (be[b], 0, f)),
                pl.BlockSpec((1, 1, tn), lambda f, b, be, nu, tok: (be[b], 0, nf + f)),
            ],
            out_specs=pl.BlockSpec((ROW_BLOCK, tn), lambda f, b, be, nu, tok: (b, f)),
            scratch_shapes=[pltpu.VMEM((D_MODEL, tn), BF16),
                            pltpu.VMEM((D_MODEL, tn), BF16),
                            pltpu.VMEM((ROW_BLOCK, D_MODEL), F32),
                            pltpu.VMEM((ROW_BLOCK, D_MODEL), F32),
                            pltpu.SemaphoreType.DMA((2,))],
        ),
        compiler_params=_params(("arbitrary", "arbitrary"), 58),
        name="expert_up",
    )(block_expert, n_used, row_tok, x, w_up, w_up, b_up3, b_up3)


def _expert_down_kernel(be_ref, nused_ref, act_ref, w_ref, bias_ref, o_ref, w_sc):
    b = pl.program_id(1)

    @pl.when(_expert_changed(be_ref, b))
    def _():
        w_sc[...] = w_ref[0].astype(BF16)

    @pl.when(b < nused_ref[0])
    def _():
        o_ref[...] = (jnp.dot(act_ref[...], w_sc[...], preferred_element_type=F32)
                      + bias_ref[0])

    @pl.when(b >= nused_ref[0])
    def _():
        o_ref[...] = jnp.zeros(o_ref.shape, o_ref.dtype)


def _expert_down(act, block_expert, n_used, w_down, b_down):
    n_rows = act.shape[0]
    tn = 1024
    b_down3 = b_down.reshape(b_down.shape[0], 1, D_MODEL)
    return pl.pallas_call(
        _expert_down_kernel,
        out_shape=jax.ShapeDtypeStruct((n_rows, D_MODEL), F32),
        grid_spec=pltpu.PrefetchScalarGridSpec(
            num_scalar_prefetch=2,
            grid=(D_MODEL // tn, n_rows // ROW_BLOCK),
            in_specs=[
                pl.BlockSpec((ROW_BLOCK, EXPERT_FF), lambda n, b, be, nu: (b, 0)),
                pl.BlockSpec((1, EXPERT_FF, tn), lambda n, b, be, nu: (be[b], 0, n)),
                pl.BlockSpec((1, 1, tn), lambda n, b, be, nu: (be[b], 0, n)),
            ],
            out_specs=pl.BlockSpec((ROW_BLOCK, tn), lambda n, b, be, nu: (b, n)),
            scratch_shapes=[pltpu.VMEM((EXPERT_FF, tn), BF16)],
        ),
        compiler_params=_params(("arbitrary", "arbitrary"), 48),
        name="expert_down",
    )(block_expert, n_used, act, w_down, b_down3)


def _combine_ln_kernel(pos_ref, y_hbm, gate_ref, x_ref, g_ref, b_ref, of_ref, ob_ref,
                       buf, sem):
    tc = x_ref.shape[0]
    i = pl.program_id(0)
    last = pl.num_programs(0) - 1
    cur = i % 2

    def start(tile, slot):
        for k in range(TOP_K):
            _start_rows(y_hbm, pos_ref, tile * (tc * TOP_K) + k, TOP_K, buf.at[slot, k],
                        sem.at[slot])

    def wait(slot):
        for k in range(TOP_K):
            _wait_rows(y_hbm, buf.at[slot, k], sem.at[slot])

    @pl.when(i == 0)
    def _():
        start(0, 0)

    wait(cur)
    start(jnp.where(i < last, i + 1, 0), 1 - cur)
    gate = gate_ref[...]
    ffn = gate[:, 0:1] * buf[cur, 0]
    for k in range(1, TOP_K):
        ffn = ffn + gate[:, k:k + 1] * buf[cur, k]
    y = _layer_norm(DEEPNORM_ALPHA * x_ref[...] + ffn, g_ref[...], b_ref[...])
    of_ref[...] = y
    ob_ref[...] = y.astype(BF16)

    @pl.when(i == last)
    def _():
        wait(1 - cur)


def _combine_ln(y_rows, pos, gate, x, g, b):
    t = x.shape[0]
    tc = min(128, t)
    row = lambda b_, p: (b_, 0)
    return pl.pallas_call(
        _combine_ln_kernel,
        out_shape=[jax.ShapeDtypeStruct((t, D_MODEL), F32),
                   jax.ShapeDtypeStruct((t, D_MODEL), BF16)],
        grid_spec=pltpu.PrefetchScalarGridSpec(
            num_scalar_prefetch=1,
            grid=(t // tc,),
            in_specs=[pl.BlockSpec(memory_space=pl.ANY),
                      pl.BlockSpec((tc, TOP_K), row),
                      pl.BlockSpec((tc, D_MODEL), row),
                      pl.BlockSpec((1, D_MODEL), lambda b_, p: (0, 0)),
                      pl.BlockSpec((1, D_MODEL), lambda b_, p: (0, 0))],
            out_specs=[pl.BlockSpec((tc, D_MODEL), row), pl.BlockSpec((tc, D_MODEL), row)],
            scratch_shapes=[pltpu.VMEM((2, TOP_K, tc, D_MODEL), F32),
                            pltpu.SemaphoreType.DMA((2,))],
        ),
        compiler_params=_params(("arbitrary",), 32),
        name="combine_ln",
    )(pos, y_rows, gate, x, g, b)


def _dispatch_plan(top_idx, rank, counts):
    t = top_idx.shape[0]
    n_assign = t * TOP_K
    n_blocks = -(-n_assign // ROW_BLOCK) + N_EXPERTS
    e_flat = top_idx.reshape(-1)
    padded = (counts + ROW_BLOCK - 1) // ROW_BLOCK * ROW_BLOCK
    pend = jnp.cumsum(padded)
    pstart = pend - padded
    dest = (pstart[e_flat] + rank.reshape(-1)).astype(I32)
    row_tok = jnp.zeros((n_blocks * ROW_BLOCK,), I32).at[dest].set(
        jnp.arange(n_assign, dtype=I32) // TOP_K)
    block_row0 = jnp.arange(n_blocks, dtype=I32) * ROW_BLOCK
    block_expert = jnp.minimum(jnp.sum((pend[None, :] <= block_row0[:, None]).astype(I32), axis=1),
                               N_EXPERTS - 1)
    n_used = (pend[-1:] // ROW_BLOCK).astype(I32)
    return dest, row_tok, block_expert, n_used


IN_FOX = N_BRANCHES * D_MODEL + 2 * RNN_WIDTH
IN_TAIL = IN_FOX + 3 * FOX_WIDTH


def _prep_in_tail(w_in, layer):
    tail = w_in[layer, :, IN_TAIL:]
    pad = LANES - MLA_ROPE_DIM - HEADS
    return jnp.concatenate([tail[:, HEADS:], tail[:, :HEADS], jnp.zeros((D_MODEL, pad), F32)],
                           axis=1).astype(BF16)


def _prep_uq(w_uq_l):
    w = w_uq_l.reshape(MLA_RANK, HEADS, MLA_QK_DIM) * (MLA_QK_DIM ** -0.5 * LOG2E)
    pad = jnp.zeros((MLA_RANK, HEADS, MLA_PAD_DIM - MLA_QK_DIM), F32)
    return jnp.concatenate([w, pad], axis=2).reshape(MLA_RANK, HEADS * MLA_PAD_DIM).astype(BF16)


def _layer(x, x_bf, tabs, p):
    t = x.shape[0]
    w_in, layer = p['w_in_all'], p['layer']
    u_a = _in_proj(x_bf, w_in, layer, 0, IN_FOX, jnp.ones((IN_FOX,), F32), F32,
                   "in_proj_gates_lru")
    q_scale = jnp.concatenate([jnp.full((FOX_WIDTH,), HEAD_DIM ** -0.5 * LOG2E, F32),
                               jnp.ones((2 * FOX_WIDTH,), F32)])
    u_b = _in_proj(x_bf, w_in, layer, IN_FOX, 3 * FOX_WIDTH, q_scale, BF16, "in_proj_fox")
    w_c = _prep_in_tail(w_in, layer)
    u_c = _matmul(x_bf, w_c, F32, 1024, w_c.shape[1], "in_proj_small")

    gate_blocks = N_BRANCHES * D_MODEL // RNN_WIDTH
    y_a = _rglru(u_a, gate_blocks, gate_blocks + 1,
                 p['conv_w'], p['conv_b'].reshape(1, RNN_WIDTH),
                 p['w_rec_gate'].astype(BF16), p['b_rec_gate'].reshape(1, RNN_WIDTH),
                 p['w_inp_gate'].astype(BF16), p['b_inp_gate'].reshape(1, RNN_WIDTH),
                 p['lru_lambda'].reshape(1, RNN_WIDTH))

    fl_col = 2 * MLA_RANK + MLA_ROPE_DIM
    cum = _fox_cum(u_c[:, fl_col:fl_col + HEADS], p['b_forget'])
    q_x, k_x = _fox_prep(u_b, cum)
    y_b = _causal_attention(q_x, k_x, u_b, 2 * FOX_WIDTH, "fox_attention")

    q_f, k_f, v_c = _mla_prep(u_c, p['g_cq'].reshape(1, MLA_RANK),
                              p['g_ckv'].reshape(1, MLA_RANK), _prep_uq(p['w_uq']),
                              p['w_ukv'].astype(BF16), tabs)
    y_c = _causal_attention(q_f, k_f, v_c, 0, "mla_attention")

    mixed = _merge(y_a, y_b, y_c, p['w_proj_lru'].astype(BF16), p['w_proj_fox'].astype(BF16),
                   p['w_proj_mla'].astype(BF16), u_a, p['b_merge'])
    x, x_bf = _outproj_ln(mixed, p['w_out'].astype(BF16), x,
                          p['ln1_g'].reshape(1, D_MODEL), p['ln1_b'].reshape(1, D_MODEL))

    top_idx, gate, rank, counts = _router(x, p['w_router'].T, p['b_router'])
    dest, row_tok, block_expert, n_used = _dispatch_plan(top_idx, rank, counts)
    block_expert = block_expert + p['expert_base']
    act = _expert_up(x, row_tok, block_expert, n_used, p['w_up_all'], p['b_up_all'])
    y_rows = _expert_down(act, block_expert, n_used, p['w_down_all'], p['b_down_all'])
    del t
    return _combine_ln(y_rows, dest, gate, x, p['ln2_g'].reshape(1, D_MODEL),
                       p['ln2_b'].reshape(1, D_MODEL))


_LAYER_PARAMS = ('w_in', 'b_merge', 'b_forget', 'conv_w', 'conv_b', 'w_rec_gate', 'b_rec_gate',
                 'w_inp_gate', 'b_inp_gate', 'lru_lambda', 'g_cq', 'g_ckv', 'w_uq', 'w_ukv',
                 'w_proj_lru', 'w_proj_fox', 'w_proj_mla', 'w_out', 'ln1_g', 'ln1_b',
                 'w_router', 'b_router', 'w_up', 'b_up', 'w_down', 'b_down', 'ln2_g', 'ln2_b')


def kernel(x, positions, w_in, b_merge, b_forget, conv_w, conv_b, w_rec_gate, b_rec_gate,
           w_inp_gate, b_inp_gate, lru_lambda, g_cq, g_ckv, w_uq, w_ukv, w_proj_lru,
           w_proj_fox, w_proj_mla, w_out, ln1_g, ln1_b, w_router, b_router, w_up, b_up,
           w_down, b_down, ln2_g, ln2_b):
    stacked = dict(zip(_LAYER_PARAMS, (
        w_in, b_merge, b_forget, conv_w, conv_b, w_rec_gate, b_rec_gate, w_inp_gate,
        b_inp_gate, lru_lambda, g_cq, g_ckv, w_uq, w_ukv, w_proj_lru, w_proj_fox,
        w_proj_mla, w_out, ln1_g, ln1_b, w_router, b_router, w_up, b_up, w_down, b_down,
        ln2_g, ln2_b)))
    batch, seq, _ = x.shape
    assert batch == 1
    xt = x.reshape(seq, D_MODEL)
    x_bf = xt.astype(BF16)
    tabs = _rope_tables(positions.reshape(seq))
    expert_stack = {'w_up', 'b_up', 'w_down', 'b_down'}
    shared = {k + '_all': stacked[k].reshape((-1,) + stacked[k].shape[2:]) for k in expert_stack}
    shared['w_in_all'] = w_in
    in_place = expert_stack | {'w_in'}
    for l in range(w_in.shape[0]):
        p = {k: v[l] for k, v in stacked.items() if k not in in_place}
        xt, x_bf = _layer(xt, x_bf, tabs,
                          dict(p, layer=l, expert_base=l * N_EXPERTS, **shared))
    return xt.reshape(batch, seq, D_MODEL)
```

```python
import functools

import jax
import jax.numpy as jnp
import numpy as np
from jax import lax
from jax.experimental import pallas as pl
from jax.experimental.pallas import tpu as pltpu

F32 = jnp.float32
BF16 = jnp.bfloat16
I32 = jnp.int32

D_MODEL = 2048
N_BRANCHES = 3
DEEPNORM_ALPHA = (2 * 2) ** 0.25
LN_EPS = 1e-5
RMS_EPS = 1e-6
RNN_WIDTH = 1024
RNN_BLOCKS = 8
RNN_BLOCK_W = RNN_WIDTH // RNN_BLOCKS
CONV_WIDTH = 4
RG_LRU_C = 8.0
HEADS = 8
HEAD_DIM = 128
FOX_WIDTH = HEADS * HEAD_DIM
MLA_RANK = 512
MLA_ROPE_DIM = 64
MLA_QK_DIM = HEAD_DIM + MLA_ROPE_DIM
ATTN_QK_DIM = 256
MLA_PAD_DIM = ATTN_QK_DIM
LOG2E = float(np.log2(np.e))
ROPE_THETA = 10000.0
N_EXPERTS = 32
TOP_K = 4
EXPERT_FF = D_MODEL
SWIGLU_ALPHA = 1.702
SWIGLU_LIMIT = 7.0

LANES = 128
SUBLANES = 8
V7X_VMEM_BYTES = 64 * 1024 * 1024

ROW_BLOCK = 256
ATTN_BLOCK = 512
GATHER_UNROLL = 8
ATTN_GROUP = 8
MASK_VALUE = -1e30


def _params(semantics, vmem_mib):
    assert vmem_mib * 1024 * 1024 < V7X_VMEM_BYTES
    return pltpu.CompilerParams(dimension_semantics=semantics,
                                vmem_limit_bytes=vmem_mib * 1024 * 1024)


def _split3(x):
    hi = x.astype(BF16)
    r = x - hi.astype(F32)
    mid = r.astype(BF16)
    lo = (r - mid.astype(F32)).astype(BF16)
    return hi, mid, lo


def _sigmoid(x):
    return 1.0 / (1.0 + jnp.exp(-x))


def _softplus(z):
    return jnp.maximum(z, 0.0) + jnp.log1p(jnp.exp(-jnp.abs(z)))


def _layer_norm(v, g, b):
    mu = jnp.mean(v, axis=-1, keepdims=True)
    c = v - mu
    var = jnp.mean(c * c, axis=-1, keepdims=True)
    return c * lax.rsqrt(var + LN_EPS) * g + b


def _mm_kernel(x_ref, w_ref, o_ref):
    o_ref[...] = jnp.dot(x_ref[...], w_ref[...],
                         preferred_element_type=F32).astype(o_ref.dtype)


def _matmul(x, w, out_dtype, tm, tn, name):
    m, k = x.shape
    n = w.shape[1]
    tm, tn = min(tm, m), min(tn, n)
    return pl.pallas_call(
        _mm_kernel,
        out_shape=jax.ShapeDtypeStruct((m, n), out_dtype),
        grid=(m // tm, n // tn),
        in_specs=[pl.BlockSpec((tm, k), lambda i, j: (i, 0)),
                  pl.BlockSpec((k, tn), lambda i, j: (0, j))],
        out_specs=pl.BlockSpec((tm, tn), lambda i, j: (i, j)),
        compiler_params=_params(("parallel", "arbitrary"), 48),
        name=name,
    )(x, w)


def _in_proj_kernel(x_ref, w_ref, s_ref, o_ref, w_sc):
    @pl.when(pl.program_id(1) == 0)
    def _():
        w_sc[...] = (w_ref[0] * s_ref[...]).T.astype(BF16)

    o_ref[...] = jnp.dot(x_ref[...], w_sc[...],
                         preferred_element_type=F32).astype(o_ref.dtype)


def _in_proj(x, w_stack_t, layer, col0, ncols, col_scale, out_dtype, name):
    m, k = x.shape
    tm, tn = min(1024, m), 1024
    assert col0 % tn == 0 and ncols % tn == 0
    blk0 = col0 // tn
    return pl.pallas_call(
        _in_proj_kernel,
        out_shape=jax.ShapeDtypeStruct((m, ncols), out_dtype),
        grid=(ncols // tn, m // tm),
        in_specs=[pl.BlockSpec((tm, k), lambda j, i: (i, 0)),
                  pl.BlockSpec((1, tn, k), lambda j, i: (layer, blk0 + j, 0)),
                  pl.BlockSpec((tn, 1), lambda j, i: (j, 0))],
        out_specs=pl.BlockSpec((tm, tn), lambda j, i: (i, j)),
        scratch_shapes=[pltpu.VMEM((k, tn), BF16)],
        compiler_params=_params(("arbitrary", "arbitrary"), 48),
        name=name,
    )(x, w_stack_t, col_scale.reshape(ncols, 1))


def _fox_cum_kernel(fl_ref, b_ref, o_ref, *, chunks):
    z = fl_ref[...] + b_ref[...]
    lf = jnp.minimum(z, 0.0) - jnp.log1p(jnp.exp(-jnp.abs(z)))
    rows = lf.shape[0]
    s = lax.broadcasted_iota(I32, (LANES, LANES), 0)
    t = lax.broadcasted_iota(I32, (LANES, LANES), 1)
    tri = (s <= t).astype(BF16)
    incl = None
    for part in _split3(lf):
        d = jnp.dot(part, tri, preferred_element_type=F32)
        incl = d if incl is None else incl + d
    i = lax.broadcasted_iota(I32, (rows, rows), 0)
    j = lax.broadcasted_iota(I32, (rows, rows), 1)
    lower = ((i // chunks == j // chunks) & (j < i)).astype(BF16)
    offs = None
    for part in _split3(incl):
        d = jnp.dot(lower, part, preferred_element_type=F32)
        offs = d if offs is None else offs + d
    o_ref[...] = incl + offs[:, LANES - 1:LANES]


def _fox_cum(f_logit, b_f):
    t = f_logit.shape[0]
    chunks = t // LANES
    fl = f_logit.T.reshape(HEADS * chunks, LANES)
    b = jnp.repeat(b_f.astype(F32), chunks).reshape(HEADS * chunks, 1)
    out = pl.pallas_call(
        functools.partial(_fox_cum_kernel, chunks=chunks),
        out_shape=jax.ShapeDtypeStruct((HEADS * chunks, LANES), F32),
        name="fox_cum",
    )(fl, b)
    return out.reshape(HEADS, t)


def _attn_kernel(qi_ref, ki_ref, q_ref, k_ref, v_ref, o_ref, m_sc, l_sc, acc_sc, *, group):
    p = pl.program_id(1)
    qi = qi_ref[p]
    ki = ki_ref[p]
    blk = q_ref.shape[0]

    @pl.when(ki == 0)
    def _():
        m_sc[...] = jnp.full(m_sc.shape, MASK_VALUE, F32)
        l_sc[...] = jnp.zeros(l_sc.shape, F32)
        acc_sc[...] = jnp.zeros(acc_sc.shape, F32)

    def step(masked):
        for g in range(group):
            qk = slice(g * ATTN_QK_DIM, (g + 1) * ATTN_QK_DIM)
            vo = slice(g * HEAD_DIM, (g + 1) * HEAD_DIM)
            s = lax.dot_general(q_ref[:, qk], k_ref[:, qk], (((1,), (1,)), ((), ())),
                                preferred_element_type=F32)
            if masked:
                row = lax.broadcasted_iota(I32, s.shape, 0)
                col = lax.broadcasted_iota(I32, s.shape, 1)
                s = jnp.where(col <= row, s, MASK_VALUE)
            chunks = [s[:, c * LANES:(c + 1) * LANES] for c in range(blk // LANES)]
            cmax = functools.reduce(jnp.maximum, chunks)
            m_prev = m_sc[g]
            m_new = jnp.maximum(m_prev, jnp.max(cmax, axis=1, keepdims=True))
            alpha = jnp.exp2(m_prev - m_new)
            probs = [jnp.exp2(c - m_new) for c in chunks]
            l_sc[g] = alpha * l_sc[g] + functools.reduce(jnp.add, probs)
            pb = jnp.concatenate([pr.astype(BF16) for pr in probs], axis=1)
            acc_sc[g] = alpha * acc_sc[g] + jnp.dot(pb, v_ref[:, vo],
                                                    preferred_element_type=F32)
            m_sc[g] = m_new

    @pl.when(ki < qi)
    def _():
        step(False)

    @pl.when(ki == qi)
    def _():
        step(True)
        for g in range(group):
            denom = jnp.sum(l_sc[g], axis=1, keepdims=True)
            o_ref[:, g * HEAD_DIM:(g + 1) * HEAD_DIM] = (acc_sc[g] / denom).astype(o_ref.dtype)


def _causal_attention(q, k, v, v_col, name):
    t = q.shape[0]
    blk = min(ATTN_BLOCK, t)
    nq = t // blk
    group = ATTN_GROUP
    pairs = [(a, b) for a in range(nq) for b in range(a + 1)]
    qi_tab = jnp.asarray(np.array([a for a, _ in pairs], np.int32))
    ki_tab = jnp.asarray(np.array([b for _, b in pairs], np.int32))
    v_blk = v_col // (group * HEAD_DIM)
    return pl.pallas_call(
        functools.partial(_attn_kernel, group=group),
        out_shape=jax.ShapeDtypeStruct((t, HEADS * HEAD_DIM), BF16),
        grid_spec=pltpu.PrefetchScalarGridSpec(
            num_scalar_prefetch=2,
            grid=(HEADS // group, len(pairs)),
            in_specs=[
                pl.BlockSpec((blk, group * ATTN_QK_DIM), lambda h, p, qi, ki: (qi[p], h)),
                pl.BlockSpec((blk, group * ATTN_QK_DIM), lambda h, p, qi, ki: (ki[p], h)),
                pl.BlockSpec((blk, group * HEAD_DIM), lambda h, p, qi, ki: (ki[p], v_blk + h)),
            ],
            out_specs=pl.BlockSpec((blk, group * HEAD_DIM), lambda h, p, qi, ki: (qi[p], h)),
            scratch_shapes=[pltpu.VMEM((group, blk, LANES), F32),
                            pltpu.VMEM((group, blk, LANES), F32),
                            pltpu.VMEM((group, blk, HEAD_DIM), F32)],
        ),
        compiler_params=_params(("parallel", "arbitrary"), 32),
        name=name,
    )(qi_tab, ki_tab, q, k, v)


def _fox_prep_kernel(q_ref, k_ref, c_ref, sel_ref, qo_ref, ko_ref):
    bias = None
    for j, part in enumerate(_split3(c_ref[...])):
        d = jnp.dot(part, sel_ref[j], preferred_element_type=F32)
        bias = d if bias is None else bias + d
    lane = lax.broadcasted_iota(I32, (q_ref.shape[0], LANES), 1)
    for h in range(HEADS):
        src = slice(h * HEAD_DIM, (h + 1) * HEAD_DIM)
        lo = h * ATTN_QK_DIM
        g = bias[:, src]
        qo_ref[:, lo:lo + HEAD_DIM] = q_ref[:, src]
        ko_ref[:, lo:lo + HEAD_DIM] = k_ref[:, src]
        qo_ref[:, lo + HEAD_DIM:lo + ATTN_QK_DIM] = jnp.where(
            (lane >= 3) & (lane < 6), 1.0, g).astype(BF16)
        ko_ref[:, lo + HEAD_DIM:lo + ATTN_QK_DIM] = jnp.where(
            lane < 3, 1.0, -pltpu.roll(g, 3, axis=1)).astype(BF16)


def _fox_prep(u_b, cum):
    t = u_b.shape[0]
    tm = min(512, t)
    c = jnp.pad(cum.T * LOG2E, ((0, 0), (0, LANES - HEADS)))
    sel = np.zeros((3, LANES, FOX_WIDTH), np.float32)
    for j in range(3):
        for h in range(HEADS):
            sel[j, h, h * HEAD_DIM + j] = 1.0
    wide = HEADS * ATTN_QK_DIM
    return pl.pallas_call(
        _fox_prep_kernel,
        out_shape=[jax.ShapeDtypeStruct((t, wide), BF16)] * 2,
        grid=(t // tm,),
        in_specs=[pl.BlockSpec((tm, FOX_WIDTH), lambda i: (i, 0)),
                  pl.BlockSpec((tm, FOX_WIDTH), lambda i: (i, 1)),
                  pl.BlockSpec((tm, LANES), lambda i: (i, 0)),
                  pl.BlockSpec((3, LANES, FOX_WIDTH), lambda i: (0, 0, 0))],
        out_specs=[pl.BlockSpec((tm, wide), lambda i: (i, 0))] * 2,
        compiler_params=_params(("arbitrary",), 32),
        name="fox_prep",
    )(u_b, u_b, c, jnp.asarray(sel, BF16))


def _rope_table_kernel(pos_ref, inv_ref, c_ref, s1_ref, s2_ref):
    half = MLA_ROPE_DIM // 2
    ang = pos_ref[...].astype(F32) * inv_ref[...]
    c = jnp.cos(ang)
    s = jnp.sin(ang)
    lane = lax.broadcasted_iota(I32, ang.shape, 1)
    c_ref[...] = jnp.where(lane < MLA_ROPE_DIM, c, 0.0)
    s1_ref[...] = jnp.where(lane < half, -s, 0.0)
    s2_ref[...] = jnp.where((lane >= half) & (lane < MLA_ROPE_DIM), s, 0.0)


def _rope_tables(positions):
    t = positions.shape[0]
    half = MLA_ROPE_DIM // 2
    inv_freq = ROPE_THETA ** (-jnp.arange(half, dtype=F32) / half)
    inv = jnp.concatenate([inv_freq, inv_freq, jnp.zeros((LANES - 2 * half,), F32)])
    tm = min(1024, t)
    spec = pl.BlockSpec((tm, LANES), lambda i: (i, 0))
    return pl.pallas_call(
        _rope_table_kernel,
        out_shape=[jax.ShapeDtypeStruct((t, LANES), F32)] * 3,
        grid=(t // tm,),
        in_specs=[pl.BlockSpec((tm, 1), lambda i: (i, 0)),
                  pl.BlockSpec((1, LANES), lambda i: (0, 0))],
        out_specs=[spec, spec, spec],
        name="rope_tables",
    )(positions.reshape(t, 1), inv.reshape(1, LANES))


def _rope_group(g, c, s1, s2):
    half = MLA_ROPE_DIM // 2
    return (g * c + pltpu.roll(g, LANES - half, axis=1) * s1
            + pltpu.roll(g, half, axis=1) * s2)


def _mla_prep_kernel(u_ref, gq_ref, gkv_ref, wq_ref, wkv_ref, c_ref, s1_ref, s2_ref,
                     q_ref, k_ref, v_ref):
    def rms(v, g):
        ms = jnp.mean(v * v, axis=-1, keepdims=True)
        return (v * lax.rsqrt(ms + RMS_EPS) * g).astype(BF16)

    c, s1, s2 = c_ref[...], s1_ref[...], s2_ref[...]
    cq = rms(u_ref[:, 0:MLA_RANK], gq_ref[...])
    ckv = rms(u_ref[:, MLA_RANK:2 * MLA_RANK], gkv_ref[...])
    q_pre = jnp.dot(cq, wq_ref[...], preferred_element_type=F32)
    kv = jnp.dot(ckv, wkv_ref[...], preferred_element_type=F32)
    k_rot = _rope_group(u_ref[:, 2 * MLA_RANK:2 * MLA_RANK + LANES], c, s1, s2).astype(BF16)
    for h in range(HEADS):
        lo = h * MLA_PAD_DIM
        mid = lo + HEAD_DIM
        hi = lo + MLA_PAD_DIM
        q_ref[:, lo:mid] = q_pre[:, lo:mid].astype(BF16)
        q_ref[:, mid:hi] = _rope_group(q_pre[:, mid:hi], c, s1, s2).astype(BF16)
        k_ref[:, lo:mid] = kv[:, lo:mid].astype(BF16)
        k_ref[:, mid:hi] = k_rot
        v_ref[:, h * HEAD_DIM:(h + 1) * HEAD_DIM] = kv[:, mid:hi].astype(BF16)


def _mla_prep(u_small, g_cq, g_ckv, w_uq_r, w_ukv, tabs):
    t = u_small.shape[0]
    tm = min(512, t)
    wide = HEADS * MLA_PAD_DIM
    row = lambda w: pl.BlockSpec((tm, w), lambda i: (i, 0))
    const = lambda a: pl.BlockSpec(a.shape, lambda i: (0, 0))
    return pl.pallas_call(
        _mla_prep_kernel,
        out_shape=[jax.ShapeDtypeStruct((t, wide), BF16),
                   jax.ShapeDtypeStruct((t, wide), BF16),
                   jax.ShapeDtypeStruct((t, HEADS * HEAD_DIM), BF16)],
        grid=(t // tm,),
        in_specs=[row(u_small.shape[1]), const(g_cq), const(g_ckv), const(w_uq_r),
                  const(w_ukv), row(LANES), row(LANES), row(LANES)],
        out_specs=[row(wide), row(wide), row(HEADS * HEAD_DIM)],
        compiler_params=_params(("arbitrary",), 48),
        name="mla_prep",
    )(u_small, g_cq, g_ckv, w_uq_r, w_ukv, *tabs)


def _rglru_kernel(y_ref, x_ref, cw_ref, cb_ref, wr_ref, br_ref, wi_ref, bi_ref, lam_ref,
                  o_ref, xprev_sc, h_sc, a_sc, g_sc, hs_sc):
    tt = x_ref.shape[0]

    @pl.when(pl.program_id(0) == 0)
    def _():
        xprev_sc[...] = jnp.zeros(xprev_sc.shape, F32)
        h_sc[...] = jnp.zeros(h_sc.shape, F32)

    x = x_ref[...]
    xext = jnp.concatenate([xprev_sc[...], x], axis=0)
    xprev_sc[...] = x[tt - SUBLANES:, :]
    xc = cb_ref[...]
    for j in range(CONV_WIDTH):
        off = SUBLANES - (CONV_WIDTH - 1) + j
        xc = xc + cw_ref[j:j + 1, :] * xext[off:off + tt, :]

    r_parts, i_parts = [], []
    for n in range(RNN_BLOCKS):
        sl = slice(n * RNN_BLOCK_W, (n + 1) * RNN_BLOCK_W)
        xb = xc[:, sl].astype(BF16)
        r_parts.append(_sigmoid(jnp.dot(xb, wr_ref[n], preferred_element_type=F32)
                                + br_ref[:, sl]))
        i_parts.append(_sigmoid(jnp.dot(xb, wi_ref[n], preferred_element_type=F32)
                                + bi_ref[:, sl]))
    r = jnp.concatenate(r_parts, axis=1)
    gate_i = jnp.concatenate(i_parts, axis=1)
    log_a = (-RG_LRU_C * _softplus(-lam_ref[...])) * r
    a = jnp.exp(log_a)
    a_sc[...] = a
    g_sc[...] = jnp.sqrt(-jnp.tanh(log_a) * (a * a + 1.0)) * (gate_i * xc)

    sub = lax.broadcasted_iota(I32, (SUBLANES, RNN_WIDTH), 0)

    def tile_scan(k, h_in):
        rows = pl.ds(pl.multiple_of(k * SUBLANES, SUBLANES), SUBLANES)
        a = a_sc[rows, :]
        g = g_sc[rows, :]
        for d in (1, 2, 4):
            keep = sub >= d
            g = jnp.where(keep, a * pltpu.roll(g, d, axis=0) + g, g)
            a = jnp.where(keep, a * pltpu.roll(a, d, axis=0), a)
        hs = a * h_in + g
        hs_sc[rows, :] = hs
        return jnp.broadcast_to(hs[SUBLANES - 1:SUBLANES, :], (SUBLANES, RNN_WIDTH))

    h_sc[...] = lax.fori_loop(0, tt // SUBLANES, tile_scan, h_sc[...])

    y = y_ref[...]
    gelu = 0.5 * y * (1.0 + jnp.tanh(np.sqrt(2.0 / np.pi) * (y + 0.044715 * (y * y * y))))
    o_ref[...] = (hs_sc[...] * gelu).astype(o_ref.dtype)


def _rglru(u_a, y_blk, x_blk, conv_w, conv_b, w_rg, b_rg, w_ig, b_ig, lam):
    t = u_a.shape[0]
    tt = min(256, t)
    const2 = lambda a: pl.BlockSpec(a.shape, lambda i: (0, 0))
    const3 = lambda a: pl.BlockSpec(a.shape, lambda i: (0, 0, 0))
    return pl.pallas_call(
        _rglru_kernel,
        out_shape=jax.ShapeDtypeStruct((t, RNN_WIDTH), BF16),
        grid=(t // tt,),
        in_specs=[pl.BlockSpec((tt, RNN_WIDTH), lambda i: (i, y_blk)),
                  pl.BlockSpec((tt, RNN_WIDTH), lambda i: (i, x_blk)),
                  const2(conv_w), const2(conv_b), const3(w_rg), const2(b_rg),
                  const3(w_ig), const2(b_ig), const2(lam)],
        out_specs=pl.BlockSpec((tt, RNN_WIDTH), lambda i: (i, 0)),
        scratch_shapes=[pltpu.VMEM((SUBLANES, RNN_WIDTH), F32),
                        pltpu.VMEM((SUBLANES, RNN_WIDTH), F32),
                        pltpu.VMEM((tt, RNN_WIDTH), F32),
                        pltpu.VMEM((tt, RNN_WIDTH), F32),
                        pltpu.VMEM((tt, RNN_WIDTH), F32)],
        compiler_params=_params(("arbitrary",), 32),
        name="rglru",
    )(u_a, u_a, conv_w, conv_b, w_rg, b_rg, w_ig, b_ig, lam)


def _merge_kernel(a_ref, b_ref, c_ref, wa_ref, wb_ref, wc_ref, g0_ref, g1_ref, g2_ref,
                  bm_ref, o_ref):
    def branch(x_ref, w_ref, g_ref, n):
        y = jnp.dot(x_ref[...], w_ref[...], preferred_element_type=F32)
        return _sigmoid(g_ref[...] + bm_ref[n:n + 1, :]) * y

    mixed = (branch(a_ref, wa_ref, g0_ref, 0) + branch(b_ref, wb_ref, g1_ref, 1)
             + branch(c_ref, wc_ref, g2_ref, 2))
    o_ref[...] = mixed.astype(o_ref.dtype)


def _merge(ya, yb, yc, wa, wb, wc, u_a, b_merge):
    t = ya.shape[0]
    tm, tn = min(512, t), 1024
    nn = D_MODEL // tn
    xin = pl.BlockSpec((tm, ya.shape[1]), lambda j, i: (i, 0))
    win = pl.BlockSpec((ya.shape[1], tn), lambda j, i: (0, j))
    gate = lambda n: pl.BlockSpec((tm, tn), lambda j, i: (i, n * nn + j))
    return pl.pallas_call(
        _merge_kernel,
        out_shape=jax.ShapeDtypeStruct((t, D_MODEL), BF16),
        grid=(nn, t // tm),
        in_specs=[xin, xin, xin, win, win, win, gate(0), gate(1), gate(2),
                  pl.BlockSpec((N_BRANCHES, tn), lambda j, i: (0, j))],
        out_specs=pl.BlockSpec((tm, tn), lambda j, i: (i, j)),
        compiler_params=_params(("arbitrary", "arbitrary"), 48),
        name="merge",
    )(ya, yb, yc, wa, wb, wc, u_a, u_a, u_a, b_merge)


def _outproj_ln_kernel(m_ref, w_ref, x_ref, g_ref, b_ref, of_ref, ob_ref):
    mix = jnp.dot(m_ref[...], w_ref[...], preferred_element_type=F32)
    y = _layer_norm(DEEPNORM_ALPHA * x_ref[...] + mix, g_ref[...], b_ref[...])
    of_ref[...] = y
    ob_ref[...] = y.astype(BF16)


def _outproj_ln(mixed, w_out, x, g, b):
    t = x.shape[0]
    tm = min(256, t)
    row = pl.BlockSpec((tm, D_MODEL), lambda i: (i, 0))
    vec = pl.BlockSpec((1, D_MODEL), lambda i: (0, 0))
    return pl.pallas_call(
        _outproj_ln_kernel,
        out_shape=[jax.ShapeDtypeStruct((t, D_MODEL), F32),
                   jax.ShapeDtypeStruct((t, D_MODEL), BF16)],
        grid=(t // tm,),
        in_specs=[row, pl.BlockSpec((D_MODEL, D_MODEL), lambda i: (0, 0)), row, vec, vec],
        out_specs=[row, row],
        compiler_params=_params(("arbitrary",), 48),
        name="outproj_ln",
    )(mixed, w_out, x, g, b)


def _router_kernel(x_ref, w_ref, b_ref, idx_ref, gate_ref, rank_ref, count_ref, count_sc):
    nt = (((1,), (1,)), ((), ()))
    x = x_ref[...]
    x_hi = x.astype(BF16)
    x_lo = (x - x_hi.astype(F32)).astype(BF16)
    w = w_ref[...]
    w_hi = w.astype(BF16)
    w_lo = (w - w_hi.astype(F32)).astype(BF16)
    logits = (lax.dot_general(w_hi, x_hi, nt, preferred_element_type=F32)
              + lax.dot_general(w_hi, x_lo, nt, preferred_element_type=F32)
              + lax.dot_general(w_lo, x_hi, nt, preferred_element_type=F32)
              + b_ref[...])
    eidx = lax.broadcasted_iota(I32, logits.shape, 0)
    vals, idxs = [], []
    for _ in range(TOP_K):
        m = jnp.max(logits, axis=0, keepdims=True)
        idx = jnp.min(jnp.where(logits == m, eidx, N_EXPERTS), axis=0, keepdims=True)
        vals.append(m)
        idxs.append(idx)
        logits = jnp.where(eidx == idx, -jnp.inf, logits)
    exps = [jnp.exp(v - vals[0]) for v in vals]
    denom = exps[0] + exps[1] + exps[2] + exps[3]
    pad = SUBLANES - TOP_K
    tokens = logits.shape[1]
    idx_ref[...] = jnp.concatenate(idxs + [jnp.zeros((pad, tokens), I32)], axis=0)
    gate_ref[...] = jnp.concatenate([e / denom for e in exps]
                                    + [jnp.zeros((pad, tokens), F32)], axis=0)

    @pl.when(pl.program_id(0) == 0)
    def _():
        count_sc[...] = jnp.zeros(count_sc.shape, F32)

    src = lax.broadcasted_iota(I32, (tokens, tokens), 0)
    dst = lax.broadcasted_iota(I32, (tokens, tokens), 1)
    before = (src < dst).astype(BF16)
    seen = count_sc[...]
    ranks = []
    for idx in idxs:
        hit = eidx == idx
        prefix = jnp.dot(hit.astype(BF16), before, preferred_element_type=F32)
        ranks.append(jnp.sum(jnp.where(hit, seen + prefix, 0.0), axis=0, keepdims=True))
        seen = seen + jnp.sum(hit.astype(F32), axis=1, keepdims=True)
    count_sc[...] = seen
    rank_ref[...] = jnp.concatenate(ranks + [jnp.zeros((pad, tokens), F32)],
                                    axis=0).astype(I32)
    count_ref[...] = seen.astype(I32)


def _router(x, w_router_t, b_router):
    t = x.shape[0]
    tm = min(1024, t)
    tok = pl.BlockSpec((SUBLANES, tm), lambda i: (0, i))
    idx, gate, rank, count = pl.pallas_call(
        _router_kernel,
        out_shape=[jax.ShapeDtypeStruct((SUBLANES, t), I32),
                   jax.ShapeDtypeStruct((SUBLANES, t), F32),
                   jax.ShapeDtypeStruct((SUBLANES, t), I32),
                   jax.ShapeDtypeStruct((N_EXPERTS, 1), I32)],
        grid=(t // tm,),
        in_specs=[pl.BlockSpec((tm, D_MODEL), lambda i: (i, 0)),
                  pl.BlockSpec((N_EXPERTS, D_MODEL), lambda i: (0, 0)),
                  pl.BlockSpec((N_EXPERTS, 1), lambda i: (0, 0))],
        out_specs=[tok, tok, tok, pl.BlockSpec((N_EXPERTS, 1), lambda i: (0, 0))],
        scratch_shapes=[pltpu.VMEM((N_EXPERTS, 1), F32)],
        compiler_params=_params(("arbitrary",), 48),
        name="router",
    )(x, w_router_t, b_router.reshape(N_EXPERTS, 1))
    return idx[:TOP_K].T, gate[:TOP_K].T, rank[:TOP_K].T, count[:, 0]


def _row_copy(src_hbm, row, dst, dst_row, sem):
    return pltpu.make_async_copy(src_hbm.at[pl.ds(row, 1), :],
                                 dst.at[pl.ds(dst_row, 1), :], sem)


def _start_rows(src_hbm, idx_ref, idx0, stride, dst, sem):
    for r in range(dst.shape[0]):
        _row_copy(src_hbm, idx_ref[idx0 + r * stride], dst, r, sem).start()


def _wait_rows(src_hbm, dst, sem):
    pltpu.make_async_copy(src_hbm.at[pl.ds(0, dst.shape[0]), :], dst, sem).wait()


def _expert_changed(be_ref, b):
    return (b == 0) | (be_ref[b] != be_ref[jnp.maximum(b - 1, 0)])


def _expert_up_kernel(be_ref, nused_ref, tok_ref, x_hbm, wg_ref, wl_ref, bg_ref, bl_ref, o_ref,
                      wg_sc, wl_sc, rows_a, rows_b, rows_c, sem):
    ring = (rows_a, rows_b, rows_c)
    depth = len(ring) - 1
    f = pl.program_id(0)
    b = pl.program_id(1)
    n_used = nused_ref[0]
    rb = rows_a.shape[0]
    phase = (f * n_used + b) % len(ring)

    def start(blk, slot):
        _start_rows(x_hbm, tok_ref, blk * rb, 1, ring[slot], sem.at[slot])

    @pl.when((f == 0) & (b == 0))
    def _():
        for s in range(depth):
            start(s % n_used, s)

    @pl.when(_expert_changed(be_ref, b))
    def _():
        wg_sc[...] = wg_ref[0].astype(BF16)
        wl_sc[...] = wl_ref[0].astype(BF16)

    def block(slot):
        cur = ring[slot]
        _wait_rows(x_hbm, cur, sem.at[slot])
        ahead = b + depth
        start(jnp.where(ahead < n_used, ahead, (ahead - n_used) % n_used),
              (slot + depth) % len(ring))
        x = cur[...].astype(BF16)
        h_glu = jnp.dot(x, wg_sc[...], preferred_element_type=F32) + bg_ref[0]
        h_lin = jnp.dot(x, wl_sc[...], preferred_element_type=F32) + bl_ref[0]
        h_glu = jnp.minimum(h_glu, SWIGLU_LIMIT)
        h_lin = jnp.clip(h_lin, -SWIGLU_LIMIT, SWIGLU_LIMIT)
        act = h_glu * _sigmoid(SWIGLU_ALPHA * h_glu) * (h_lin + 1.0)
        o_ref[...] = act.astype(o_ref.dtype)

    for slot in range(len(ring)):
        pl.when((b < n_used) & (phase == slot))(functools.partial(block, slot))

    @pl.when(b >= n_used)
    def _():
        o_ref[...] = jnp.zeros(o_ref.shape, o_ref.dtype)

    @pl.when((f == pl.num_programs(0) - 1) & (b == pl.num_programs(1) - 1))
    def _():
        total = pl.num_programs(0) * n_used
        for s in range(depth):
            spare = (total + s) % len(ring)
            for slot in range(len(ring)):
                pl.when(spare == slot)(
                    functools.partial(_wait_rows, x_hbm, ring[slot], sem.at[slot]))


def _expert_up(x, row_tok, block_expert, n_used, w_up, b_up):
    n_rows = row_tok.shape[0]
    tn = 1024
    nf = EXPERT_FF // tn
    b_up3 = b_up.reshape(b_up.shape[0], 1, 2 * EXPERT_FF)
    return pl.pallas_call(
        _expert_up_kernel,
        out_shape=jax.ShapeDtypeStruct((n_rows, EXPERT_FF), BF16),
        grid_spec=pltpu.PrefetchScalarGridSpec(
            num_scalar_prefetch=3,
            grid=(nf, n_rows // ROW_BLOCK),
            in_specs=[
                pl.BlockSpec(memory_space=pl.ANY),
                pl.BlockSpec((1, D_MODEL, tn), lambda f, b, be, nu, tok: (be[b], 0, f)),
                pl.BlockSpec((1, D_MODEL, tn), lambda f, b, be, nu, tok: (be[b], 0, nf + f)),
                pl.BlockSpec((1, 1, tn), lambda f, b, be, nu, tok: (be[b], 0, f)),
                pl.BlockSpec((1, 1, tn), lambda f, b, be, nu, tok: (be[b], 0, nf + f)),
            ],
            out_specs=pl.BlockSpec((ROW_BLOCK, tn), lambda f, b, be, nu, tok: (b, f)),
            scratch_shapes=[pltpu.VMEM((D_MODEL, tn), BF16),
                            pltpu.VMEM((D_MODEL, tn), BF16),
                            pltpu.VMEM((ROW_BLOCK, D_MODEL), F32),
                            pltpu.VMEM((ROW_BLOCK, D_MODEL), F32),
                            pltpu.VMEM((ROW_BLOCK, D_MODEL), F32),
                            pltpu.SemaphoreType.DMA((3,))],
        ),
        compiler_params=_params(("arbitrary", "arbitrary"), 58),
        name="expert_up",
    )(block_expert, n_used, row_tok, x, w_up, w_up, b_up3, b_up3)


def _expert_down_kernel(be_ref, nused_ref, act_ref, w_ref, bias_ref, o_ref, w_sc):
    b = pl.program_id(1)

    @pl.when(_expert_changed(be_ref, b))
    def _():
        w_sc[...] = w_ref[0].astype(BF16)

    @pl.when(b < nused_ref[0])
    def _():
        o_ref[...] = (jnp.dot(act_ref[...], w_sc[...], preferred_element_type=F32)
                      + bias_ref[0])

    @pl.when(b >= nused_ref[0])
    def _():
        o_ref[...] = jnp.zeros(o_ref.shape, o_ref.dtype)


def _expert_down(act, block_expert, n_used, w_down, b_down):
    n_rows = act.shape[0]
    tn = D_MODEL
    b_down3 = b_down.reshape(b_down.shape[0], 1, D_MODEL)
    return pl.pallas_call(
        _expert_down_kernel,
        out_shape=jax.ShapeDtypeStruct((n_rows, D_MODEL), F32),
        grid_spec=pltpu.PrefetchScalarGridSpec(
            num_scalar_prefetch=2,
            grid=(D_MODEL // tn, n_rows // ROW_BLOCK),
            in_specs=[
                pl.BlockSpec((ROW_BLOCK, EXPERT_FF), lambda n, b, be, nu: (b, 0)),
                pl.BlockSpec((1, EXPERT_FF, tn), lambda n, b, be, nu: (be[b], 0, n)),
                pl.BlockSpec((1, 1, tn), lambda n, b, be, nu: (be[b], 0, n)),
            ],
            out_specs=pl.BlockSpec((ROW_BLOCK, tn), lambda n, b, be, nu: (b, n)),
            scratch_shapes=[pltpu.VMEM((EXPERT_FF, tn), BF16)],
        ),
        compiler_params=_params(("arbitrary", "arbitrary"), 56),
        name="expert_down",
    )(block_expert, n_used, act, w_down, b_down3)


def _combine_ln_kernel(pos_ref, y_hbm, gate_ref, x_ref, g_ref, b_ref, of_ref, ob_ref,
                       buf, sem):
    tc = x_ref.shape[0]
    i = pl.program_id(0)
    last = pl.num_programs(0) - 1
    cur = i % 2

    def start(tile, slot):
        for k in range(TOP_K):
            _start_rows(y_hbm, pos_ref, tile * (tc * TOP_K) + k, TOP_K, buf.at[slot, k],
                        sem.at[slot])

    def wait(slot):
        for k in range(TOP_K):
            _wait_rows(y_hbm, buf.at[slot, k], sem.at[slot])

    @pl.when(i == 0)
    def _():
        start(0, 0)

    wait(cur)
    start(jnp.where(i < last, i + 1, 0), 1 - cur)
    gate = gate_ref[...]
    ffn = gate[:, 0:1] * buf[cur, 0]
    for k in range(1, TOP_K):
        ffn = ffn + gate[:, k:k + 1] * buf[cur, k]
    y = _layer_norm(DEEPNORM_ALPHA * x_ref[...] + ffn, g_ref[...], b_ref[...])
    of_ref[...] = y
    ob_ref[...] = y.astype(BF16)

    @pl.when(i == last)
    def _():
        wait(1 - cur)


def _combine_ln(y_rows, pos, gate, x, g, b):
    t = x.shape[0]
    tc = min(128, t)
    row = lambda b_, p: (b_, 0)
    return pl.pallas_call(
        _combine_ln_kernel,
        out_shape=[jax.ShapeDtypeStruct((t, D_MODEL), F32),
                   jax.ShapeDtypeStruct((t, D_MODEL), BF16)],
        grid_spec=pltpu.PrefetchScalarGridSpec(
            num_scalar_prefetch=1,
            grid=(t // tc,),
            in_specs=[pl.BlockSpec(memory_space=pl.ANY),
                      pl.BlockSpec((tc, TOP_K), row),
                      pl.BlockSpec((tc, D_MODEL), row),
                      pl.BlockSpec((1, D_MODEL), lambda b_, p: (0, 0)),
                      pl.BlockSpec((1, D_MODEL), lambda b_, p: (0, 0))],
            out_specs=[pl.BlockSpec((tc, D_MODEL), row), pl.BlockSpec((tc, D_MODEL), row)],
            scratch_shapes=[pltpu.VMEM((2, TOP_K, tc, D_MODEL), F32),
                            pltpu.SemaphoreType.DMA((2,))],
        ),
        compiler_params=_params(("arbitrary",), 32),
        name="combine_ln",
    )(pos, y_rows, gate, x, g, b)


def _dispatch_plan(top_idx, rank, counts):
    t = top_idx.shape[0]
    n_assign = t * TOP_K
    n_blocks = -(-n_assign // ROW_BLOCK) + N_EXPERTS
    e_flat = top_idx.reshape(-1)
    padded = (counts + ROW_BLOCK - 1) // ROW_BLOCK * ROW_BLOCK
    pend = jnp.cumsum(padded)
    pstart = pend - padded
    dest = (pstart[e_flat] + rank.reshape(-1)).astype(I32)
    row_tok = jnp.zeros((n_blocks * ROW_BLOCK,), I32).at[dest].set(
        jnp.arange(n_assign, dtype=I32) // TOP_K)
    block_row0 = jnp.arange(n_blocks, dtype=I32) * ROW_BLOCK
    block_expert = jnp.minimum(jnp.sum((pend[None, :] <= block_row0[:, None]).astype(I32), axis=1),
                               N_EXPERTS - 1)
    n_used = (pend[-1:] // ROW_BLOCK).astype(I32)
    return dest, row_tok, block_expert, n_used


IN_FOX = N_BRANCHES * D_MODEL + 2 * RNN_WIDTH
IN_TAIL = IN_FOX + 3 * FOX_WIDTH


def _prep_in_tail(w_in, layer):
    tail = w_in[layer, :, IN_TAIL:]
    pad = LANES - MLA_ROPE_DIM - HEADS
    return jnp.concatenate([tail[:, HEADS:], tail[:, :HEADS], jnp.zeros((D_MODEL, pad), F32)],
                           axis=1).astype(BF16)


def _prep_uq(w_uq_l):
    w = w_uq_l.reshape(MLA_RANK, HEADS, MLA_QK_DIM) * (MLA_QK_DIM ** -0.5 * LOG2E)
    pad = jnp.zeros((MLA_RANK, HEADS, MLA_PAD_DIM - MLA_QK_DIM), F32)
    return jnp.concatenate([w, pad], axis=2).reshape(MLA_RANK, HEADS * MLA_PAD_DIM).astype(BF16)


def _layer(x, x_bf, tabs, p):
    t = x.shape[0]
    w_in, layer = p['w_in_all'], p['layer']
    w_in_t = jnp.swapaxes(w_in, 1, 2)
    u_a = _in_proj(x_bf, w_in_t, layer, 0, IN_FOX, jnp.ones((IN_FOX,), F32), F32,
                   "in_proj_gates_lru")
    q_scale = jnp.concatenate([jnp.full((FOX_WIDTH,), HEAD_DIM ** -0.5 * LOG2E, F32),
                               jnp.ones((2 * FOX_WIDTH,), F32)])
    u_b = _in_proj(x_bf, w_in_t, layer, IN_FOX, 3 * FOX_WIDTH, q_scale, BF16, "in_proj_fox")
    w_c = _prep_in_tail(w_in, layer)
    u_c = _matmul(x_bf, w_c, F32, 1024, w_c.shape[1], "in_proj_small")

    gate_blocks = N_BRANCHES * D_MODEL // RNN_WIDTH
    y_a = _rglru(u_a, gate_blocks, gate_blocks + 1,
                 p['conv_w'], p['conv_b'].reshape(1, RNN_WIDTH),
                 p['w_rec_gate'].astype(BF16), p['b_rec_gate'].reshape(1, RNN_WIDTH),
                 p['w_inp_gate'].astype(BF16), p['b_inp_gate'].reshape(1, RNN_WIDTH),
                 p['lru_lambda'].reshape(1, RNN_WIDTH))

    fl_col = 2 * MLA_RANK + MLA_ROPE_DIM
    cum = _fox_cum(u_c[:, fl_col:fl_col + HEADS], p['b_forget'])
    q_x, k_x = _fox_prep(u_b, cum)
    y_b = _causal_attention(q_x, k_x, u_b, 2 * FOX_WIDTH, "fox_attention")

    q_f, k_f, v_c = _mla_prep(u_c, p['g_cq'].reshape(1, MLA_RANK),
                              p['g_ckv'].reshape(1, MLA_RANK), _prep_uq(p['w_uq']),
                              p['w_ukv'].astype(BF16), tabs)
    y_c = _causal_attention(q_f, k_f, v_c, 0, "mla_attention")

    mixed = _merge(y_a, y_b, y_c, p['w_proj_lru'].astype(BF16), p['w_proj_fox'].astype(BF16),
                   p['w_proj_mla'].astype(BF16), u_a, p['b_merge'])
    x, x_bf = _outproj_ln(mixed, p['w_out'].astype(BF16), x,
                          p['ln1_g'].reshape(1, D_MODEL), p['ln1_b'].reshape(1, D_MODEL))

    top_idx, gate, rank, counts = _router(x, p['w_router'].T, p['b_router'])
    dest, row_tok, block_expert, n_used = _dispatch_plan(top_idx, rank, counts)
    block_expert = block_expert + p['expert_base']
    act = _expert_up(x, row_tok, block_expert, n_used, p['w_up_all'], p['b_up_all'])
    y_rows = _expert_down(act, block_expert, n_used, p['w_down_all'], p['b_down_all'])
    del t
    return _combine_ln(y_rows, dest, gate, x, p['ln2_g'].reshape(1, D_MODEL),
                       p['ln2_b'].reshape(1, D_MODEL))


_LAYER_PARAMS = ('w_in', 'b_merge', 'b_forget', 'conv_w', 'conv_b', 'w_rec_gate', 'b_rec_gate',
                 'w_inp_gate', 'b_inp_gate', 'lru_lambda', 'g_cq', 'g_ckv', 'w_uq', 'w_ukv',
                 'w_proj_lru', 'w_proj_fox', 'w_proj_mla', 'w_out', 'ln1_g', 'ln1_b',
                 'w_router', 'b_router', 'w_up', 'b_up', 'w_down', 'b_down', 'ln2_g', 'ln2_b')


def kernel(x, positions, w_in, b_merge, b_forget, conv_w, conv_b, w_rec_gate, b_rec_gate,
           w_inp_gate, b_inp_gate, lru_lambda, g_cq, g_ckv, w_uq, w_ukv, w_proj_lru,
           w_proj_fox, w_proj_mla, w_out, ln1_g, ln1_b, w_router, b_router, w_up, b_up,
           w_down, b_down, ln2_g, ln2_b):
    stacked = dict(zip(_LAYER_PARAMS, (
        w_in, b_merge, b_forget, conv_w, conv_b, w_rec_gate, b_rec_gate, w_inp_gate,
        b_inp_gate, lru_lambda, g_cq, g_ckv, w_uq, w_ukv, w_proj_lru, w_proj_fox,
        w_proj_mla, w_out, ln1_g, ln1_b, w_router, b_router, w_up, b_up, w_down, b_down,
        ln2_g, ln2_b)))
    batch, seq, _ = x.shape
    assert batch == 1
    xt = x.reshape(seq, D_MODEL)
    x_bf = xt.astype(BF16)
    tabs = _rope_tables(positions.reshape(seq))
    expert_stack = {'w_up', 'b_up', 'w_down', 'b_down'}
    shared = {k + '_all': stacked[k].reshape((-1,) + stacked[k].shape[2:]) for k in expert_stack}
    shared['w_in_all'] = w_in
    in_place = expert_stack | {'w_in'}
    for l in range(w_in.shape[0]):
        p = {k: v[l] for k, v in stacked.items() if k not in in_place}
        xt, x_bf = _layer(xt, x_bf, tabs,
                          dict(p, layer=l, expert_base=l * N_EXPERTS, **shared))
    return xt.reshape(batch, seq, D_MODEL)
```

```python
import functools

import jax
import jax.numpy as jnp
import numpy as np
from jax import lax
from jax.experimental import pallas as pl
from jax.experimental.pallas import tpu as pltpu

F32 = jnp.float32
BF16 = jnp.bfloat16
I32 = jnp.int32

D_MODEL = 2048
N_BRANCHES = 3
DEEPNORM_ALPHA = (2 * 2) ** 0.25
LN_EPS = 1e-5
RMS_EPS = 1e-6
RNN_WIDTH = 1024
RNN_BLOCKS = 8
RNN_BLOCK_W = RNN_WIDTH // RNN_BLOCKS
CONV_WIDTH = 4
RG_LRU_C = 8.0
HEADS = 8
HEAD_DIM = 128
FOX_WIDTH = HEADS * HEAD_DIM
MLA_RANK = 512
MLA_ROPE_DIM = 64
MLA_QK_DIM = HEAD_DIM + MLA_ROPE_DIM
ATTN_QK_DIM = 256
MLA_PAD_DIM = ATTN_QK_DIM
LOG2E = float(np.log2(np.e))
ROPE_THETA = 10000.0
N_EXPERTS = 32
TOP_K = 4
EXPERT_FF = D_MODEL
SWIGLU_ALPHA = 1.702
SWIGLU_LIMIT = 7.0

LANES = 128
SUBLANES = 8
V7X_VMEM_BYTES = 64 * 1024 * 1024

ROW_BLOCK = 256
ATTN_BLOCK = 512
GATHER_UNROLL = 8
ATTN_GROUP = 8
MASK_VALUE = -1e30


def _params(semantics, vmem_mib):
    assert vmem_mib * 1024 * 1024 < V7X_VMEM_BYTES
    return pltpu.CompilerParams(dimension_semantics=semantics,
                                vmem_limit_bytes=vmem_mib * 1024 * 1024)


def _split3(x):
    hi = x.astype(BF16)
    r = x - hi.astype(F32)
    mid = r.astype(BF16)
    lo = (r - mid.astype(F32)).astype(BF16)
    return hi, mid, lo


def _sigmoid(x):
    return 1.0 / (1.0 + jnp.exp(-x))


def _softplus(z):
    return jnp.maximum(z, 0.0) + jnp.log1p(jnp.exp(-jnp.abs(z)))


def _layer_norm(v, g, b):
    mu = jnp.mean(v, axis=-1, keepdims=True)
    c = v - mu
    var = jnp.mean(c * c, axis=-1, keepdims=True)
    return c * lax.rsqrt(var + LN_EPS) * g + b


def _mm_kernel(x_ref, w_ref, o_ref):
    o_ref[...] = jnp.dot(x_ref[...], w_ref[...],
                         preferred_element_type=F32).astype(o_ref.dtype)


def _matmul(x, w, out_dtype, tm, tn, name):
    m, k = x.shape
    n = w.shape[1]
    tm, tn = min(tm, m), min(tn, n)
    return pl.pallas_call(
        _mm_kernel,
        out_shape=jax.ShapeDtypeStruct((m, n), out_dtype),
        grid=(m // tm, n // tn),
        in_specs=[pl.BlockSpec((tm, k), lambda i, j: (i, 0)),
                  pl.BlockSpec((k, tn), lambda i, j: (0, j))],
        out_specs=pl.BlockSpec((tm, tn), lambda i, j: (i, j)),
        compiler_params=_params(("parallel", "arbitrary"), 48),
        name=name,
    )(x, w)


def _in_proj_kernel(x_ref, w_ref, s_ref, o_ref, w_sc):
    @pl.when(pl.program_id(1) == 0)
    def _():
        w_sc[...] = (w_ref[0] * s_ref[...]).T.astype(BF16)

    o_ref[...] = jnp.dot(x_ref[...], w_sc[...],
                         preferred_element_type=F32).astype(o_ref.dtype)


def _in_proj(x, w_stack_t, layer, col0, ncols, col_scale, out_dtype, name):
    m, k = x.shape
    tm, tn = min(1024, m), 1024
    assert col0 % tn == 0 and ncols % tn == 0
    blk0 = col0 // tn
    return pl.pallas_call(
        _in_proj_kernel,
        out_shape=jax.ShapeDtypeStruct((m, ncols), out_dtype),
        grid=(ncols // tn, m // tm),
        in_specs=[pl.BlockSpec((tm, k), lambda j, i: (i, 0)),
                  pl.BlockSpec((1, tn, k), lambda j, i: (layer, blk0 + j, 0)),
                  pl.BlockSpec((tn, 1), lambda j, i: (j, 0))],
        out_specs=pl.BlockSpec((tm, tn), lambda j, i: (i, j)),
        scratch_shapes=[pltpu.VMEM((k, tn), BF16)],
        compiler_params=_params(("arbitrary", "arbitrary"), 48),
        name=name,
    )(x, w_stack_t, col_scale.reshape(ncols, 1))


def _fox_cum_kernel(fl_ref, b_ref, o_ref, *, chunks):
    z = fl_ref[...] + b_ref[...]
    lf = jnp.minimum(z, 0.0) - jnp.log1p(jnp.exp(-jnp.abs(z)))
    rows = lf.shape[0]
    s = lax.broadcasted_iota(I32, (LANES, LANES), 0)
    t = lax.broadcasted_iota(I32, (LANES, LANES), 1)
    tri = (s <= t).astype(BF16)
    incl = None
    for part in _split3(lf):
        d = jnp.dot(part, tri, preferred_element_type=F32)
        incl = d if incl is None else incl + d
    i = lax.broadcasted_iota(I32, (rows, rows), 0)
    j = lax.broadcasted_iota(I32, (rows, rows), 1)
    lower = ((i // chunks == j // chunks) & (j < i)).astype(BF16)
    offs = None
    for part in _split3(incl):
        d = jnp.dot(lower, part, preferred_element_type=F32)
        offs = d if offs is None else offs + d
    o_ref[...] = incl + offs[:, LANES - 1:LANES]


def _fox_cum(f_logit, b_f):
    t = f_logit.shape[0]
    chunks = t // LANES
    fl = f_logit.T.reshape(HEADS * chunks, LANES)
    b = jnp.repeat(b_f.astype(F32), chunks).reshape(HEADS * chunks, 1)
    out = pl.pallas_call(
        functools.partial(_fox_cum_kernel, chunks=chunks),
        out_shape=jax.ShapeDtypeStruct((HEADS * chunks, LANES), F32),
        name="fox_cum",
    )(fl, b)
    return out.reshape(HEADS, t)


def _attn_kernel(qi_ref, ki_ref, q_ref, k_ref, v_ref, o_ref, m_sc, l_sc, acc_sc, *, group):
    p = pl.program_id(1)
    qi = qi_ref[p]
    ki = ki_ref[p]
    blk = q_ref.shape[0]

    @pl.when(ki == 0)
    def _():
        m_sc[...] = jnp.full(m_sc.shape, MASK_VALUE, F32)
        l_sc[...] = jnp.zeros(l_sc.shape, F32)
        acc_sc[...] = jnp.zeros(acc_sc.shape, F32)

    def step(masked):
        for g in range(group):
            qk = slice(g * ATTN_QK_DIM, (g + 1) * ATTN_QK_DIM)
            vo = slice(g * HEAD_DIM, (g + 1) * HEAD_DIM)
            s = lax.dot_general(q_ref[:, qk], k_ref[:, qk], (((1,), (1,)), ((), ())),
                                preferred_element_type=F32)
            if masked:
                row = lax.broadcasted_iota(I32, s.shape, 0)
                col = lax.broadcasted_iota(I32, s.shape, 1)
                s = jnp.where(col <= row, s, MASK_VALUE)
            chunks = [s[:, c * LANES:(c + 1) * LANES] for c in range(blk // LANES)]
            cmax = functools.reduce(jnp.maximum, chunks)
            m_prev = m_sc[g]
            m_new = jnp.maximum(m_prev, jnp.max(cmax, axis=1, keepdims=True))
            alpha = jnp.exp2(m_prev - m_new)
            probs = [jnp.exp2(c - m_new) for c in chunks]
            l_sc[g] = alpha * l_sc[g] + functools.reduce(jnp.add, probs)
            pb = jnp.concatenate([pr.astype(BF16) for pr in probs], axis=1)
            acc_sc[g] = alpha * acc_sc[g] + jnp.dot(pb, v_ref[:, vo],
                                                    preferred_element_type=F32)
            m_sc[g] = m_new

    @pl.when(ki < qi)
    def _():
        step(False)

    @pl.when(ki == qi)
    def _():
        step(True)
        for g in range(group):
            denom = jnp.sum(l_sc[g], axis=1, keepdims=True)
            o_ref[:, g * HEAD_DIM:(g + 1) * HEAD_DIM] = (acc_sc[g] / denom).astype(o_ref.dtype)


def _causal_attention(q, k, v, v_col, name):
    t = q.shape[0]
    blk = min(ATTN_BLOCK, t)
    nq = t // blk
    group = ATTN_GROUP
    pairs = [(a, b) for a in range(nq) for b in range(a + 1)]
    qi_tab = jnp.asarray(np.array([a for a, _ in pairs], np.int32))
    ki_tab = jnp.asarray(np.array([b for _, b in pairs], np.int32))
    v_blk = v_col // (group * HEAD_DIM)
    return pl.pallas_call(
        functools.partial(_attn_kernel, group=group),
        out_shape=jax.ShapeDtypeStruct((t, HEADS * HEAD_DIM), BF16),
        grid_spec=pltpu.PrefetchScalarGridSpec(
            num_scalar_prefetch=2,
            grid=(HEADS // group, len(pairs)),
            in_specs=[
                pl.BlockSpec((blk, group * ATTN_QK_DIM), lambda h, p, qi, ki: (qi[p], h)),
                pl.BlockSpec((blk, group * ATTN_QK_DIM), lambda h, p, qi, ki: (ki[p], h)),
                pl.BlockSpec((blk, group * HEAD_DIM), lambda h, p, qi, ki: (ki[p], v_blk + h)),
            ],
            out_specs=pl.BlockSpec((blk, group * HEAD_DIM), lambda h, p, qi, ki: (qi[p], h)),
            scratch_shapes=[pltpu.VMEM((group, blk, LANES), F32),
                            pltpu.VMEM((group, blk, LANES), F32),
                            pltpu.VMEM((group, blk, HEAD_DIM), F32)],
        ),
        compiler_params=_params(("parallel", "arbitrary"), 32),
        name=name,
    )(qi_tab, ki_tab, q, k, v)


def _fox_prep_kernel(q_ref, k_ref, c_ref, sel_ref, qo_ref, ko_ref):
    bias = None
    for j, part in enumerate(_split3(c_ref[...])):
        d = jnp.dot(part, sel_ref[j], preferred_element_type=F32)
        bias = d if bias is None else bias + d
    lane = lax.broadcasted_iota(I32, (q_ref.shape[0], LANES), 1)
    for h in range(HEADS):
        src = slice(h * HEAD_DIM, (h + 1) * HEAD_DIM)
        lo = h * ATTN_QK_DIM
        g = bias[:, src]
        qo_ref[:, lo:lo + HEAD_DIM] = q_ref[:, src]
        ko_ref[:, lo:lo + HEAD_DIM] = k_ref[:, src]
        qo_ref[:, lo + HEAD_DIM:lo + ATTN_QK_DIM] = jnp.where(
            (lane >= 3) & (lane < 6), 1.0, g).astype(BF16)
        ko_ref[:, lo + HEAD_DIM:lo + ATTN_QK_DIM] = jnp.where(
            lane < 3, 1.0, -pltpu.roll(g, 3, axis=1)).astype(BF16)


def _fox_prep(u_b, cum):
    t = u_b.shape[0]
    tm = min(512, t)
    c = jnp.pad(cum.T * LOG2E, ((0, 0), (0, LANES - HEADS)))
    sel = np.zeros((3, LANES, FOX_WIDTH), np.float32)
    for j in range(3):
        for h in range(HEADS):
            sel[j, h, h * HEAD_DIM + j] = 1.0
    wide = HEADS * ATTN_QK_DIM
    return pl.pallas_call(
        _fox_prep_kernel,
        out_shape=[jax.ShapeDtypeStruct((t, wide), BF16)] * 2,
        grid=(t // tm,),
        in_specs=[pl.BlockSpec((tm, FOX_WIDTH), lambda i: (i, 0)),
                  pl.BlockSpec((tm, FOX_WIDTH), lambda i: (i, 1)),
                  pl.BlockSpec((tm, LANES), lambda i: (i, 0)),
                  pl.BlockSpec((3, LANES, FOX_WIDTH), lambda i: (0, 0, 0))],
        out_specs=[pl.BlockSpec((tm, wide), lambda i: (i, 0))] * 2,
        compiler_params=_params(("arbitrary",), 32),
        name="fox_prep",
    )(u_b, u_b, c, jnp.asarray(sel, BF16))


def _rope_table_kernel(pos_ref, inv_ref, c_ref, s1_ref, s2_ref):
    half = MLA_ROPE_DIM // 2
    ang = pos_ref[...].astype(F32) * inv_ref[...]
    c = jnp.cos(ang)
    s = jnp.sin(ang)
    lane = lax.broadcasted_iota(I32, ang.shape, 1)
    c_ref[...] = jnp.where(lane < MLA_ROPE_DIM, c, 0.0)
    s1_ref[...] = jnp.where(lane < half, -s, 0.0)
    s2_ref[...] = jnp.where((lane >= half) & (lane < MLA_ROPE_DIM), s, 0.0)


def _rope_tables(positions):
    t = positions.shape[0]
    half = MLA_ROPE_DIM // 2
    inv_freq = ROPE_THETA ** (-jnp.arange(half, dtype=F32) / half)
    inv = jnp.concatenate([inv_freq, inv_freq, jnp.zeros((LANES - 2 * half,), F32)])
    tm = min(1024, t)
    spec = pl.BlockSpec((tm, LANES), lambda i: (i, 0))
    return pl.pallas_call(
        _rope_table_kernel,
        out_shape=[jax.ShapeDtypeStruct((t, LANES), F32)] * 3,
        grid=(t // tm,),
        in_specs=[pl.BlockSpec((tm, 1), lambda i: (i, 0)),
                  pl.BlockSpec((1, LANES), lambda i: (0, 0))],
        out_specs=[spec, spec, spec],
        name="rope_tables",
    )(positions.reshape(t, 1), inv.reshape(1, LANES))


def _rope_group(g, c, s1, s2):
    half = MLA_ROPE_DIM // 2
    return (g * c + pltpu.roll(g, LANES - half, axis=1) * s1
            + pltpu.roll(g, half, axis=1) * s2)


def _mla_prep_kernel(u_ref, gq_ref, gkv_ref, wq_ref, wkv_ref, c_ref, s1_ref, s2_ref,
                     q_ref, k_ref, v_ref):
    def rms(v, g):
        ms = jnp.mean(v * v, axis=-1, keepdims=True)
        return (v * lax.rsqrt(ms + RMS_EPS) * g).astype(BF16)

    c, s1, s2 = c_ref[...], s1_ref[...], s2_ref[...]
    cq = rms(u_ref[:, 0:MLA_RANK], gq_ref[...])
    ckv = rms(u_ref[:, MLA_RANK:2 * MLA_RANK], gkv_ref[...])
    q_pre = jnp.dot(cq, wq_ref[...], preferred_element_type=F32)
    kv = jnp.dot(ckv, wkv_ref[...], preferred_element_type=F32)
    k_rot = _rope_group(u_ref[:, 2 * MLA_RANK:2 * MLA_RANK + LANES], c, s1, s2).astype(BF16)
    for h in range(HEADS):
        lo = h * MLA_PAD_DIM
        mid = lo + HEAD_DIM
        hi = lo + MLA_PAD_DIM
        q_ref[:, lo:mid] = q_pre[:, lo:mid].astype(BF16)
        q_ref[:, mid:hi] = _rope_group(q_pre[:, mid:hi], c, s1, s2).astype(BF16)
        k_ref[:, lo:mid] = kv[:, lo:mid].astype(BF16)
        k_ref[:, mid:hi] = k_rot
        v_ref[:, h * HEAD_DIM:(h + 1) * HEAD_DIM] = kv[:, mid:hi].astype(BF16)


def _mla_prep(u_small, g_cq, g_ckv, w_uq_r, w_ukv, tabs):
    t = u_small.shape[0]
    tm = min(512, t)
    wide = HEADS * MLA_PAD_DIM
    row = lambda w: pl.BlockSpec((tm, w), lambda i: (i, 0))
    const = lambda a: pl.BlockSpec(a.shape, lambda i: (0, 0))
    return pl.pallas_call(
        _mla_prep_kernel,
        out_shape=[jax.ShapeDtypeStruct((t, wide), BF16),
                   jax.ShapeDtypeStruct((t, wide), BF16),
                   jax.ShapeDtypeStruct((t, HEADS * HEAD_DIM), BF16)],
        grid=(t // tm,),
        in_specs=[row(u_small.shape[1]), const(g_cq), const(g_ckv), const(w_uq_r),
                  const(w_ukv), row(LANES), row(LANES), row(LANES)],
        out_specs=[row(wide), row(wide), row(HEADS * HEAD_DIM)],
        compiler_params=_params(("arbitrary",), 48),
        name="mla_prep",
    )(u_small, g_cq, g_ckv, w_uq_r, w_ukv, *tabs)


def _rglru_kernel(y_ref, x_ref, cw_ref, cb_ref, wr_ref, br_ref, wi_ref, bi_ref, lam_ref,
                  o_ref, xprev_sc, h_sc, a_sc, g_sc, hs_sc):
    tt = x_ref.shape[0]

    @pl.when(pl.program_id(0) == 0)
    def _():
        xprev_sc[...] = jnp.zeros(xprev_sc.shape, F32)
        h_sc[...] = jnp.zeros(h_sc.shape, F32)

    x = x_ref[...]
    xext = jnp.concatenate([xprev_sc[...], x], axis=0)
    xprev_sc[...] = x[tt - SUBLANES:, :]
    xc = cb_ref[...]
    for j in range(CONV_WIDTH):
        off = SUBLANES - (CONV_WIDTH - 1) + j
        xc = xc + cw_ref[j:j + 1, :] * xext[off:off + tt, :]

    r_parts, i_parts = [], []
    for n in range(RNN_BLOCKS):
        sl = slice(n * RNN_BLOCK_W, (n + 1) * RNN_BLOCK_W)
        xb = xc[:, sl].astype(BF16)
        r_parts.append(_sigmoid(jnp.dot(xb, wr_ref[n], preferred_element_type=F32)
                                + br_ref[:, sl]))
        i_parts.append(_sigmoid(jnp.dot(xb, wi_ref[n], preferred_element_type=F32)
                                + bi_ref[:, sl]))
    r = jnp.concatenate(r_parts, axis=1)
    gate_i = jnp.concatenate(i_parts, axis=1)
    log_a = (-RG_LRU_C * _softplus(-lam_ref[...])) * r
    a = jnp.exp(log_a)
    a_sc[...] = a
    g_sc[...] = jnp.sqrt(-jnp.tanh(log_a) * (a * a + 1.0)) * (gate_i * xc)

    sub = lax.broadcasted_iota(I32, (SUBLANES, RNN_WIDTH), 0)

    def tile_scan(k, h_in):
        rows = pl.ds(pl.multiple_of(k * SUBLANES, SUBLANES), SUBLANES)
        a = a_sc[rows, :]
        g = g_sc[rows, :]
        for d in (1, 2, 4):
            keep = sub >= d
            g = jnp.where(keep, a * pltpu.roll(g, d, axis=0) + g, g)
            a = jnp.where(keep, a * pltpu.roll(a, d, axis=0), a)
        hs = a * h_in + g
        hs_sc[rows, :] = hs
        return jnp.broadcast_to(hs[SUBLANES - 1:SUBLANES, :], (SUBLANES, RNN_WIDTH))

    h_sc[...] = lax.fori_loop(0, tt // SUBLANES, tile_scan, h_sc[...])

    y = y_ref[...]
    gelu = 0.5 * y * (1.0 + jnp.tanh(np.sqrt(2.0 / np.pi) * (y + 0.044715 * (y * y * y))))
    o_ref[...] = (hs_sc[...] * gelu).astype(o_ref.dtype)


def _rglru(u_a, y_blk, x_blk, conv_w, conv_b, w_rg, b_rg, w_ig, b_ig, lam):
    t = u_a.shape[0]
    tt = min(256, t)
    const2 = lambda a: pl.BlockSpec(a.shape, lambda i: (0, 0))
    const3 = lambda a: pl.BlockSpec(a.shape, lambda i: (0, 0, 0))
    return pl.pallas_call(
        _rglru_kernel,
        out_shape=jax.ShapeDtypeStruct((t, RNN_WIDTH), BF16),
        grid=(t // tt,),
        in_specs=[pl.BlockSpec((tt, RNN_WIDTH), lambda i: (i, y_blk)),
                  pl.BlockSpec((tt, RNN_WIDTH), lambda i: (i, x_blk)),
                  const2(conv_w), const2(conv_b), const3(w_rg), const2(b_rg),
                  const3(w_ig), const2(b_ig), const2(lam)],
        out_specs=pl.BlockSpec((tt, RNN_WIDTH), lambda i: (i, 0)),
        scratch_shapes=[pltpu.VMEM((SUBLANES, RNN_WIDTH), F32),
                        pltpu.VMEM((SUBLANES, RNN_WIDTH), F32),
                        pltpu.VMEM((tt, RNN_WIDTH), F32),
                        pltpu.VMEM((tt, RNN_WIDTH), F32),
                        pltpu.VMEM((tt, RNN_WIDTH), F32)],
        compiler_params=_params(("arbitrary",), 32),
        name="rglru",
    )(u_a, u_a, conv_w, conv_b, w_rg, b_rg, w_ig, b_ig, lam)


def _merge_kernel(a_ref, b_ref, c_ref, wa_ref, wb_ref, wc_ref, g0_ref, g1_ref, g2_ref,
                  bm_ref, o_ref):
    def branch(x_ref, w_ref, g_ref, n):
        y = jnp.dot(x_ref[...], w_ref[...], preferred_element_type=F32)
        return _sigmoid(g_ref[...] + bm_ref[n:n + 1, :]) * y

    mixed = (branch(a_ref, wa_ref, g0_ref, 0) + branch(b_ref, wb_ref, g1_ref, 1)
             + branch(c_ref, wc_ref, g2_ref, 2))
    o_ref[...] = mixed.astype(o_ref.dtype)


def _merge(ya, yb, yc, wa, wb, wc, u_a, b_merge):
    t = ya.shape[0]
    tm, tn = min(512, t), 1024
    nn = D_MODEL // tn
    xin = pl.BlockSpec((tm, ya.shape[1]), lambda j, i: (i, 0))
    win = pl.BlockSpec((ya.shape[1], tn), lambda j, i: (0, j))
    gate = lambda n: pl.BlockSpec((tm, tn), lambda j, i: (i, n * nn + j))
    return pl.pallas_call(
        _merge_kernel,
        out_shape=jax.ShapeDtypeStruct((t, D_MODEL), BF16),
        grid=(nn, t // tm),
        in_specs=[xin, xin, xin, win, win, win, gate(0), gate(1), gate(2),
                  pl.BlockSpec((N_BRANCHES, tn), lambda j, i: (0, j))],
        out_specs=pl.BlockSpec((tm, tn), lambda j, i: (i, j)),
        compiler_params=_params(("arbitrary", "arbitrary"), 48),
        name="merge",
    )(ya, yb, yc, wa, wb, wc, u_a, u_a, u_a, b_merge)


def _outproj_ln_kernel(m_ref, w_ref, x_ref, g_ref, b_ref, of_ref, ob_ref):
    mix = jnp.dot(m_ref[...], w_ref[...], preferred_element_type=F32)
    y = _layer_norm(DEEPNORM_ALPHA * x_ref[...] + mix, g_ref[...], b_ref[...])
    of_ref[...] = y
    ob_ref[...] = y.astype(BF16)


def _outproj_ln(mixed, w_out, x, g, b):
    t = x.shape[0]
    tm = min(256, t)
    row = pl.BlockSpec((tm, D_MODEL), lambda i: (i, 0))
    vec = pl.BlockSpec((1, D_MODEL), lambda i: (0, 0))
    return pl.pallas_call(
        _outproj_ln_kernel,
        out_shape=[jax.ShapeDtypeStruct((t, D_MODEL), F32),
                   jax.ShapeDtypeStruct((t, D_MODEL), BF16)],
        grid=(t // tm,),
        in_specs=[row, pl.BlockSpec((D_MODEL, D_MODEL), lambda i: (0, 0)), row, vec, vec],
        out_specs=[row, row],
        compiler_params=_params(("arbitrary",), 48),
        name="outproj_ln",
    )(mixed, w_out, x, g, b)


def _router_kernel(x_ref, w_ref, b_ref, idx_ref, gate_ref, rank_ref, count_ref, count_sc):
    nt = (((1,), (1,)), ((), ()))
    x = x_ref[...]
    x_hi = x.astype(BF16)
    x_lo = (x - x_hi.astype(F32)).astype(BF16)
    w = w_ref[...]
    w_hi = w.astype(BF16)
    w_lo = (w - w_hi.astype(F32)).astype(BF16)
    logits = (lax.dot_general(w_hi, x_hi, nt, preferred_element_type=F32)
              + lax.dot_general(w_hi, x_lo, nt, preferred_element_type=F32)
              + lax.dot_general(w_lo, x_hi, nt, preferred_element_type=F32)
              + b_ref[...])
    eidx = lax.broadcasted_iota(I32, logits.shape, 0)
    vals, idxs = [], []
    for _ in range(TOP_K):
        m = jnp.max(logits, axis=0, keepdims=True)
        idx = jnp.min(jnp.where(logits == m, eidx, N_EXPERTS), axis=0, keepdims=True)
        vals.append(m)
        idxs.append(idx)
        logits = jnp.where(eidx == idx, -jnp.inf, logits)
    exps = [jnp.exp(v - vals[0]) for v in vals]
    denom = exps[0] + exps[1] + exps[2] + exps[3]
    pad = SUBLANES - TOP_K
    tokens = logits.shape[1]
    idx_ref[...] = jnp.concatenate(idxs + [jnp.zeros((pad, tokens), I32)], axis=0)
    gate_ref[...] = jnp.concatenate([e / denom for e in exps]
                                    + [jnp.zeros((pad, tokens), F32)], axis=0)

    @pl.when(pl.program_id(0) == 0)
    def _():
        count_sc[...] = jnp.zeros(count_sc.shape, F32)

    src = lax.broadcasted_iota(I32, (tokens, tokens), 0)
    dst = lax.broadcasted_iota(I32, (tokens, tokens), 1)
    before = (src < dst).astype(BF16)
    seen = count_sc[...]
    ranks = []
    for idx in idxs:
        hit = eidx == idx
        prefix = jnp.dot(hit.astype(BF16), before, preferred_element_type=F32)
        ranks.append(jnp.sum(jnp.where(hit, seen + prefix, 0.0), axis=0, keepdims=True))
        seen = seen + jnp.sum(hit.astype(F32), axis=1, keepdims=True)
    count_sc[...] = seen
    rank_ref[...] = jnp.concatenate(ranks + [jnp.zeros((pad, tokens), F32)],
                                    axis=0).astype(I32)
    count_ref[...] = seen.astype(I32)


def _router(x, w_router_t, b_router):
    t = x.shape[0]
    tm = min(1024, t)
    tok = pl.BlockSpec((SUBLANES, tm), lambda i: (0, i))
    idx, gate, rank, count = pl.pallas_call(
        _router_kernel,
        out_shape=[jax.ShapeDtypeStruct((SUBLANES, t), I32),
                   jax.ShapeDtypeStruct((SUBLANES, t), F32),
                   jax.ShapeDtypeStruct((SUBLANES, t), I32),
                   jax.ShapeDtypeStruct((N_EXPERTS, 1), I32)],
        grid=(t // tm,),
        in_specs=[pl.BlockSpec((tm, D_MODEL), lambda i: (i, 0)),
                  pl.BlockSpec((N_EXPERTS, D_MODEL), lambda i: (0, 0)),
                  pl.BlockSpec((N_EXPERTS, 1), lambda i: (0, 0))],
        out_specs=[tok, tok, tok, pl.BlockSpec((N_EXPERTS, 1), lambda i: (0, 0))],
        scratch_shapes=[pltpu.VMEM((N_EXPERTS, 1), F32)],
        compiler_params=_params(("arbitrary",), 48),
        name="router",
    )(x, w_router_t, b_router.reshape(N_EXPERTS, 1))
    return idx[:TOP_K].T, gate[:TOP_K].T, rank[:TOP_K].T, count[:, 0]


def _row_copy(src_hbm, row, dst, dst_row, sem):
    return pltpu.make_async_copy(src_hbm.at[pl.ds(row, 1), :],
                                 dst.at[pl.ds(dst_row, 1), :], sem)


def _start_rows(src_hbm, idx_ref, idx0, stride, dst, sem):
    for r in range(dst.shape[0]):
        _row_copy(src_hbm, idx_ref[idx0 + r * stride], dst, r, sem).start()


def _wait_rows(src_hbm, dst, sem):
    pltpu.make_async_copy(src_hbm.at[pl.ds(0, dst.shape[0]), :], dst, sem).wait()


def _expert_schedule(block_expert, counts, n_used):
    b = jnp.arange(block_expert.shape[0], dtype=I32)
    prev = jnp.concatenate([block_expert[:1] - 1, block_expert[:-1]])
    first = (b < n_used[0]) & (block_expert != prev)
    ordinal = jnp.cumsum(first.astype(I32)) - 1
    e = jnp.arange(N_EXPERTS, dtype=I32)
    later_live = (counts > 0)[None, :] & (e[None, :] > e[:, None])
    nxt = jnp.min(jnp.where(later_live, e[None, :], N_EXPERTS), axis=1)
    nxt = jnp.where(nxt == N_EXPERTS, -1, nxt).astype(I32)
    return ordinal, nxt[block_expert]


def _expert_weights(b, n_used, be_ref, ord_ref, nxt_ref, tiles, stages, sem, casts):
    def copies(e, slot):
        return [pltpu.make_async_copy(w.at[e, :, pl.ds(c0, width)], st.at[slot], sem.at[slot])
                for (w, c0, width), st in zip(tiles, stages)]

    e = be_ref[b]
    first = (b < n_used) & ((b == 0) | (e != be_ref[jnp.maximum(b - 1, 0)]))

    @pl.when(b == 0)
    def _():
        for c in copies(e, 0):
            c.start()

    @pl.when(first)
    def _():
        slot = ord_ref[b] % 2
        for c in copies(e, slot):
            c.wait()
        nxt = nxt_ref[b]

        @pl.when(nxt >= 0)
        def _():
            for c in copies(nxt, 1 - slot):
                c.start()

        for st, dst in zip(stages, casts):
            dst[...] = st[slot].astype(BF16)


def _swiglu_block(x, wg_sc, wl_sc, bg_ref, bl_ref, o_ref):
    h_glu = jnp.dot(x, wg_sc[...], preferred_element_type=F32) + bg_ref[0]
    h_lin = jnp.dot(x, wl_sc[...], preferred_element_type=F32) + bl_ref[0]
    h_glu = jnp.minimum(h_glu, SWIGLU_LIMIT)
    h_lin = jnp.clip(h_lin, -SWIGLU_LIMIT, SWIGLU_LIMIT)
    act = h_glu * _sigmoid(SWIGLU_ALPHA * h_glu) * (h_lin + 1.0)
    o_ref[...] = act.astype(o_ref.dtype)


def _expert_up_gather_kernel(be_ref, ord_ref, nxt_ref, nused_ref, tok_ref, x_hbm, w_hbm,
                             bg_ref, bl_ref, o_ref, rows_ref, stage_g, stage_l, wg_sc, wl_sc,
                             rows_a, rows_b, rows_c, wsem, rsem, *, tiles):
    ring = (rows_a, rows_b, rows_c)
    depth = len(ring) - 1
    b = pl.program_id(0)
    n_used = nused_ref[0]
    rb = rows_a.shape[0]

    def start(blk, slot):
        _start_rows(x_hbm, tok_ref, blk * rb, 1, ring[slot], rsem.at[slot])

    @pl.when(b == 0)
    def _():
        for s in range(depth):
            start(jnp.minimum(s, n_used - 1), s)

    _expert_weights(b, n_used, be_ref, ord_ref, nxt_ref,
                    [(w_hbm,) + t for t in tiles], [stage_g, stage_l], wsem, [wg_sc, wl_sc])

    def block(slot):
        cur = ring[slot]
        _wait_rows(x_hbm, cur, rsem.at[slot])
        start(jnp.minimum(b + depth, n_used - 1), (slot + depth) % len(ring))
        x = cur[...].astype(BF16)
        rows_ref[...] = x
        _swiglu_block(x, wg_sc, wl_sc, bg_ref, bl_ref, o_ref)

    for slot in range(len(ring)):
        pl.when((b < n_used) & (b % len(ring) == slot))(functools.partial(block, slot))

    @pl.when(b >= n_used)
    def _():
        o_ref[...] = jnp.zeros(o_ref.shape, o_ref.dtype)
        rows_ref[...] = jnp.zeros(rows_ref.shape, rows_ref.dtype)

    @pl.when(b == pl.num_programs(0) - 1)
    def _():
        for s in range(depth):
            spare = (n_used + s) % len(ring)
            for slot in range(len(ring)):
                pl.when(spare == slot)(
                    functools.partial(_wait_rows, x_hbm, ring[slot], rsem.at[slot]))


def _expert_up_rows_kernel(be_ref, ord_ref, nxt_ref, nused_ref, rows_ref, w_hbm, bg_ref, bl_ref,
                           o_ref, stage_g, stage_l, wg_sc, wl_sc, wsem, *, tiles):
    b = pl.program_id(0)
    n_used = nused_ref[0]
    _expert_weights(b, n_used, be_ref, ord_ref, nxt_ref,
                    [(w_hbm,) + t for t in tiles], [stage_g, stage_l], wsem, [wg_sc, wl_sc])

    @pl.when(b < n_used)
    def _():
        _swiglu_block(rows_ref[...], wg_sc, wl_sc, bg_ref, bl_ref, o_ref)

    @pl.when(b >= n_used)
    def _():
        o_ref[...] = jnp.zeros(o_ref.shape, o_ref.dtype)


UP_TILE = 1024


def _expert_up(x, row_tok, sched, w_up, b_up):
    be, ordinal, nxt, n_used = sched
    n_rows = row_tok.shape[0]
    tn = UP_TILE
    nf = EXPERT_FF // tn
    b_up3 = b_up.reshape(b_up.shape[0], 1, 2 * EXPERT_FF)
    weight_scratch = [pltpu.VMEM((2, D_MODEL, tn), F32), pltpu.VMEM((2, D_MODEL, tn), F32),
                      pltpu.VMEM((D_MODEL, tn), BF16), pltpu.VMEM((D_MODEL, tn), BF16)]
    row_blk = lambda width: pl.BlockSpec((ROW_BLOCK, width), lambda b, *_: (b, 0))

    def bias(col_blk):
        return pl.BlockSpec((1, 1, tn), lambda b, be_, *_: (be_[b], 0, col_blk))

    def tiles(f):
        return ((f * tn, tn), ((nf + f) * tn, tn))

    act0, rows = pl.pallas_call(
        functools.partial(_expert_up_gather_kernel, tiles=tiles(0)),
        out_shape=[jax.ShapeDtypeStruct((n_rows, tn), BF16),
                   jax.ShapeDtypeStruct((n_rows, D_MODEL), BF16)],
        grid_spec=pltpu.PrefetchScalarGridSpec(
            num_scalar_prefetch=5,
            grid=(n_rows // ROW_BLOCK,),
            in_specs=[pl.BlockSpec(memory_space=pl.ANY), pl.BlockSpec(memory_space=pl.ANY),
                      bias(0), bias(nf)],
            out_specs=[row_blk(tn), row_blk(D_MODEL)],
            scratch_shapes=weight_scratch + [
                pltpu.VMEM((ROW_BLOCK, D_MODEL), F32), pltpu.VMEM((ROW_BLOCK, D_MODEL), F32),
                pltpu.VMEM((ROW_BLOCK, D_MODEL), F32),
                pltpu.SemaphoreType.DMA((2,)), pltpu.SemaphoreType.DMA((3,))],
        ),
        compiler_params=_params(("arbitrary",), 58),
        name="expert_up_gather",
    )(be, ordinal, nxt, n_used, row_tok, x, w_up, b_up3, b_up3)
    acts = [act0]
    for f in range(1, nf):
        acts.append(pl.pallas_call(
            functools.partial(_expert_up_rows_kernel, tiles=tiles(f)),
            out_shape=jax.ShapeDtypeStruct((n_rows, tn), BF16),
            grid_spec=pltpu.PrefetchScalarGridSpec(
                num_scalar_prefetch=4,
                grid=(n_rows // ROW_BLOCK,),
                in_specs=[row_blk(D_MODEL), pl.BlockSpec(memory_space=pl.ANY),
                          bias(f), bias(nf + f)],
                out_specs=row_blk(tn),
                scratch_shapes=weight_scratch + [pltpu.SemaphoreType.DMA((2,))],
            ),
            compiler_params=_params(("arbitrary",), 56),
            name="expert_up_rows",
        )(be, ordinal, nxt, n_used, rows, w_up, b_up3, b_up3))
    return acts


def _expert_down_kernel(be_ref, ord_ref, nxt_ref, nused_ref, *refs, n_act):
    act_refs = refs[:n_act]
    w_hbm, bias_ref, o_ref, stage, w_sc, wsem = refs[n_act:]
    b = pl.program_id(0)
    n_used = nused_ref[0]
    _expert_weights(b, n_used, be_ref, ord_ref, nxt_ref, [(w_hbm, 0, D_MODEL)], [stage], wsem,
                    [w_sc])

    @pl.when(b < n_used)
    def _():
        y = bias_ref[0]
        for f, act_ref in enumerate(act_refs):
            k0 = f * act_ref.shape[1]
            y = y + jnp.dot(act_ref[...], w_sc[k0:k0 + act_ref.shape[1], :],
                            preferred_element_type=F32)
        o_ref[...] = y

    @pl.when(b >= n_used)
    def _():
        o_ref[...] = jnp.zeros(o_ref.shape, o_ref.dtype)


def _expert_down(acts, sched, w_down, b_down):
    be, ordinal, nxt, n_used = sched
    n_rows, tn = acts[0].shape
    b_down3 = b_down.reshape(b_down.shape[0], 1, D_MODEL)
    return pl.pallas_call(
        functools.partial(_expert_down_kernel, n_act=len(acts)),
        out_shape=jax.ShapeDtypeStruct((n_rows, D_MODEL), F32),
        grid_spec=pltpu.PrefetchScalarGridSpec(
            num_scalar_prefetch=4,
            grid=(n_rows // ROW_BLOCK,),
            in_specs=[pl.BlockSpec((ROW_BLOCK, tn), lambda b, *_: (b, 0)) for _ in acts] + [
                pl.BlockSpec(memory_space=pl.ANY),
                pl.BlockSpec((1, 1, D_MODEL), lambda b, be_, *_: (be_[b], 0, 0))],
            out_specs=pl.BlockSpec((ROW_BLOCK, D_MODEL), lambda b, *_: (b, 0)),
            scratch_shapes=[pltpu.VMEM((2, EXPERT_FF, D_MODEL), F32),
                            pltpu.VMEM((EXPERT_FF, D_MODEL), BF16),
                            pltpu.SemaphoreType.DMA((2,))],
        ),
        compiler_params=_params(("arbitrary",), 56),
        name="expert_down",
    )(be, ordinal, nxt, n_used, *acts, w_down, b_down3)


def _combine_ln_kernel(pos_ref, y_hbm, gate_ref, x_ref, g_ref, b_ref, of_ref, ob_ref,
                       buf, sem):
    tc = x_ref.shape[0]
    i = pl.program_id(0)
    last = pl.num_programs(0) - 1
    cur = i % 2

    def start(tile, slot):
        for k in range(TOP_K):
            _start_rows(y_hbm, pos_ref, tile * (tc * TOP_K) + k, TOP_K, buf.at[slot, k],
                        sem.at[slot])

    def wait(slot):
        for k in range(TOP_K):
            _wait_rows(y_hbm, buf.at[slot, k], sem.at[slot])

    @pl.when(i == 0)
    def _():
        start(0, 0)

    wait(cur)
    start(jnp.where(i < last, i + 1, 0), 1 - cur)
    gate = gate_ref[...]
    ffn = gate[:, 0:1] * buf[cur, 0]
    for k in range(1, TOP_K):
        ffn = ffn + gate[:, k:k + 1] * buf[cur, k]
    y = _layer_norm(DEEPNORM_ALPHA * x_ref[...] + ffn, g_ref[...], b_ref[...])
    of_ref[...] = y
    ob_ref[...] = y.astype(BF16)

    @pl.when(i == last)
    def _():
        wait(1 - cur)


def _combine_ln(y_rows, pos, gate, x, g, b):
    t = x.shape[0]
    tc = min(128, t)
    row = lambda b_, p: (b_, 0)
    return pl.pallas_call(
        _combine_ln_kernel,
        out_shape=[jax.ShapeDtypeStruct((t, D_MODEL), F32),
                   jax.ShapeDtypeStruct((t, D_MODEL), BF16)],
        grid_spec=pltpu.PrefetchScalarGridSpec(
            num_scalar_prefetch=1,
            grid=(t // tc,),
            in_specs=[pl.BlockSpec(memory_space=pl.ANY),
                      pl.BlockSpec((tc, TOP_K), row),
                      pl.BlockSpec((tc, D_MODEL), row),
                      pl.BlockSpec((1, D_MODEL), lambda b_, p: (0, 0)),
                      pl.BlockSpec((1, D_MODEL), lambda b_, p: (0, 0))],
            out_specs=[pl.BlockSpec((tc, D_MODEL), row), pl.BlockSpec((tc, D_MODEL), row)],
            scratch_shapes=[pltpu.VMEM((2, TOP_K, tc, D_MODEL), F32),
                            pltpu.SemaphoreType.DMA((2,))],
        ),
        compiler_params=_params(("arbitrary",), 32),
        name="combine_ln",
    )(pos, y_rows, gate, x, g, b)


def _dispatch_plan(top_idx, rank, counts):
    t = top_idx.shape[0]
    n_assign = t * TOP_K
    n_blocks = -(-n_assign // ROW_BLOCK) + N_EXPERTS
    e_flat = top_idx.reshape(-1)
    padded = (counts + ROW_BLOCK - 1) // ROW_BLOCK * ROW_BLOCK
    pend = jnp.cumsum(padded)
    pstart = pend - padded
    dest = (pstart[e_flat] + rank.reshape(-1)).astype(I32)
    row_tok = jnp.zeros((n_blocks * ROW_BLOCK,), I32).at[dest].set(
        jnp.arange(n_assign, dtype=I32) // TOP_K)
    block_row0 = jnp.arange(n_blocks, dtype=I32) * ROW_BLOCK
    block_expert = jnp.minimum(jnp.sum((pend[None, :] <= block_row0[:, None]).astype(I32), axis=1),
                               N_EXPERTS - 1)
    n_used = (pend[-1:] // ROW_BLOCK).astype(I32)
    return dest, row_tok, block_expert, n_used


IN_FOX = N_BRANCHES * D_MODEL + 2 * RNN_WIDTH
IN_TAIL = IN_FOX + 3 * FOX_WIDTH


def _prep_in_tail(w_in, layer):
    tail = w_in[layer, :, IN_TAIL:]
    pad = LANES - MLA_ROPE_DIM - HEADS
    return jnp.concatenate([tail[:, HEADS:], tail[:, :HEADS], jnp.zeros((D_MODEL, pad), F32)],
                           axis=1).astype(BF16)


def _prep_uq(w_uq_l):
    w = w_uq_l.reshape(MLA_RANK, HEADS, MLA_QK_DIM) * (MLA_QK_DIM ** -0.5 * LOG2E)
    pad = jnp.zeros((MLA_RANK, HEADS, MLA_PAD_DIM - MLA_QK_DIM), F32)
    return jnp.concatenate([w, pad], axis=2).reshape(MLA_RANK, HEADS * MLA_PAD_DIM).astype(BF16)


def _layer(x, x_bf, tabs, p):
    t = x.shape[0]
    w_in, layer = p['w_in_all'], p['layer']
    w_in_t = jnp.swapaxes(w_in, 1, 2)
    u_a = _in_proj(x_bf, w_in_t, layer, 0, IN_FOX, jnp.ones((IN_FOX,), F32), F32,
                   "in_proj_gates_lru")
    q_scale = jnp.concatenate([jnp.full((FOX_WIDTH,), HEAD_DIM ** -0.5 * LOG2E, F32),
                               jnp.ones((2 * FOX_WIDTH,), F32)])
    u_b = _in_proj(x_bf, w_in_t, layer, IN_FOX, 3 * FOX_WIDTH, q_scale, BF16, "in_proj_fox")
    w_c = _prep_in_tail(w_in, layer)
    u_c = _matmul(x_bf, w_c, F32, 1024, w_c.shape[1], "in_proj_small")

    gate_blocks = N_BRANCHES * D_MODEL // RNN_WIDTH
    y_a = _rglru(u_a, gate_blocks, gate_blocks + 1,
                 p['conv_w'], p['conv_b'].reshape(1, RNN_WIDTH),
                 p['w_rec_gate'].astype(BF16), p['b_rec_gate'].reshape(1, RNN_WIDTH),
                 p['w_inp_gate'].astype(BF16), p['b_inp_gate'].reshape(1, RNN_WIDTH),
                 p['lru_lambda'].reshape(1, RNN_WIDTH))

    fl_col = 2 * MLA_RANK + MLA_ROPE_DIM
    cum = _fox_cum(u_c[:, fl_col:fl_col + HEADS], p['b_forget'])
    q_x, k_x = _fox_prep(u_b, cum)
    y_b = _causal_attention(q_x, k_x, u_b, 2 * FOX_WIDTH, "fox_attention")

    q_f, k_f, v_c = _mla_prep(u_c, p['g_cq'].reshape(1, MLA_RANK),
                              p['g_ckv'].reshape(1, MLA_RANK), _prep_uq(p['w_uq']),
                              p['w_ukv'].astype(BF16), tabs)
    y_c = _causal_attention(q_f, k_f, v_c, 0, "mla_attention")

    mixed = _merge(y_a, y_b, y_c, p['w_proj_lru'].astype(BF16), p['w_proj_fox'].astype(BF16),
                   p['w_proj_mla'].astype(BF16), u_a, p['b_merge'])
    x, x_bf = _outproj_ln(mixed, p['w_out'].astype(BF16), x,
                          p['ln1_g'].reshape(1, D_MODEL), p['ln1_b'].reshape(1, D_MODEL))

    top_idx, gate, rank, counts = _router(x, p['w_router'].T, p['b_router'])
    dest, row_tok, block_expert, n_used = _dispatch_plan(top_idx, rank, counts)
    ordinal, nxt = _expert_schedule(block_expert, counts, n_used)
    base = p['expert_base']
    sched = (block_expert + base, ordinal, jnp.where(nxt >= 0, nxt + base, -1), n_used)
    acts = _expert_up(x, row_tok, sched, p['w_up_all'], p['b_up_all'])
    y_rows = _expert_down(acts, sched, p['w_down_all'], p['b_down_all'])
    del t
    return _combine_ln(y_rows, dest, gate, x, p['ln2_g'].reshape(1, D_MODEL),
                       p['ln2_b'].reshape(1, D_MODEL))


_LAYER_PARAMS = ('w_in', 'b_merge', 'b_forget', 'conv_w', 'conv_b', 'w_rec_gate', 'b_rec_gate',
                 'w_inp_gate', 'b_inp_gate', 'lru_lambda', 'g_cq', 'g_ckv', 'w_uq', 'w_ukv',
                 'w_proj_lru', 'w_proj_fox', 'w_proj_mla', 'w_out', 'ln1_g', 'ln1_b',
                 'w_router', 'b_router', 'w_up', 'b_up', 'w_down', 'b_down', 'ln2_g', 'ln2_b')


def kernel(x, positions, w_in, b_merge, b_forget, conv_w, conv_b, w_rec_gate, b_rec_gate,
           w_inp_gate, b_inp_gate, lru_lambda, g_cq, g_ckv, w_uq, w_ukv, w_proj_lru,
           w_proj_fox, w_proj_mla, w_out, ln1_g, ln1_b, w_router, b_router, w_up, b_up,
           w_down, b_down, ln2_g, ln2_b):
    stacked = dict(zip(_LAYER_PARAMS, (
        w_in, b_merge, b_forget, conv_w, conv_b, w_rec_gate, b_rec_gate, w_inp_gate,
        b_inp_gate, lru_lambda, g_cq, g_ckv, w_uq, w_ukv, w_proj_lru, w_proj_fox,
        w_proj_mla, w_out, ln1_g, ln1_b, w_router, b_router, w_up, b_up, w_down, b_down,
        ln2_g, ln2_b)))
    batch, seq, _ = x.shape
    assert batch == 1
    xt = x.reshape(seq, D_MODEL)
    x_bf = xt.astype(BF16)
    tabs = _rope_tables(positions.reshape(seq))
    expert_stack = {'w_up', 'b_up', 'w_down', 'b_down'}
    shared = {k + '_all': stacked[k].reshape((-1,) + stacked[k].shape[2:]) for k in expert_stack}
    shared['w_in_all'] = w_in
    in_place = expert_stack | {'w_in'}
    for l in range(w_in.shape[0]):
        p = {k: v[l] for k, v in stacked.items() if k not in in_place}
        xt, x_bf = _layer(xt, x_bf, tabs,
                          dict(p, layer=l, expert_base=l * N_EXPERTS, **shared))
    return xt.reshape(batch, seq, D_MODEL)
```

```python
import functools

import jax
import jax.numpy as jnp
import numpy as np
from jax import lax
from jax.experimental import pallas as pl
from jax.experimental.pallas import tpu as pltpu

F32 = jnp.float32
BF16 = jnp.bfloat16
I32 = jnp.int32

D_MODEL = 2048
N_BRANCHES = 3
DEEPNORM_ALPHA = (2 * 2) ** 0.25
LN_EPS = 1e-5
RMS_EPS = 1e-6
RNN_WIDTH = 1024
RNN_BLOCKS = 8
RNN_BLOCK_W = RNN_WIDTH // RNN_BLOCKS
CONV_WIDTH = 4
RG_LRU_C = 8.0
HEADS = 8
HEAD_DIM = 128
FOX_WIDTH = HEADS * HEAD_DIM
MLA_RANK = 512
MLA_ROPE_DIM = 64
MLA_QK_DIM = HEAD_DIM + MLA_ROPE_DIM
ATTN_QK_DIM = 256
MLA_PAD_DIM = ATTN_QK_DIM
LOG2E = float(np.log2(np.e))
ROPE_THETA = 10000.0
N_EXPERTS = 32
TOP_K = 4
EXPERT_FF = D_MODEL
SWIGLU_ALPHA = 1.702
SWIGLU_LIMIT = 7.0

LANES = 128
SUBLANES = 8
V7X_VMEM_BYTES = 64 * 1024 * 1024

ROW_BLOCK = 256
ATTN_BLOCK = 512
GATHER_UNROLL = 8
ATTN_GROUP = 8
MASK_VALUE = -1e30


def _params(semantics, vmem_mib):
    assert vmem_mib * 1024 * 1024 < V7X_VMEM_BYTES
    return pltpu.CompilerParams(dimension_semantics=semantics,
                                vmem_limit_bytes=vmem_mib * 1024 * 1024)


def _split3(x):
    hi = x.astype(BF16)
    r = x - hi.astype(F32)
    mid = r.astype(BF16)
    lo = (r - mid.astype(F32)).astype(BF16)
    return hi, mid, lo


def _sigmoid(x):
    return 1.0 / (1.0 + jnp.exp(-x))


def _softplus(z):
    return jnp.maximum(z, 0.0) + jnp.log1p(jnp.exp(-jnp.abs(z)))


def _layer_norm(v, g, b):
    mu = jnp.mean(v, axis=-1, keepdims=True)
    c = v - mu
    var = jnp.mean(c * c, axis=-1, keepdims=True)
    return c * lax.rsqrt(var + LN_EPS) * g + b


def _mm_kernel(x_ref, w_ref, o_ref):
    o_ref[...] = jnp.dot(x_ref[...], w_ref[...],
                         preferred_element_type=F32).astype(o_ref.dtype)


def _matmul(x, w, out_dtype, tm, tn, name):
    m, k = x.shape
    n = w.shape[1]
    tm, tn = min(tm, m), min(tn, n)
    return pl.pallas_call(
        _mm_kernel,
        out_shape=jax.ShapeDtypeStruct((m, n), out_dtype),
        grid=(m // tm, n // tn),
        in_specs=[pl.BlockSpec((tm, k), lambda i, j: (i, 0)),
                  pl.BlockSpec((k, tn), lambda i, j: (0, j))],
        out_specs=pl.BlockSpec((tm, tn), lambda i, j: (i, j)),
        compiler_params=_params(("parallel", "arbitrary"), 48),
        name=name,
    )(x, w)


def _in_proj_kernel(x_ref, w_ref, s_ref, o_ref, w_sc):
    @pl.when(pl.program_id(1) == 0)
    def _():
        w_sc[...] = (w_ref[0] * s_ref[...]).T.astype(BF16)

    o_ref[...] = jnp.dot(x_ref[...], w_sc[...],
                         preferred_element_type=F32).astype(o_ref.dtype)


def _in_proj(x, w_stack_t, layer, col0, ncols, col_scale, out_dtype, name):
    m, k = x.shape
    tm, tn = min(1024, m), 1024
    assert col0 % tn == 0 and ncols % tn == 0
    blk0 = col0 // tn
    return pl.pallas_call(
        _in_proj_kernel,
        out_shape=jax.ShapeDtypeStruct((m, ncols), out_dtype),
        grid=(ncols // tn, m // tm),
        in_specs=[pl.BlockSpec((tm, k), lambda j, i: (i, 0)),
                  pl.BlockSpec((1, tn, k), lambda j, i: (layer, blk0 + j, 0)),
                  pl.BlockSpec((tn, 1), lambda j, i: (j, 0))],
        out_specs=pl.BlockSpec((tm, tn), lambda j, i: (i, j)),
        scratch_shapes=[pltpu.VMEM((k, tn), BF16)],
        compiler_params=_params(("arbitrary", "arbitrary"), 48),
        name=name,
    )(x, w_stack_t, col_scale.reshape(ncols, 1))


def _fox_cum_kernel(fl_ref, b_ref, o_ref, *, chunks):
    z = fl_ref[...] + b_ref[...]
    lf = jnp.minimum(z, 0.0) - jnp.log1p(jnp.exp(-jnp.abs(z)))
    rows = lf.shape[0]
    s = lax.broadcasted_iota(I32, (LANES, LANES), 0)
    t = lax.broadcasted_iota(I32, (LANES, LANES), 1)
    tri = (s <= t).astype(BF16)
    incl = None
    for part in _split3(lf):
        d = jnp.dot(part, tri, preferred_element_type=F32)
        incl = d if incl is None else incl + d
    i = lax.broadcasted_iota(I32, (rows, rows), 0)
    j = lax.broadcasted_iota(I32, (rows, rows), 1)
    lower = ((i // chunks == j // chunks) & (j < i)).astype(BF16)
    offs = None
    for part in _split3(incl):
        d = jnp.dot(lower, part, preferred_element_type=F32)
        offs = d if offs is None else offs + d
    o_ref[...] = incl + offs[:, LANES - 1:LANES]


def _fox_cum(f_logit, b_f):
    t = f_logit.shape[0]
    chunks = t // LANES
    fl = f_logit.T.reshape(HEADS * chunks, LANES)
    b = jnp.repeat(b_f.astype(F32), chunks).reshape(HEADS * chunks, 1)
    out = pl.pallas_call(
        functools.partial(_fox_cum_kernel, chunks=chunks),
        out_shape=jax.ShapeDtypeStruct((HEADS * chunks, LANES), F32),
        name="fox_cum",
    )(fl, b)
    return out.reshape(HEADS, t)


def _attn_kernel(qi_ref, ki_ref, q_ref, k_ref, v_ref, o_ref, m_sc, l_sc, acc_sc, *, group):
    p = pl.program_id(1)
    qi = qi_ref[p]
    ki = ki_ref[p]
    blk = q_ref.shape[0]

    @pl.when(ki == 0)
    def _():
        m_sc[...] = jnp.full(m_sc.shape, MASK_VALUE, F32)
        l_sc[...] = jnp.zeros(l_sc.shape, F32)
        acc_sc[...] = jnp.zeros(acc_sc.shape, F32)

    def step(masked):
        for g in range(group):
            qk = slice(g * ATTN_QK_DIM, (g + 1) * ATTN_QK_DIM)
            vo = slice(g * HEAD_DIM, (g + 1) * HEAD_DIM)
            s = lax.dot_general(q_ref[:, qk], k_ref[:, qk], (((1,), (1,)), ((), ())),
                                preferred_element_type=F32)
            if masked:
                row = lax.broadcasted_iota(I32, s.shape, 0)
                col = lax.broadcasted_iota(I32, s.shape, 1)
                s = jnp.where(col <= row, s, MASK_VALUE)
            chunks = [s[:, c * LANES:(c + 1) * LANES] for c in range(blk // LANES)]
            cmax = functools.reduce(jnp.maximum, chunks)
            m_prev = m_sc[g]
            m_new = jnp.maximum(m_prev, jnp.max(cmax, axis=1, keepdims=True))
            alpha = jnp.exp2(m_prev - m_new)
            probs = [jnp.exp2(c - m_new) for c in chunks]
            l_sc[g] = alpha * l_sc[g] + functools.reduce(jnp.add, probs)
            pb = jnp.concatenate(probs, axis=1).astype(BF16)
            acc_sc[g] = alpha * acc_sc[g] + jnp.dot(pb, v_ref[:, vo],
                                                    preferred_element_type=F32)
            m_sc[g] = m_new

    @pl.when(ki < qi)
    def _():
        step(False)

    @pl.when(ki == qi)
    def _():
        step(True)
        for g in range(group):
            denom = jnp.sum(l_sc[g], axis=1, keepdims=True)
            o_ref[:, g * HEAD_DIM:(g + 1) * HEAD_DIM] = (acc_sc[g] / denom).astype(o_ref.dtype)


def _causal_attention(q, k, v, v_col, name):
    t = q.shape[0]
    blk = min(ATTN_BLOCK, t)
    nq = t // blk
    group = ATTN_GROUP
    pairs = [(a, b) for a in range(nq) for b in range(a + 1)]
    qi_tab = jnp.asarray(np.array([a for a, _ in pairs], np.int32))
    ki_tab = jnp.asarray(np.array([b for _, b in pairs], np.int32))
    v_blk = v_col // (group * HEAD_DIM)
    return pl.pallas_call(
        functools.partial(_attn_kernel, group=group),
        out_shape=jax.ShapeDtypeStruct((t, HEADS * HEAD_DIM), BF16),
        grid_spec=pltpu.PrefetchScalarGridSpec(
            num_scalar_prefetch=2,
            grid=(HEADS // group, len(pairs)),
            in_specs=[
                pl.BlockSpec((blk, group * ATTN_QK_DIM), lambda h, p, qi, ki: (qi[p], h)),
                pl.BlockSpec((blk, group * ATTN_QK_DIM), lambda h, p, qi, ki: (ki[p], h)),
                pl.BlockSpec((blk, group * HEAD_DIM), lambda h, p, qi, ki: (ki[p], v_blk + h)),
            ],
            out_specs=pl.BlockSpec((blk, group * HEAD_DIM), lambda h, p, qi, ki: (qi[p], h)),
            scratch_shapes=[pltpu.VMEM((group, blk, LANES), F32),
                            pltpu.VMEM((group, blk, LANES), F32),
                            pltpu.VMEM((group, blk, HEAD_DIM), F32)],
        ),
        compiler_params=_params(("parallel", "arbitrary"), 32),
        name=name,
    )(qi_tab, ki_tab, q, k, v)


def _fox_prep_kernel(q_ref, k_ref, c_ref, sel_ref, qo_ref, ko_ref):
    bias = None
    for j, part in enumerate(_split3(c_ref[...])):
        d = jnp.dot(part, sel_ref[j], preferred_element_type=F32)
        bias = d if bias is None else bias + d
    lane = lax.broadcasted_iota(I32, (q_ref.shape[0], LANES), 1)
    for h in range(HEADS):
        src = slice(h * HEAD_DIM, (h + 1) * HEAD_DIM)
        lo = h * ATTN_QK_DIM
        g = bias[:, src]
        qo_ref[:, lo:lo + HEAD_DIM] = q_ref[:, src]
        ko_ref[:, lo:lo + HEAD_DIM] = k_ref[:, src]
        qo_ref[:, lo + HEAD_DIM:lo + ATTN_QK_DIM] = jnp.where(
            (lane >= 3) & (lane < 6), 1.0, g).astype(BF16)
        ko_ref[:, lo + HEAD_DIM:lo + ATTN_QK_DIM] = jnp.where(
            lane < 3, 1.0, -pltpu.roll(g, 3, axis=1)).astype(BF16)


def _fox_prep(u_b, cum):
    t = u_b.shape[0]
    tm = min(512, t)
    c = jnp.pad(cum.T * LOG2E, ((0, 0), (0, LANES - HEADS)))
    sel = np.zeros((3, LANES, FOX_WIDTH), np.float32)
    for j in range(3):
        for h in range(HEADS):
            sel[j, h, h * HEAD_DIM + j] = 1.0
    wide = HEADS * ATTN_QK_DIM
    return pl.pallas_call(
        _fox_prep_kernel,
        out_shape=[jax.ShapeDtypeStruct((t, wide), BF16)] * 2,
        grid=(t // tm,),
        in_specs=[pl.BlockSpec((tm, FOX_WIDTH), lambda i: (i, 0)),
                  pl.BlockSpec((tm, FOX_WIDTH), lambda i: (i, 1)),
                  pl.BlockSpec((tm, LANES), lambda i: (i, 0)),
                  pl.BlockSpec((3, LANES, FOX_WIDTH), lambda i: (0, 0, 0))],
        out_specs=[pl.BlockSpec((tm, wide), lambda i: (i, 0))] * 2,
        compiler_params=_params(("arbitrary",), 32),
        name="fox_prep",
    )(u_b, u_b, c, jnp.asarray(sel, BF16))


def _rope_table_kernel(pos_ref, inv_ref, c_ref, s1_ref, s2_ref):
    half = MLA_ROPE_DIM // 2
    ang = pos_ref[...].astype(F32) * inv_ref[...]
    c = jnp.cos(ang)
    s = jnp.sin(ang)
    lane = lax.broadcasted_iota(I32, ang.shape, 1)
    c_ref[...] = jnp.where(lane < MLA_ROPE_DIM, c, 0.0)
    s1_ref[...] = jnp.where(lane < half, -s, 0.0)
    s2_ref[...] = jnp.where((lane >= half) & (lane < MLA_ROPE_DIM), s, 0.0)


def _rope_tables(positions):
    t = positions.shape[0]
    half = MLA_ROPE_DIM // 2
    inv_freq = ROPE_THETA ** (-jnp.arange(half, dtype=F32) / half)
    inv = jnp.concatenate([inv_freq, inv_freq, jnp.zeros((LANES - 2 * half,), F32)])
    tm = min(1024, t)
    spec = pl.BlockSpec((tm, LANES), lambda i: (i, 0))
    return pl.pallas_call(
        _rope_table_kernel,
        out_shape=[jax.ShapeDtypeStruct((t, LANES), F32)] * 3,
        grid=(t // tm,),
        in_specs=[pl.BlockSpec((tm, 1), lambda i: (i, 0)),
                  pl.BlockSpec((1, LANES), lambda i: (0, 0))],
        out_specs=[spec, spec, spec],
        name="rope_tables",
    )(positions.reshape(t, 1), inv.reshape(1, LANES))


def _rope_group(g, c, s1, s2):
    half = MLA_ROPE_DIM // 2
    return (g * c + pltpu.roll(g, LANES - half, axis=1) * s1
            + pltpu.roll(g, half, axis=1) * s2)


def _mla_prep_kernel(u_ref, gq_ref, gkv_ref, wq_ref, wkv_ref, c_ref, s1_ref, s2_ref,
                     q_ref, k_ref, v_ref):
    def rms(v, g):
        ms = jnp.mean(v * v, axis=-1, keepdims=True)
        return (v * lax.rsqrt(ms + RMS_EPS) * g).astype(BF16)

    c, s1, s2 = c_ref[...], s1_ref[...], s2_ref[...]
    cq = rms(u_ref[:, 0:MLA_RANK], gq_ref[...])
    ckv = rms(u_ref[:, MLA_RANK:2 * MLA_RANK], gkv_ref[...])
    q_pre = jnp.dot(cq, wq_ref[...], preferred_element_type=F32)
    kv = jnp.dot(ckv, wkv_ref[...], preferred_element_type=F32)
    k_rot = _rope_group(u_ref[:, 2 * MLA_RANK:2 * MLA_RANK + LANES], c, s1, s2).astype(BF16)
    for h in range(HEADS):
        lo = h * MLA_PAD_DIM
        mid = lo + HEAD_DIM
        hi = lo + MLA_PAD_DIM
        q_ref[:, lo:mid] = q_pre[:, lo:mid].astype(BF16)
        q_ref[:, mid:hi] = _rope_group(q_pre[:, mid:hi], c, s1, s2).astype(BF16)
        k_ref[:, lo:mid] = kv[:, lo:mid].astype(BF16)
        k_ref[:, mid:hi] = k_rot
        v_ref[:, h * HEAD_DIM:(h + 1) * HEAD_DIM] = kv[:, mid:hi].astype(BF16)


def _mla_prep(u_small, g_cq, g_ckv, w_uq_r, w_ukv, tabs):
    t = u_small.shape[0]
    tm = min(512, t)
    wide = HEADS * MLA_PAD_DIM
    row = lambda w: pl.BlockSpec((tm, w), lambda i: (i, 0))
    const = lambda a: pl.BlockSpec(a.shape, lambda i: (0, 0))
    return pl.pallas_call(
        _mla_prep_kernel,
        out_shape=[jax.ShapeDtypeStruct((t, wide), BF16),
                   jax.ShapeDtypeStruct((t, wide), BF16),
                   jax.ShapeDtypeStruct((t, HEADS * HEAD_DIM), BF16)],
        grid=(t // tm,),
        in_specs=[row(u_small.shape[1]), const(g_cq), const(g_ckv), const(w_uq_r),
                  const(w_ukv), row(LANES), row(LANES), row(LANES)],
        out_specs=[row(wide), row(wide), row(HEADS * HEAD_DIM)],
        compiler_params=_params(("arbitrary",), 48),
        name="mla_prep",
    )(u_small, g_cq, g_ckv, w_uq_r, w_ukv, *tabs)


def _rglru_kernel(y_ref, x_ref, cw_ref, cb_ref, wr_ref, br_ref, wi_ref, bi_ref, lam_ref,
                  o_ref, xprev_sc, h_sc, a_sc, g_sc, hs_sc):
    tt = x_ref.shape[0]

    @pl.when(pl.program_id(0) == 0)
    def _():
        xprev_sc[...] = jnp.zeros(xprev_sc.shape, F32)
        h_sc[...] = jnp.zeros(h_sc.shape, F32)

    x = x_ref[...]
    xext = jnp.concatenate([xprev_sc[...], x], axis=0)
    xprev_sc[...] = x[tt - SUBLANES:, :]
    xc = cb_ref[...]
    for j in range(CONV_WIDTH):
        off = SUBLANES - (CONV_WIDTH - 1) + j
        xc = xc + cw_ref[j:j + 1, :] * xext[off:off + tt, :]

    r_parts, i_parts = [], []
    for n in range(RNN_BLOCKS):
        sl = slice(n * RNN_BLOCK_W, (n + 1) * RNN_BLOCK_W)
        xb = xc[:, sl].astype(BF16)
        r_parts.append(_sigmoid(jnp.dot(xb, wr_ref[n], preferred_element_type=F32)
                                + br_ref[:, sl]))
        i_parts.append(_sigmoid(jnp.dot(xb, wi_ref[n], preferred_element_type=F32)
                                + bi_ref[:, sl]))
    r = jnp.concatenate(r_parts, axis=1)
    gate_i = jnp.concatenate(i_parts, axis=1)
    log_a = (-RG_LRU_C * _softplus(-lam_ref[...])) * r
    a = jnp.exp(log_a)
    a_sc[...] = a
    g_sc[...] = jnp.sqrt(-jnp.tanh(log_a) * (a * a + 1.0)) * (gate_i * xc)

    sub = lax.broadcasted_iota(I32, (SUBLANES, RNN_WIDTH), 0)

    def tile_scan(k, h_in):
        rows = pl.ds(pl.multiple_of(k * SUBLANES, SUBLANES), SUBLANES)
        a = a_sc[rows, :]
        g = g_sc[rows, :]
        for d in (1, 2, 4):
            keep = sub >= d
            g = jnp.where(keep, a * pltpu.roll(g, d, axis=0) + g, g)
            a = jnp.where(keep, a * pltpu.roll(a, d, axis=0), a)
        hs = a * h_in + g
        hs_sc[rows, :] = hs
        return jnp.broadcast_to(hs[SUBLANES - 1:SUBLANES, :], (SUBLANES, RNN_WIDTH))

    h_sc[...] = lax.fori_loop(0, tt // SUBLANES, tile_scan, h_sc[...])

    y = y_ref[...]
    gelu = 0.5 * y * (1.0 + jnp.tanh(np.sqrt(2.0 / np.pi) * (y + 0.044715 * (y * y * y))))
    o_ref[...] = (hs_sc[...] * gelu).astype(o_ref.dtype)


def _rglru(u_a, y_blk, x_blk, conv_w, conv_b, w_rg, b_rg, w_ig, b_ig, lam):
    t = u_a.shape[0]
    tt = min(256, t)
    const2 = lambda a: pl.BlockSpec(a.shape, lambda i: (0, 0))
    const3 = lambda a: pl.BlockSpec(a.shape, lambda i: (0, 0, 0))
    return pl.pallas_call(
        _rglru_kernel,
        out_shape=jax.ShapeDtypeStruct((t, RNN_WIDTH), BF16),
        grid=(t // tt,),
        in_specs=[pl.BlockSpec((tt, RNN_WIDTH), lambda i: (i, y_blk)),
                  pl.BlockSpec((tt, RNN_WIDTH), lambda i: (i, x_blk)),
                  const2(conv_w), const2(conv_b), const3(w_rg), const2(b_rg),
                  const3(w_ig), const2(b_ig), const2(lam)],
        out_specs=pl.BlockSpec((tt, RNN_WIDTH), lambda i: (i, 0)),
        scratch_shapes=[pltpu.VMEM((SUBLANES, RNN_WIDTH), F32),
                        pltpu.VMEM((SUBLANES, RNN_WIDTH), F32),
                        pltpu.VMEM((tt, RNN_WIDTH), F32),
                        pltpu.VMEM((tt, RNN_WIDTH), F32),
                        pltpu.VMEM((tt, RNN_WIDTH), F32)],
        compiler_params=_params(("arbitrary",), 32),
        name="rglru",
    )(u_a, u_a, conv_w, conv_b, w_rg, b_rg, w_ig, b_ig, lam)


def _merge_kernel(a_ref, b_ref, c_ref, wa_ref, wb_ref, wc_ref, g0_ref, g1_ref, g2_ref,
                  bm_ref, o_ref):
    def branch(x_ref, w_ref, g_ref, n):
        y = jnp.dot(x_ref[...], w_ref[0].astype(BF16), preferred_element_type=F32)
        return _sigmoid(g_ref[...] + bm_ref[n:n + 1, :]) * y

    mixed = (branch(a_ref, wa_ref, g0_ref, 0) + branch(b_ref, wb_ref, g1_ref, 1)
             + branch(c_ref, wc_ref, g2_ref, 2))
    o_ref[...] = mixed.astype(o_ref.dtype)


def _merge(ya, yb, yc, wa, wb, wc, layer, u_a, b_merge):
    t = ya.shape[0]
    tm, tn = min(512, t), 1024
    nn = D_MODEL // tn
    xin = pl.BlockSpec((tm, ya.shape[1]), lambda j, i: (i, 0))
    win = pl.BlockSpec((1, ya.shape[1], tn), lambda j, i: (layer, 0, j))
    gate = lambda n: pl.BlockSpec((tm, tn), lambda j, i: (i, n * nn + j))
    return pl.pallas_call(
        _merge_kernel,
        out_shape=jax.ShapeDtypeStruct((t, D_MODEL), BF16),
        grid=(nn, t // tm),
        in_specs=[xin, xin, xin, win, win, win, gate(0), gate(1), gate(2),
                  pl.BlockSpec((N_BRANCHES, tn), lambda j, i: (0, j))],
        out_specs=pl.BlockSpec((tm, tn), lambda j, i: (i, j)),
        compiler_params=_params(("arbitrary", "arbitrary"), 56),
        name="merge",
    )(ya, yb, yc, wa, wb, wc, u_a, u_a, u_a, b_merge)


def _outproj_ln_kernel(m_ref, w_ref, x_ref, g_ref, b_ref, o_ref):
    mix = jnp.dot(m_ref[...], w_ref[0].astype(BF16), preferred_element_type=F32)
    o_ref[...] = _layer_norm(DEEPNORM_ALPHA * x_ref[...] + mix, g_ref[...], b_ref[...])


def _outproj_ln(mixed, w_out, layer, x, g, b):
    t = x.shape[0]
    tm = min(256, t)
    row = pl.BlockSpec((tm, D_MODEL), lambda i: (i, 0))
    vec = pl.BlockSpec((1, D_MODEL), lambda i: (0, 0))
    return pl.pallas_call(
        _outproj_ln_kernel,
        out_shape=jax.ShapeDtypeStruct((t, D_MODEL), F32),
        grid=(t // tm,),
        in_specs=[row, pl.BlockSpec((1, D_MODEL, D_MODEL), lambda i: (layer, 0, 0)), row, vec,
                  vec],
        out_specs=row,
        compiler_params=_params(("arbitrary",), 56),
        name="outproj_ln",
    )(mixed, w_out, x, g, b)


def _route(x, w_ref, b_ref, idx_ref, gate_ref, rank_ref, count_ref, count_sc):
    nt = (((1,), (1,)), ((), ()))
    x_hi = x.astype(BF16)
    x_lo = (x - x_hi.astype(F32)).astype(BF16)
    w = w_ref[...]
    w_hi = w.astype(BF16)
    w_lo = (w - w_hi.astype(F32)).astype(BF16)
    logits = (lax.dot_general(w_hi, x_hi, nt, preferred_element_type=F32)
              + lax.dot_general(w_hi, x_lo, nt, preferred_element_type=F32)
              + lax.dot_general(w_lo, x_hi, nt, preferred_element_type=F32)
              + b_ref[...])
    eidx = lax.broadcasted_iota(I32, logits.shape, 0)
    vals, idxs = [], []
    for _ in range(TOP_K):
        m = jnp.max(logits, axis=0, keepdims=True)
        idx = jnp.min(jnp.where(logits == m, eidx, N_EXPERTS), axis=0, keepdims=True)
        vals.append(m)
        idxs.append(idx)
        logits = jnp.where(eidx == idx, -jnp.inf, logits)
    exps = [jnp.exp(v - vals[0]) for v in vals]
    denom = exps[0] + exps[1] + exps[2] + exps[3]
    pad = SUBLANES - TOP_K
    tokens = logits.shape[1]
    idx_ref[...] = jnp.concatenate(idxs + [jnp.zeros((pad, tokens), I32)], axis=0)
    gate_ref[...] = jnp.concatenate([e / denom for e in exps]
                                    + [jnp.zeros((pad, tokens), F32)], axis=0)

    @pl.when(pl.program_id(0) == 0)
    def _():
        count_sc[...] = jnp.zeros(count_sc.shape, F32)

    src = lax.broadcasted_iota(I32, (tokens, tokens), 0)
    dst = lax.broadcasted_iota(I32, (tokens, tokens), 1)
    before = (src < dst).astype(BF16)
    seen = count_sc[...]
    ranks = []
    for idx in idxs:
        hit = eidx == idx
        prefix = jnp.dot(hit.astype(BF16), before, preferred_element_type=F32)
        ranks.append(jnp.sum(jnp.where(hit, seen + prefix, 0.0), axis=0, keepdims=True))
        seen = seen + jnp.sum(hit.astype(F32), axis=1, keepdims=True)
    count_sc[...] = seen
    rank_ref[...] = jnp.concatenate(ranks + [jnp.zeros((pad, tokens), F32)],
                                    axis=0).astype(I32)
    count_ref[...] = seen.astype(I32)


def _router_kernel(x_ref, w_ref, b_ref, idx_ref, gate_ref, rank_ref, count_ref, count_sc):
    _route(x_ref[...], w_ref, b_ref, idx_ref, gate_ref, rank_ref, count_ref, count_sc)


def _router(x, w_router_t, b_router):
    t = x.shape[0]
    tm = min(1024, t)
    tok = pl.BlockSpec((SUBLANES, tm), lambda i: (0, i))
    idx, gate, rank, count = pl.pallas_call(
        _router_kernel,
        out_shape=[jax.ShapeDtypeStruct((SUBLANES, t), I32),
                   jax.ShapeDtypeStruct((SUBLANES, t), F32),
                   jax.ShapeDtypeStruct((SUBLANES, t), I32),
                   jax.ShapeDtypeStruct((N_EXPERTS, 1), I32)],
        grid=(t // tm,),
        in_specs=[pl.BlockSpec((tm, D_MODEL), lambda i: (i, 0)),
                  pl.BlockSpec((N_EXPERTS, D_MODEL), lambda i: (0, 0)),
                  pl.BlockSpec((N_EXPERTS, 1), lambda i: (0, 0))],
        out_specs=[tok, tok, tok, pl.BlockSpec((N_EXPERTS, 1), lambda i: (0, 0))],
        scratch_shapes=[pltpu.VMEM((N_EXPERTS, 1), F32)],
        compiler_params=_params(("arbitrary",), 48),
        name="router",
    )(x, w_router_t, b_router.reshape(N_EXPERTS, 1))
    return idx[:TOP_K].T, gate[:TOP_K].T, rank[:TOP_K].T, count[:, 0]


def _row_copy(src_hbm, row, dst, dst_row, sem):
    return pltpu.make_async_copy(src_hbm.at[pl.ds(row, 1), :],
                                 dst.at[pl.ds(dst_row, 1), :], sem)


def _start_rows(src_hbm, idx_ref, idx0, stride, dst, sem):
    for r in range(dst.shape[0]):
        _row_copy(src_hbm, idx_ref[idx0 + r * stride], dst, r, sem).start()


def _wait_rows(src_hbm, dst, sem):
    pltpu.make_async_copy(src_hbm.at[pl.ds(0, dst.shape[0]), :], dst, sem).wait()


def _expert_schedule(block_expert, counts, n_used):
    b = jnp.arange(block_expert.shape[0], dtype=I32)
    prev = jnp.concatenate([block_expert[:1] - 1, block_expert[:-1]])
    first = (b < n_used[0]) & (block_expert != prev)
    ordinal = jnp.cumsum(first.astype(I32)) - 1
    e = jnp.arange(N_EXPERTS, dtype=I32)
    later_live = (counts > 0)[None, :] & (e[None, :] > e[:, None])
    nxt = jnp.min(jnp.where(later_live, e[None, :], N_EXPERTS), axis=1)
    nxt = jnp.where(nxt == N_EXPERTS, -1, nxt).astype(I32)
    return ordinal, nxt[block_expert]


def _expert_weights(b, n_used, be_ref, ord_ref, nxt_ref, tiles, stages, sem, casts=()):
    def copies(e, slot):
        return [pltpu.make_async_copy(w.at[e, :, pl.ds(c0, width)], st.at[slot], sem.at[slot])
                for (w, c0, width), st in zip(tiles, stages)]

    e = be_ref[b]
    first = (b < n_used) & ((b == 0) | (e != be_ref[jnp.maximum(b - 1, 0)]))

    @pl.when(b == 0)
    def _():
        for c in copies(e, 0):
            c.start()

    @pl.when(first)
    def _():
        slot = ord_ref[b] % 2
        for c in copies(e, slot):
            c.wait()
        nxt = nxt_ref[b]

        @pl.when(nxt >= 0)
        def _():
            for c in copies(nxt, 1 - slot):
                c.start()

        for st, dst in zip(stages, casts):
            dst[...] = st[slot].astype(BF16)

    return ord_ref[b] % 2


def _swiglu_block(x, w_glu, w_lin, bg_ref, bl_ref, o_ref):
    h_glu = jnp.dot(x, w_glu.astype(BF16), preferred_element_type=F32) + bg_ref[0]
    h_lin = jnp.dot(x, w_lin.astype(BF16), preferred_element_type=F32) + bl_ref[0]
    h_glu = jnp.minimum(h_glu, SWIGLU_LIMIT)
    h_lin = jnp.clip(h_lin, -SWIGLU_LIMIT, SWIGLU_LIMIT)
    act = h_glu * _sigmoid(SWIGLU_ALPHA * h_glu) * (h_lin + 1.0)
    o_ref[...] = act.astype(o_ref.dtype)


def _expert_up_gather_kernel(be_ref, ord_ref, nxt_ref, nused_ref, tok_ref, x_hbm, w_hbm,
                             bg_ref, bl_ref, o_ref, rows_ref, stage_g, stage_l, wg_sc, wl_sc,
                             rows_a, rows_b, rows_c, wsem, rsem, *, tiles):
    ring = (rows_a, rows_b, rows_c)
    depth = len(ring) - 1
    b = pl.program_id(0)
    n_used = nused_ref[0]
    rb = rows_a.shape[0]

    def start(blk, slot):
        _start_rows(x_hbm, tok_ref, blk * rb, 1, ring[slot], rsem.at[slot])

    @pl.when(b == 0)
    def _():
        for s in range(depth):
            start(jnp.minimum(s, n_used - 1), s)

    _expert_weights(b, n_used, be_ref, ord_ref, nxt_ref, [(w_hbm,) + t for t in tiles],
                    [stage_g, stage_l], wsem, casts=[wg_sc, wl_sc])

    def block(slot):
        cur = ring[slot]
        _wait_rows(x_hbm, cur, rsem.at[slot])
        start(jnp.minimum(b + depth, n_used - 1), (slot + depth) % len(ring))
        x = cur[...].astype(BF16)
        rows_ref[...] = x
        _swiglu_block(x, wg_sc[...], wl_sc[...], bg_ref, bl_ref, o_ref)

    for slot in range(len(ring)):
        pl.when((b < n_used) & (b % len(ring) == slot))(functools.partial(block, slot))

    @pl.when(b >= n_used)
    def _():
        o_ref[...] = jnp.zeros(o_ref.shape, o_ref.dtype)
        rows_ref[...] = jnp.zeros(rows_ref.shape, rows_ref.dtype)

    @pl.when(b == pl.num_programs(0) - 1)
    def _():
        for s in range(depth):
            spare = (n_used + s) % len(ring)
            for slot in range(len(ring)):
                pl.when(spare == slot)(
                    functools.partial(_wait_rows, x_hbm, ring[slot], rsem.at[slot]))


def _expert_up_rows_kernel(be_ref, ord_ref, nxt_ref, nused_ref, rows_ref, w_hbm, bg_ref, bl_ref,
                           o_ref, stage_g, stage_l, wsem, *, tiles):
    b = pl.program_id(0)
    n_used = nused_ref[0]
    wslot = _expert_weights(b, n_used, be_ref, ord_ref, nxt_ref,
                            [(w_hbm,) + t for t in tiles], [stage_g, stage_l], wsem)

    @pl.when(b < n_used)
    def _():
        _swiglu_block(rows_ref[...], stage_g[wslot], stage_l[wslot], bg_ref, bl_ref, o_ref)

    @pl.when(b >= n_used)
    def _():
        o_ref[...] = jnp.zeros(o_ref.shape, o_ref.dtype)


UP_TILE = 1024


def _expert_up(x, row_tok, sched, w_up, b_up):
    be, ordinal, nxt, n_used = sched
    n_rows = row_tok.shape[0]
    tn = UP_TILE
    nf = EXPERT_FF // tn
    b_up3 = b_up.reshape(b_up.shape[0], 1, 2 * EXPERT_FF)
    weight_scratch = [pltpu.VMEM((2, D_MODEL, tn), F32), pltpu.VMEM((2, D_MODEL, tn), F32)]
    row_blk = lambda width: pl.BlockSpec((ROW_BLOCK, width), lambda b, *_: (b, 0))

    def bias(col_blk):
        return pl.BlockSpec((1, 1, tn), lambda b, be_, *_: (be_[b], 0, col_blk))

    def tiles(f):
        return ((f * tn, tn), ((nf + f) * tn, tn))

    act0, rows = pl.pallas_call(
        functools.partial(_expert_up_gather_kernel, tiles=tiles(0)),
        out_shape=[jax.ShapeDtypeStruct((n_rows, tn), BF16),
                   jax.ShapeDtypeStruct((n_rows, D_MODEL), BF16)],
        grid_spec=pltpu.PrefetchScalarGridSpec(
            num_scalar_prefetch=5,
            grid=(n_rows // ROW_BLOCK,),
            in_specs=[pl.BlockSpec(memory_space=pl.ANY), pl.BlockSpec(memory_space=pl.ANY),
                      bias(0), bias(nf)],
            out_specs=[row_blk(tn), row_blk(D_MODEL)],
            scratch_shapes=weight_scratch + [
                pltpu.VMEM((D_MODEL, tn), BF16), pltpu.VMEM((D_MODEL, tn), BF16),
                pltpu.VMEM((ROW_BLOCK, D_MODEL), F32), pltpu.VMEM((ROW_BLOCK, D_MODEL), F32),
                pltpu.VMEM((ROW_BLOCK, D_MODEL), F32),
                pltpu.SemaphoreType.DMA((2,)), pltpu.SemaphoreType.DMA((3,))],
        ),
        compiler_params=_params(("arbitrary",), 58),
        name="expert_up_gather",
    )(be, ordinal, nxt, n_used, row_tok, x, w_up, b_up3, b_up3)
    acts = [act0]
    for f in range(1, nf):
        acts.append(pl.pallas_call(
            functools.partial(_expert_up_rows_kernel, tiles=tiles(f)),
            out_shape=jax.ShapeDtypeStruct((n_rows, tn), BF16),
            grid_spec=pltpu.PrefetchScalarGridSpec(
                num_scalar_prefetch=4,
                grid=(n_rows // ROW_BLOCK,),
                in_specs=[row_blk(D_MODEL), pl.BlockSpec(memory_space=pl.ANY),
                          bias(f), bias(nf + f)],
                out_specs=row_blk(tn),
                scratch_shapes=weight_scratch + [pltpu.SemaphoreType.DMA((2,))],
            ),
            compiler_params=_params(("arbitrary",), 56),
            name="expert_up_rows",
        )(be, ordinal, nxt, n_used, rows, w_up, b_up3, b_up3))
    return acts


def _expert_down_kernel(be_ref, ord_ref, nxt_ref, nused_ref, *refs, n_act):
    act_refs = refs[:n_act]
    w_hbm, bias_ref, o_ref, stage, wsem = refs[n_act:]
    b = pl.program_id(0)
    n_used = nused_ref[0]
    slot = _expert_weights(b, n_used, be_ref, ord_ref, nxt_ref, [(w_hbm, 0, D_MODEL)], [stage],
                           wsem)

    @pl.when(b < n_used)
    def _():
        y = bias_ref[0]
        for f, act_ref in enumerate(act_refs):
            k0 = f * act_ref.shape[1]
            w = stage[slot, k0:k0 + act_ref.shape[1], :].astype(BF16)
            y = y + jnp.dot(act_ref[...], w, preferred_element_type=F32)
        o_ref[...] = y

    @pl.when(b >= n_used)
    def _():
        o_ref[...] = jnp.zeros(o_ref.shape, o_ref.dtype)


def _expert_down(acts, sched, w_down, b_down):
    be, ordinal, nxt, n_used = sched
    n_rows, tn = acts[0].shape
    b_down3 = b_down.reshape(b_down.shape[0], 1, D_MODEL)
    return pl.pallas_call(
        functools.partial(_expert_down_kernel, n_act=len(acts)),
        out_shape=jax.ShapeDtypeStruct((n_rows, D_MODEL), F32),
        grid_spec=pltpu.PrefetchScalarGridSpec(
            num_scalar_prefetch=4,
            grid=(n_rows // ROW_BLOCK,),
            in_specs=[pl.BlockSpec((ROW_BLOCK, tn), lambda b, *_: (b, 0)) for _ in acts] + [
                pl.BlockSpec(memory_space=pl.ANY),
                pl.BlockSpec((1, 1, D_MODEL), lambda b, be_, *_: (be_[b], 0, 0))],
            out_specs=pl.BlockSpec((ROW_BLOCK, D_MODEL), lambda b, *_: (b, 0)),
            scratch_shapes=[pltpu.VMEM((2, EXPERT_FF, D_MODEL), F32),
                            pltpu.SemaphoreType.DMA((2,))],
        ),
        compiler_params=_params(("arbitrary",), 56),
        name="expert_down",
    )(be, ordinal, nxt, n_used, *acts, w_down, b_down3)


def _combine_ln_kernel(pos_ref, y_hbm, gate_ref, x_ref, g_ref, b_ref, of_ref, ob_ref,
                       buf, sem):
    tc = x_ref.shape[0]
    i = pl.program_id(0)
    last = pl.num_programs(0) - 1
    cur = i % 2

    def start(tile, slot):
        for k in range(TOP_K):
            _start_rows(y_hbm, pos_ref, tile * (tc * TOP_K) + k, TOP_K, buf.at[slot, k],
                        sem.at[slot])

    def wait(slot):
        for k in range(TOP_K):
            _wait_rows(y_hbm, buf.at[slot, k], sem.at[slot])

    @pl.when(i == 0)
    def _():
        start(0, 0)

    wait(cur)
    start(jnp.where(i < last, i + 1, 0), 1 - cur)
    gate = gate_ref[...]
    ffn = gate[:, 0:1] * buf[cur, 0]
    for k in range(1, TOP_K):
        ffn = ffn + gate[:, k:k + 1] * buf[cur, k]
    y = _layer_norm(DEEPNORM_ALPHA * x_ref[...] + ffn, g_ref[...], b_ref[...])
    of_ref[...] = y
    ob_ref[...] = y.astype(BF16)

    @pl.when(i == last)
    def _():
        wait(1 - cur)


def _combine_ln(y_rows, pos, gate, x, g, b):
    t = x.shape[0]
    tc = min(128, t)
    row = lambda b_, p: (b_, 0)
    return pl.pallas_call(
        _combine_ln_kernel,
        out_shape=[jax.ShapeDtypeStruct((t, D_MODEL), F32),
                   jax.ShapeDtypeStruct((t, D_MODEL), BF16)],
        grid_spec=pltpu.PrefetchScalarGridSpec(
            num_scalar_prefetch=1,
            grid=(t // tc,),
            in_specs=[pl.BlockSpec(memory_space=pl.ANY),
                      pl.BlockSpec((tc, TOP_K), row),
                      pl.BlockSpec((tc, D_MODEL), row),
                      pl.BlockSpec((1, D_MODEL), lambda b_, p: (0, 0)),
                      pl.BlockSpec((1, D_MODEL), lambda b_, p: (0, 0))],
            out_specs=[pl.BlockSpec((tc, D_MODEL), row), pl.BlockSpec((tc, D_MODEL), row)],
            scratch_shapes=[pltpu.VMEM((2, TOP_K, tc, D_MODEL), F32),
                            pltpu.SemaphoreType.DMA((2,))],
        ),
        compiler_params=_params(("arbitrary",), 32),
        name="combine_ln",
    )(pos, y_rows, gate, x, g, b)


def _dispatch_plan(top_idx, rank, counts):
    t = top_idx.shape[0]
    n_assign = t * TOP_K
    n_blocks = -(-n_assign // ROW_BLOCK) + N_EXPERTS
    e_flat = top_idx.reshape(-1)
    padded = (counts + ROW_BLOCK - 1) // ROW_BLOCK * ROW_BLOCK
    pend = jnp.cumsum(padded)
    pstart = pend - padded
    dest = (pstart[e_flat] + rank.reshape(-1)).astype(I32)
    row_tok = jnp.zeros((n_blocks * ROW_BLOCK,), I32).at[dest].set(
        jnp.arange(n_assign, dtype=I32) // TOP_K)
    block_row0 = jnp.arange(n_blocks, dtype=I32) * ROW_BLOCK
    block_expert = jnp.minimum(jnp.sum((pend[None, :] <= block_row0[:, None]).astype(I32), axis=1),
                               N_EXPERTS - 1)
    n_used = (pend[-1:] // ROW_BLOCK).astype(I32)
    return dest, row_tok, block_expert, n_used


IN_FOX = N_BRANCHES * D_MODEL + 2 * RNN_WIDTH
IN_TAIL = IN_FOX + 3 * FOX_WIDTH


def _prep_in_tail(w_in, layer):
    tail = w_in[layer, :, IN_TAIL:]
    pad = LANES - MLA_ROPE_DIM - HEADS
    return jnp.concatenate([tail[:, HEADS:], tail[:, :HEADS], jnp.zeros((D_MODEL, pad), F32)],
                           axis=1).astype(BF16)


def _prep_uq(w_uq_l):
    w = w_uq_l.reshape(MLA_RANK, HEADS, MLA_QK_DIM) * (MLA_QK_DIM ** -0.5 * LOG2E)
    pad = jnp.zeros((MLA_RANK, HEADS, MLA_PAD_DIM - MLA_QK_DIM), F32)
    return jnp.concatenate([w, pad], axis=2).reshape(MLA_RANK, HEADS * MLA_PAD_DIM).astype(BF16)


def _layer(x, x_bf, tabs, p):
    t = x.shape[0]
    w_in, layer = p['w_in_all'], p['layer']
    w_in_t = jnp.swapaxes(w_in, 1, 2)
    u_a = _in_proj(x_bf, w_in_t, layer, 0, IN_FOX, jnp.ones((IN_FOX,), F32), F32,
                   "in_proj_gates_lru")
    q_scale = jnp.concatenate([jnp.full((FOX_WIDTH,), HEAD_DIM ** -0.5 * LOG2E, F32),
                               jnp.ones((2 * FOX_WIDTH,), F32)])
    u_b = _in_proj(x_bf, w_in_t, layer, IN_FOX, 3 * FOX_WIDTH, q_scale, BF16, "in_proj_fox")
    w_c = _prep_in_tail(w_in, layer)
    u_c = _matmul(x_bf, w_c, F32, 1024, w_c.shape[1], "in_proj_small")

    gate_blocks = N_BRANCHES * D_MODEL // RNN_WIDTH
    y_a = _rglru(u_a, gate_blocks, gate_blocks + 1,
                 p['conv_w'], p['conv_b'].reshape(1, RNN_WIDTH),
                 p['w_rec_gate'].astype(BF16), p['b_rec_gate'].reshape(1, RNN_WIDTH),
                 p['w_inp_gate'].astype(BF16), p['b_inp_gate'].reshape(1, RNN_WIDTH),
                 p['lru_lambda'].reshape(1, RNN_WIDTH))

    fl_col = 2 * MLA_RANK + MLA_ROPE_DIM
    cum = _fox_cum(u_c[:, fl_col:fl_col + HEADS], p['b_forget'])
    q_x, k_x = _fox_prep(u_b, cum)
    y_b = _causal_attention(q_x, k_x, u_b, 2 * FOX_WIDTH, "fox_attention")

    q_f, k_f, v_c = _mla_prep(u_c, p['g_cq'].reshape(1, MLA_RANK),
                              p['g_ckv'].reshape(1, MLA_RANK), _prep_uq(p['w_uq']),
                              p['w_ukv'].astype(BF16), tabs)
    y_c = _causal_attention(q_f, k_f, v_c, 0, "mla_attention")

    mixed = _merge(y_a, y_b, y_c, p['w_proj_lru_all'], p['w_proj_fox_all'], p['w_proj_mla_all'],
                   layer, u_a, p['b_merge'])
    x = _outproj_ln(mixed, p['w_out_all'], layer, x,
                    p['ln1_g'].reshape(1, D_MODEL), p['ln1_b'].reshape(1, D_MODEL))
    top_idx, gate, rank, counts = _router(x, p['w_router'].T, p['b_router'])
    dest, row_tok, block_expert, n_used = _dispatch_plan(top_idx, rank, counts)
    ordinal, nxt = _expert_schedule(block_expert, counts, n_used)
    base = p['expert_base']
    sched = (block_expert + base, ordinal, jnp.where(nxt >= 0, nxt + base, -1), n_used)
    acts = _expert_up(x, row_tok, sched, p['w_up_all'], p['b_up_all'])
    y_rows = _expert_down(acts, sched, p['w_down_all'], p['b_down_all'])
    del t
    return _combine_ln(y_rows, dest, gate, x, p['ln2_g'].reshape(1, D_MODEL),
                       p['ln2_b'].reshape(1, D_MODEL))


_LAYER_PARAMS = ('w_in', 'b_merge', 'b_forget', 'conv_w', 'conv_b', 'w_rec_gate', 'b_rec_gate',
                 'w_inp_gate', 'b_inp_gate', 'lru_lambda', 'g_cq', 'g_ckv', 'w_uq', 'w_ukv',
                 'w_proj_lru', 'w_proj_fox', 'w_proj_mla', 'w_out', 'ln1_g', 'ln1_b',
                 'w_router', 'b_router', 'w_up', 'b_up', 'w_down', 'b_down', 'ln2_g', 'ln2_b')


def kernel(x, positions, w_in, b_merge, b_forget, conv_w, conv_b, w_rec_gate, b_rec_gate,
           w_inp_gate, b_inp_gate, lru_lambda, g_cq, g_ckv, w_uq, w_ukv, w_proj_lru,
           w_proj_fox, w_proj_mla, w_out, ln1_g, ln1_b, w_router, b_router, w_up, b_up,
           w_down, b_down, ln2_g, ln2_b):
    stacked = dict(zip(_LAYER_PARAMS, (
        w_in, b_merge, b_forget, conv_w, conv_b, w_rec_gate, b_rec_gate, w_inp_gate,
        b_inp_gate, lru_lambda, g_cq, g_ckv, w_uq, w_ukv, w_proj_lru, w_proj_fox,
        w_proj_mla, w_out, ln1_g, ln1_b, w_router, b_router, w_up, b_up, w_down, b_down,
        ln2_g, ln2_b)))
    batch, seq, _ = x.shape
    assert batch == 1
    xt = x.reshape(seq, D_MODEL)
    x_bf = xt.astype(BF16)
    tabs = _rope_tables(positions.reshape(seq))
    expert_stack = {'w_up', 'b_up', 'w_down', 'b_down'}
    shared = {k + '_all': stacked[k].reshape((-1,) + stacked[k].shape[2:]) for k in expert_stack}
    dense_stack = {'w_in', 'w_proj_lru', 'w_proj_fox', 'w_proj_mla', 'w_out'}
    shared.update({k + '_all': stacked[k] for k in dense_stack})
    in_place = expert_stack | dense_stack
    for l in range(w_in.shape[0]):
        p = {k: v[l] for k, v in stacked.items() if k not in in_place}
        xt, x_bf = _layer(xt, x_bf, tabs,
                          dict(p, layer=l, expert_base=l * N_EXPERTS, **shared))
    return xt.reshape(batch, seq, D_MODEL)
```

```python
import functools

import jax
import jax.numpy as jnp
import numpy as np
from jax import lax
from jax.experimental import pallas as pl
from jax.experimental.pallas import tpu as pltpu

F32 = jnp.float32
BF16 = jnp.bfloat16
I32 = jnp.int32

D_MODEL = 2048
N_BRANCHES = 3
DEEPNORM_ALPHA = (2 * 2) ** 0.25
LN_EPS = 1e-5
RMS_EPS = 1e-6
RNN_WIDTH = 1024
RNN_BLOCKS = 8
RNN_BLOCK_W = RNN_WIDTH // RNN_BLOCKS
CONV_WIDTH = 4
RG_LRU_C = 8.0
HEADS = 8
HEAD_DIM = 128
FOX_WIDTH = HEADS * HEAD_DIM
MLA_RANK = 512
MLA_ROPE_DIM = 64
MLA_QK_DIM = HEAD_DIM + MLA_ROPE_DIM
ATTN_QK_DIM = 256
MLA_PAD_DIM = ATTN_QK_DIM
LOG2E = float(np.log2(np.e))
ROPE_THETA = 10000.0
N_EXPERTS = 32
TOP_K = 4
EXPERT_FF = D_MODEL
SWIGLU_ALPHA = 1.702
SWIGLU_LIMIT = 7.0

LANES = 128
SUBLANES = 8
V7X_VMEM_BYTES = 64 * 1024 * 1024

ROW_BLOCK = 256
ATTN_BLOCK = 512
GATHER_UNROLL = 8
ATTN_GROUP = 8
MASK_VALUE = -1e30


def _params(semantics, vmem_mib):
    assert vmem_mib * 1024 * 1024 < V7X_VMEM_BYTES
    return pltpu.CompilerParams(dimension_semantics=semantics,
                                vmem_limit_bytes=vmem_mib * 1024 * 1024)


def _split3(x):
    hi = x.astype(BF16)
    r = x - hi.astype(F32)
    mid = r.astype(BF16)
    lo = (r - mid.astype(F32)).astype(BF16)
    return hi, mid, lo


def _sigmoid(x):
    return 1.0 / (1.0 + jnp.exp(-x))


def _softplus(z):
    return jnp.maximum(z, 0.0) + jnp.log1p(jnp.exp(-jnp.abs(z)))


def _layer_norm(v, g, b):
    mu = jnp.mean(v, axis=-1, keepdims=True)
    c = v - mu
    var = jnp.mean(c * c, axis=-1, keepdims=True)
    return c * lax.rsqrt(var + LN_EPS) * g + b


def _mm_kernel(x_ref, w_ref, o_ref):
    o_ref[...] = jnp.dot(x_ref[...], w_ref[...],
                         preferred_element_type=F32).astype(o_ref.dtype)


def _matmul(x, w, out_dtype, tm, tn, name):
    m, k = x.shape
    n = w.shape[1]
    tm, tn = min(tm, m), min(tn, n)
    return pl.pallas_call(
        _mm_kernel,
        out_shape=jax.ShapeDtypeStruct((m, n), out_dtype),
        grid=(m // tm, n // tn),
        in_specs=[pl.BlockSpec((tm, k), lambda i, j: (i, 0)),
                  pl.BlockSpec((k, tn), lambda i, j: (0, j))],
        out_specs=pl.BlockSpec((tm, tn), lambda i, j: (i, j)),
        compiler_params=_params(("parallel", "arbitrary"), 48),
        name=name,
    )(x, w)


def _in_proj_kernel(x_ref, w_ref, s_ref, o_ref, w_sc):
    @pl.when(pl.program_id(1) == 0)
    def _():
        w_sc[...] = (w_ref[0] * s_ref[...]).T.astype(BF16)

    o_ref[...] = jnp.dot(x_ref[...], w_sc[...],
                         preferred_element_type=F32).astype(o_ref.dtype)


def _in_proj(x, w_stack_t, layer, col0, ncols, col_scale, out_dtype, name):
    m, k = x.shape
    tm, tn = min(1024, m), 1024
    assert col0 % tn == 0 and ncols % tn == 0
    blk0 = col0 // tn
    return pl.pallas_call(
        _in_proj_kernel,
        out_shape=jax.ShapeDtypeStruct((m, ncols), out_dtype),
        grid=(ncols // tn, m // tm),
        in_specs=[pl.BlockSpec((tm, k), lambda j, i: (i, 0)),
                  pl.BlockSpec((1, tn, k), lambda j, i: (layer, blk0 + j, 0)),
                  pl.BlockSpec((tn, 1), lambda j, i: (j, 0))],
        out_specs=pl.BlockSpec((tm, tn), lambda j, i: (i, j)),
        scratch_shapes=[pltpu.VMEM((k, tn), BF16)],
        compiler_params=_params(("arbitrary", "arbitrary"), 48),
        name=name,
    )(x, w_stack_t, col_scale.reshape(ncols, 1))


def _fox_cum_kernel(fl_ref, b_ref, o_ref, *, chunks):
    z = fl_ref[...] + b_ref[...]
    lf = jnp.minimum(z, 0.0) - jnp.log1p(jnp.exp(-jnp.abs(z)))
    rows = lf.shape[0]
    s = lax.broadcasted_iota(I32, (LANES, LANES), 0)
    t = lax.broadcasted_iota(I32, (LANES, LANES), 1)
    tri = (s <= t).astype(BF16)
    incl = None
    for part in _split3(lf):
        d = jnp.dot(part, tri, preferred_element_type=F32)
        incl = d if incl is None else incl + d
    i = lax.broadcasted_iota(I32, (rows, rows), 0)
    j = lax.broadcasted_iota(I32, (rows, rows), 1)
    lower = ((i // chunks == j // chunks) & (j < i)).astype(BF16)
    offs = None
    for part in _split3(incl):
        d = jnp.dot(lower, part, preferred_element_type=F32)
        offs = d if offs is None else offs + d
    o_ref[...] = incl + offs[:, LANES - 1:LANES]


def _fox_cum(f_logit, b_f):
    t = f_logit.shape[0]
    chunks = t // LANES
    fl = f_logit.T.reshape(HEADS * chunks, LANES)
    b = jnp.repeat(b_f.astype(F32), chunks).reshape(HEADS * chunks, 1)
    out = pl.pallas_call(
        functools.partial(_fox_cum_kernel, chunks=chunks),
        out_shape=jax.ShapeDtypeStruct((HEADS * chunks, LANES), F32),
        name="fox_cum",
    )(fl, b)
    return out.reshape(HEADS, t)


def _attn_kernel(qi_ref, ki_ref, q_ref, k_ref, v_ref, o_ref, m_sc, l_sc, acc_sc, *, group):
    p = pl.program_id(1)
    qi = qi_ref[p]
    ki = ki_ref[p]
    blk = q_ref.shape[0]

    @pl.when(ki == 0)
    def _():
        m_sc[...] = jnp.full(m_sc.shape, MASK_VALUE, F32)
        l_sc[...] = jnp.zeros(l_sc.shape, F32)
        acc_sc[...] = jnp.zeros(acc_sc.shape, F32)

    def step(masked):
        for g in range(group):
            qk = slice(g * ATTN_QK_DIM, (g + 1) * ATTN_QK_DIM)
            vo = slice(g * HEAD_DIM, (g + 1) * HEAD_DIM)
            s = lax.dot_general(q_ref[:, qk], k_ref[:, qk], (((1,), (1,)), ((), ())),
                                preferred_element_type=F32)
            if masked:
                row = lax.broadcasted_iota(I32, s.shape, 0)
                col = lax.broadcasted_iota(I32, s.shape, 1)
                s = jnp.where(col <= row, s, MASK_VALUE)
            chunks = [s[:, c * LANES:(c + 1) * LANES] for c in range(blk // LANES)]
            cmax = functools.reduce(jnp.maximum, chunks)
            m_prev = m_sc[g]
            m_new = jnp.maximum(m_prev, jnp.max(cmax, axis=1, keepdims=True))
            alpha = jnp.exp2(m_prev - m_new)
            probs = [jnp.exp2(c - m_new) for c in chunks]
            l_sc[g] = alpha * l_sc[g] + functools.reduce(jnp.add, probs)
            pb = jnp.concatenate(probs, axis=1).astype(BF16)
            acc_sc[g] = alpha * acc_sc[g] + jnp.dot(pb, v_ref[:, vo],
                                                    preferred_element_type=F32)
            m_sc[g] = m_new

    @pl.when(ki < qi)
    def _():
        step(False)

    @pl.when(ki == qi)
    def _():
        step(True)
        for g in range(group):
            denom = jnp.sum(l_sc[g], axis=1, keepdims=True)
            o_ref[:, g * HEAD_DIM:(g + 1) * HEAD_DIM] = (acc_sc[g] / denom).astype(o_ref.dtype)


def _causal_attention(q, k, v, v_col, name):
    t = q.shape[0]
    blk = min(ATTN_BLOCK, t)
    nq = t // blk
    group = ATTN_GROUP
    pairs = [(a, b) for a in range(nq) for b in range(a + 1)]
    qi_tab = jnp.asarray(np.array([a for a, _ in pairs], np.int32))
    ki_tab = jnp.asarray(np.array([b for _, b in pairs], np.int32))
    v_blk = v_col // (group * HEAD_DIM)
    return pl.pallas_call(
        functools.partial(_attn_kernel, group=group),
        out_shape=jax.ShapeDtypeStruct((t, HEADS * HEAD_DIM), BF16),
        grid_spec=pltpu.PrefetchScalarGridSpec(
            num_scalar_prefetch=2,
            grid=(HEADS // group, len(pairs)),
            in_specs=[
                pl.BlockSpec((blk, group * ATTN_QK_DIM), lambda h, p, qi, ki: (qi[p], h)),
                pl.BlockSpec((blk, group * ATTN_QK_DIM), lambda h, p, qi, ki: (ki[p], h)),
                pl.BlockSpec((blk, group * HEAD_DIM), lambda h, p, qi, ki: (ki[p], v_blk + h)),
            ],
            out_specs=pl.BlockSpec((blk, group * HEAD_DIM), lambda h, p, qi, ki: (qi[p], h)),
            scratch_shapes=[pltpu.VMEM((group, blk, LANES), F32),
                            pltpu.VMEM((group, blk, LANES), F32),
                            pltpu.VMEM((group, blk, HEAD_DIM), F32)],
        ),
        compiler_params=_params(("parallel", "arbitrary"), 32),
        name=name,
    )(qi_tab, ki_tab, q, k, v)


def _fox_prep_kernel(q_ref, k_ref, c_ref, sel_ref, qo_ref, ko_ref):
    bias = None
    for j, part in enumerate(_split3(c_ref[...])):
        d = jnp.dot(part, sel_ref[j], preferred_element_type=F32)
        bias = d if bias is None else bias + d
    lane = lax.broadcasted_iota(I32, (q_ref.shape[0], LANES), 1)
    for h in range(HEADS):
        src = slice(h * HEAD_DIM, (h + 1) * HEAD_DIM)
        lo = h * ATTN_QK_DIM
        g = bias[:, src]
        qo_ref[:, lo:lo + HEAD_DIM] = q_ref[:, src]
        ko_ref[:, lo:lo + HEAD_DIM] = k_ref[:, src]
        qo_ref[:, lo + HEAD_DIM:lo + ATTN_QK_DIM] = jnp.where(
            (lane >= 3) & (lane < 6), 1.0, g).astype(BF16)
        ko_ref[:, lo + HEAD_DIM:lo + ATTN_QK_DIM] = jnp.where(
            lane < 3, 1.0, -pltpu.roll(g, 3, axis=1)).astype(BF16)


def _fox_prep(u_b, cum):
    t = u_b.shape[0]
    tm = min(512, t)
    c = jnp.pad(cum.T * LOG2E, ((0, 0), (0, LANES - HEADS)))
    sel = np.zeros((3, LANES, FOX_WIDTH), np.float32)
    for j in range(3):
        for h in range(HEADS):
            sel[j, h, h * HEAD_DIM + j] = 1.0
    wide = HEADS * ATTN_QK_DIM
    return pl.pallas_call(
        _fox_prep_kernel,
        out_shape=[jax.ShapeDtypeStruct((t, wide), BF16)] * 2,
        grid=(t // tm,),
        in_specs=[pl.BlockSpec((tm, FOX_WIDTH), lambda i: (i, 0)),
                  pl.BlockSpec((tm, FOX_WIDTH), lambda i: (i, 1)),
                  pl.BlockSpec((tm, LANES), lambda i: (i, 0)),
                  pl.BlockSpec((3, LANES, FOX_WIDTH), lambda i: (0, 0, 0))],
        out_specs=[pl.BlockSpec((tm, wide), lambda i: (i, 0))] * 2,
        compiler_params=_params(("arbitrary",), 32),
        name="fox_prep",
    )(u_b, u_b, c, jnp.asarray(sel, BF16))


def _rope_table_kernel(pos_ref, inv_ref, c_ref, s1_ref, s2_ref):
    half = MLA_ROPE_DIM // 2
    ang = pos_ref[...].astype(F32) * inv_ref[...]
    c = jnp.cos(ang)
    s = jnp.sin(ang)
    lane = lax.broadcasted_iota(I32, ang.shape, 1)
    c_ref[...] = jnp.where(lane < MLA_ROPE_DIM, c, 0.0)
    s1_ref[...] = jnp.where(lane < half, -s, 0.0)
    s2_ref[...] = jnp.where((lane >= half) & (lane < MLA_ROPE_DIM), s, 0.0)


def _rope_tables(positions):
    t = positions.shape[0]
    half = MLA_ROPE_DIM // 2
    inv_freq = ROPE_THETA ** (-jnp.arange(half, dtype=F32) / half)
    inv = jnp.concatenate([inv_freq, inv_freq, jnp.zeros((LANES - 2 * half,), F32)])
    tm = min(1024, t)
    spec = pl.BlockSpec((tm, LANES), lambda i: (i, 0))
    return pl.pallas_call(
        _rope_table_kernel,
        out_shape=[jax.ShapeDtypeStruct((t, LANES), F32)] * 3,
        grid=(t // tm,),
        in_specs=[pl.BlockSpec((tm, 1), lambda i: (i, 0)),
                  pl.BlockSpec((1, LANES), lambda i: (0, 0))],
        out_specs=[spec, spec, spec],
        name="rope_tables",
    )(positions.reshape(t, 1), inv.reshape(1, LANES))


def _rope_group(g, c, s1, s2):
    half = MLA_ROPE_DIM // 2
    return (g * c + pltpu.roll(g, LANES - half, axis=1) * s1
            + pltpu.roll(g, half, axis=1) * s2)


def _mla_prep_kernel(u_ref, gq_ref, gkv_ref, wq_ref, wkv_ref, c_ref, s1_ref, s2_ref,
                     q_ref, k_ref, v_ref):
    def rms(v, g):
        ms = jnp.mean(v * v, axis=-1, keepdims=True)
        return (v * lax.rsqrt(ms + RMS_EPS) * g).astype(BF16)

    c, s1, s2 = c_ref[...], s1_ref[...], s2_ref[...]
    cq = rms(u_ref[:, 0:MLA_RANK], gq_ref[...])
    ckv = rms(u_ref[:, MLA_RANK:2 * MLA_RANK], gkv_ref[...])
    q_pre = jnp.dot(cq, wq_ref[...], preferred_element_type=F32)
    kv = jnp.dot(ckv, wkv_ref[...], preferred_element_type=F32)
    k_rot = _rope_group(u_ref[:, 2 * MLA_RANK:2 * MLA_RANK + LANES], c, s1, s2).astype(BF16)
    for h in range(HEADS):
        lo = h * MLA_PAD_DIM
        mid = lo + HEAD_DIM
        hi = lo + MLA_PAD_DIM
        q_ref[:, lo:mid] = q_pre[:, lo:mid].astype(BF16)
        q_ref[:, mid:hi] = _rope_group(q_pre[:, mid:hi], c, s1, s2).astype(BF16)
        k_ref[:, lo:mid] = kv[:, lo:mid].astype(BF16)
        k_ref[:, mid:hi] = k_rot
        v_ref[:, h * HEAD_DIM:(h + 1) * HEAD_DIM] = kv[:, mid:hi].astype(BF16)


def _mla_prep(u_small, g_cq, g_ckv, w_uq_r, w_ukv, tabs):
    t = u_small.shape[0]
    tm = min(512, t)
    wide = HEADS * MLA_PAD_DIM
    row = lambda w: pl.BlockSpec((tm, w), lambda i: (i, 0))
    const = lambda a: pl.BlockSpec(a.shape, lambda i: (0, 0))
    return pl.pallas_call(
        _mla_prep_kernel,
        out_shape=[jax.ShapeDtypeStruct((t, wide), BF16),
                   jax.ShapeDtypeStruct((t, wide), BF16),
                   jax.ShapeDtypeStruct((t, HEADS * HEAD_DIM), BF16)],
        grid=(t // tm,),
        in_specs=[row(u_small.shape[1]), const(g_cq), const(g_ckv), const(w_uq_r),
                  const(w_ukv), row(LANES), row(LANES), row(LANES)],
        out_specs=[row(wide), row(wide), row(HEADS * HEAD_DIM)],
        compiler_params=_params(("arbitrary",), 48),
        name="mla_prep",
    )(u_small, g_cq, g_ckv, w_uq_r, w_ukv, *tabs)


def _rglru_kernel(y_ref, x_ref, cw_ref, cb_ref, wr_ref, br_ref, wi_ref, bi_ref, lam_ref,
                  o_ref, xprev_sc, h_sc, a_sc, g_sc, hs_sc):
    tt = x_ref.shape[0]

    @pl.when(pl.program_id(0) == 0)
    def _():
        xprev_sc[...] = jnp.zeros(xprev_sc.shape, F32)
        h_sc[...] = jnp.zeros(h_sc.shape, F32)

    x = x_ref[...]
    xext = jnp.concatenate([xprev_sc[...], x], axis=0)
    xprev_sc[...] = x[tt - SUBLANES:, :]
    xc = cb_ref[...]
    for j in range(CONV_WIDTH):
        off = SUBLANES - (CONV_WIDTH - 1) + j
        xc = xc + cw_ref[j:j + 1, :] * xext[off:off + tt, :]

    r_parts, i_parts = [], []
    for n in range(RNN_BLOCKS):
        sl = slice(n * RNN_BLOCK_W, (n + 1) * RNN_BLOCK_W)
        xb = xc[:, sl].astype(BF16)
        r_parts.append(_sigmoid(jnp.dot(xb, wr_ref[n], preferred_element_type=F32)
                                + br_ref[:, sl]))
        i_parts.append(_sigmoid(jnp.dot(xb, wi_ref[n], preferred_element_type=F32)
                                + bi_ref[:, sl]))
    r = jnp.concatenate(r_parts, axis=1)
    gate_i = jnp.concatenate(i_parts, axis=1)
    log_a = (-RG_LRU_C * _softplus(-lam_ref[...])) * r
    a = jnp.exp(log_a)
    a_sc[...] = a
    g_sc[...] = jnp.sqrt(-jnp.tanh(log_a) * (a * a + 1.0)) * (gate_i * xc)

    sub = lax.broadcasted_iota(I32, (SUBLANES, RNN_WIDTH), 0)

    def tile_scan(k, h_in):
        rows = pl.ds(pl.multiple_of(k * SUBLANES, SUBLANES), SUBLANES)
        a = a_sc[rows, :]
        g = g_sc[rows, :]
        for d in (1, 2, 4):
            keep = sub >= d
            g = jnp.where(keep, a * pltpu.roll(g, d, axis=0) + g, g)
            a = jnp.where(keep, a * pltpu.roll(a, d, axis=0), a)
        hs = a * h_in + g
        hs_sc[rows, :] = hs
        return jnp.broadcast_to(hs[SUBLANES - 1:SUBLANES, :], (SUBLANES, RNN_WIDTH))

    h_sc[...] = lax.fori_loop(0, tt // SUBLANES, tile_scan, h_sc[...])

    y = y_ref[...]
    gelu = 0.5 * y * (1.0 + jnp.tanh(np.sqrt(2.0 / np.pi) * (y + 0.044715 * (y * y * y))))
    o_ref[...] = (hs_sc[...] * gelu).astype(o_ref.dtype)


def _rglru(u_a, y_blk, x_blk, conv_w, conv_b, w_rg, b_rg, w_ig, b_ig, lam):
    t = u_a.shape[0]
    tt = min(256, t)
    const2 = lambda a: pl.BlockSpec(a.shape, lambda i: (0, 0))
    const3 = lambda a: pl.BlockSpec(a.shape, lambda i: (0, 0, 0))
    return pl.pallas_call(
        _rglru_kernel,
        out_shape=jax.ShapeDtypeStruct((t, RNN_WIDTH), BF16),
        grid=(t // tt,),
        in_specs=[pl.BlockSpec((tt, RNN_WIDTH), lambda i: (i, y_blk)),
                  pl.BlockSpec((tt, RNN_WIDTH), lambda i: (i, x_blk)),
                  const2(conv_w), const2(conv_b), const3(w_rg), const2(b_rg),
                  const3(w_ig), const2(b_ig), const2(lam)],
        out_specs=pl.BlockSpec((tt, RNN_WIDTH), lambda i: (i, 0)),
        scratch_shapes=[pltpu.VMEM((SUBLANES, RNN_WIDTH), F32),
                        pltpu.VMEM((SUBLANES, RNN_WIDTH), F32),
                        pltpu.VMEM((tt, RNN_WIDTH), F32),
                        pltpu.VMEM((tt, RNN_WIDTH), F32),
                        pltpu.VMEM((tt, RNN_WIDTH), F32)],
        compiler_params=_params(("arbitrary",), 32),
        name="rglru",
    )(u_a, u_a, conv_w, conv_b, w_rg, b_rg, w_ig, b_ig, lam)


def _merge_kernel(a_ref, b_ref, c_ref, wa_ref, wb_ref, wc_ref, g0_ref, g1_ref, g2_ref,
                  bm_ref, o_ref):
    def branch(x_ref, w_ref, g_ref, n):
        y = jnp.dot(x_ref[...], w_ref[0].astype(BF16), preferred_element_type=F32)
        return _sigmoid(g_ref[...] + bm_ref[n:n + 1, :]) * y

    mixed = (branch(a_ref, wa_ref, g0_ref, 0) + branch(b_ref, wb_ref, g1_ref, 1)
             + branch(c_ref, wc_ref, g2_ref, 2))
    o_ref[...] = mixed.astype(o_ref.dtype)


def _merge(ya, yb, yc, wa, wb, wc, layer, u_a, b_merge):
    t = ya.shape[0]
    tm, tn = min(512, t), 1024
    nn = D_MODEL // tn
    xin = pl.BlockSpec((tm, ya.shape[1]), lambda j, i: (i, 0))
    win = pl.BlockSpec((1, ya.shape[1], tn), lambda j, i: (layer, 0, j))
    gate = lambda n: pl.BlockSpec((tm, tn), lambda j, i: (i, n * nn + j))
    return pl.pallas_call(
        _merge_kernel,
        out_shape=jax.ShapeDtypeStruct((t, D_MODEL), BF16),
        grid=(nn, t // tm),
        in_specs=[xin, xin, xin, win, win, win, gate(0), gate(1), gate(2),
                  pl.BlockSpec((N_BRANCHES, tn), lambda j, i: (0, j))],
        out_specs=pl.BlockSpec((tm, tn), lambda j, i: (i, j)),
        compiler_params=_params(("arbitrary", "arbitrary"), 56),
        name="merge",
    )(ya, yb, yc, wa, wb, wc, u_a, u_a, u_a, b_merge)


def _outproj_ln_kernel(m_ref, w_ref, x_ref, g_ref, b_ref, o_ref):
    mix = jnp.dot(m_ref[...], w_ref[0].astype(BF16), preferred_element_type=F32)
    o_ref[...] = _layer_norm(DEEPNORM_ALPHA * x_ref[...] + mix, g_ref[...], b_ref[...])


def _outproj_ln(mixed, w_out, layer, x, g, b):
    t = x.shape[0]
    tm = min(256, t)
    row = pl.BlockSpec((tm, D_MODEL), lambda i: (i, 0))
    vec = pl.BlockSpec((1, D_MODEL), lambda i: (0, 0))
    return pl.pallas_call(
        _outproj_ln_kernel,
        out_shape=jax.ShapeDtypeStruct((t, D_MODEL), F32),
        grid=(t // tm,),
        in_specs=[row, pl.BlockSpec((1, D_MODEL, D_MODEL), lambda i: (layer, 0, 0)), row, vec,
                  vec],
        out_specs=row,
        compiler_params=_params(("arbitrary",), 56),
        name="outproj_ln",
    )(mixed, w_out, x, g, b)


def _route(x, w_ref, b_ref, idx_ref, gate_ref, rank_ref, count_ref, count_sc):
    nt = (((1,), (1,)), ((), ()))
    x_hi = x.astype(BF16)
    x_lo = (x - x_hi.astype(F32)).astype(BF16)
    w = w_ref[...]
    w_hi = w.astype(BF16)
    w_lo = (w - w_hi.astype(F32)).astype(BF16)
    logits = (lax.dot_general(w_hi, x_hi, nt, preferred_element_type=F32)
              + lax.dot_general(w_hi, x_lo, nt, preferred_element_type=F32)
              + lax.dot_general(w_lo, x_hi, nt, preferred_element_type=F32)
              + b_ref[...])
    eidx = lax.broadcasted_iota(I32, logits.shape, 0)
    vals, idxs = [], []
    for _ in range(TOP_K):
        m = jnp.max(logits, axis=0, keepdims=True)
        idx = jnp.min(jnp.where(logits == m, eidx, N_EXPERTS), axis=0, keepdims=True)
        vals.append(m)
        idxs.append(idx)
        logits = jnp.where(eidx == idx, -jnp.inf, logits)
    exps = [jnp.exp(v - vals[0]) for v in vals]
    denom = exps[0] + exps[1] + exps[2] + exps[3]
    pad = SUBLANES - TOP_K
    tokens = logits.shape[1]
    idx_ref[...] = jnp.concatenate(idxs + [jnp.zeros((pad, tokens), I32)], axis=0)
    gate_ref[...] = jnp.concatenate([e / denom for e in exps]
                                    + [jnp.zeros((pad, tokens), F32)], axis=0)

    @pl.when(pl.program_id(0) == 0)
    def _():
        count_sc[...] = jnp.zeros(count_sc.shape, F32)

    src = lax.broadcasted_iota(I32, (tokens, tokens), 0)
    dst = lax.broadcasted_iota(I32, (tokens, tokens), 1)
    before = (src < dst).astype(BF16)
    seen = count_sc[...]
    ranks = []
    for idx in idxs:
        hit = eidx == idx
        prefix = jnp.dot(hit.astype(BF16), before, preferred_element_type=F32)
        ranks.append(jnp.sum(jnp.where(hit, seen + prefix, 0.0), axis=0, keepdims=True))
        seen = seen + jnp.sum(hit.astype(F32), axis=1, keepdims=True)
    count_sc[...] = seen
    rank_ref[...] = jnp.concatenate(ranks + [jnp.zeros((pad, tokens), F32)],
                                    axis=0).astype(I32)
    count_ref[...] = seen.astype(I32)


def _router_kernel(x_ref, w_ref, b_ref, idx_ref, gate_ref, rank_ref, count_ref, count_sc):
    _route(x_ref[...], w_ref, b_ref, idx_ref, gate_ref, rank_ref, count_ref, count_sc)


def _router(x, w_router_t, b_router):
    t = x.shape[0]
    tm = min(1024, t)
    tok = pl.BlockSpec((SUBLANES, tm), lambda i: (0, i))
    idx, gate, rank, count = pl.pallas_call(
        _router_kernel,
        out_shape=[jax.ShapeDtypeStruct((SUBLANES, t), I32),
                   jax.ShapeDtypeStruct((SUBLANES, t), F32),
                   jax.ShapeDtypeStruct((SUBLANES, t), I32),
                   jax.ShapeDtypeStruct((N_EXPERTS, 1), I32)],
        grid=(t // tm,),
        in_specs=[pl.BlockSpec((tm, D_MODEL), lambda i: (i, 0)),
                  pl.BlockSpec((N_EXPERTS, D_MODEL), lambda i: (0, 0)),
                  pl.BlockSpec((N_EXPERTS, 1), lambda i: (0, 0))],
        out_specs=[tok, tok, tok, pl.BlockSpec((N_EXPERTS, 1), lambda i: (0, 0))],
        scratch_shapes=[pltpu.VMEM((N_EXPERTS, 1), F32)],
        compiler_params=_params(("arbitrary",), 48),
        name="router",
    )(x, w_router_t, b_router.reshape(N_EXPERTS, 1))
    return idx[:TOP_K].T, gate[:TOP_K].T, rank[:TOP_K].T, count[:, 0]


def _row_copy(src_hbm, row, dst, dst_row, sem):
    return pltpu.make_async_copy(src_hbm.at[pl.ds(row, 1), :],
                                 dst.at[pl.ds(dst_row, 1), :], sem)


def _start_rows(src_hbm, idx_ref, idx0, stride, dst, sem):
    for r in range(dst.shape[0]):
        _row_copy(src_hbm, idx_ref[idx0 + r * stride], dst, r, sem).start()


def _wait_rows(src_hbm, dst, sem):
    pltpu.make_async_copy(src_hbm.at[pl.ds(0, dst.shape[0]), :], dst, sem).wait()


def _expert_schedule(block_expert, counts, n_used):
    b = jnp.arange(block_expert.shape[0], dtype=I32)
    prev = jnp.concatenate([block_expert[:1] - 1, block_expert[:-1]])
    first = (b < n_used[0]) & (block_expert != prev)
    ordinal = jnp.cumsum(first.astype(I32)) - 1
    e = jnp.arange(N_EXPERTS, dtype=I32)
    later_live = (counts > 0)[None, :] & (e[None, :] > e[:, None])
    nxt = jnp.min(jnp.where(later_live, e[None, :], N_EXPERTS), axis=1)
    nxt = jnp.where(nxt == N_EXPERTS, -1, nxt).astype(I32)
    return ordinal, nxt[block_expert]


def _expert_weights(b, n_used, be_ref, ord_ref, nxt_ref, tiles, stages, sem, casts=()):
    def copies(e, slot):
        return [pltpu.make_async_copy(w.at[e, :, pl.ds(c0, width)], st.at[slot], sem.at[slot])
                for (w, c0, width), st in zip(tiles, stages)]

    e = be_ref[b]
    first = (b < n_used) & ((b == 0) | (e != be_ref[jnp.maximum(b - 1, 0)]))

    @pl.when(b == 0)
    def _():
        for c in copies(e, 0):
            c.start()

    @pl.when(first)
    def _():
        slot = ord_ref[b] % 2
        for c in copies(e, slot):
            c.wait()
        nxt = nxt_ref[b]

        @pl.when(nxt >= 0)
        def _():
            for c in copies(nxt, 1 - slot):
                c.start()

        for st, dst in zip(stages, casts):
            dst[...] = st[slot].astype(BF16)

    return ord_ref[b] % 2


def _swiglu_block(x, w_glu, w_lin, bg_ref, bl_ref, o_ref):
    h_glu = jnp.dot(x, w_glu.astype(BF16), preferred_element_type=F32) + bg_ref[0]
    h_lin = jnp.dot(x, w_lin.astype(BF16), preferred_element_type=F32) + bl_ref[0]
    h_glu = jnp.minimum(h_glu, SWIGLU_LIMIT)
    h_lin = jnp.clip(h_lin, -SWIGLU_LIMIT, SWIGLU_LIMIT)
    act = h_glu * _sigmoid(SWIGLU_ALPHA * h_glu) * (h_lin + 1.0)
    o_ref[...] = act.astype(o_ref.dtype)


def _expert_up_gather_kernel(be_ref, ord_ref, nxt_ref, nused_ref, row0_ref, tok_ref, x_hbm, w_hbm,
                             bg_ref, bl_ref, o_ref, rows_ref, stage_g, stage_l, wg_sc, wl_sc,
                             rows_a, rows_b, rows_c, wsem, rsem, *, tiles):
    ring = (rows_a, rows_b, rows_c)
    depth = len(ring) - 1
    b = pl.program_id(0)
    n_used = nused_ref[0]

    def start(blk, slot):
        _start_rows(x_hbm, tok_ref, row0_ref[blk], 1, ring[slot], rsem.at[slot])

    @pl.when(b == 0)
    def _():
        for s in range(depth):
            start(jnp.minimum(s, n_used - 1), s)

    _expert_weights(b, n_used, be_ref, ord_ref, nxt_ref, [(w_hbm,) + t for t in tiles],
                    [stage_g, stage_l], wsem, casts=[wg_sc, wl_sc])

    def block(slot):
        cur = ring[slot]
        _wait_rows(x_hbm, cur, rsem.at[slot])
        start(jnp.minimum(b + depth, n_used - 1), (slot + depth) % len(ring))
        x = cur[...].astype(BF16)
        rows_ref[...] = x
        _swiglu_block(x, wg_sc[...], wl_sc[...], bg_ref, bl_ref, o_ref)

    for slot in range(len(ring)):
        pl.when((b < n_used) & (b % len(ring) == slot))(functools.partial(block, slot))

    @pl.when(b >= n_used)
    def _():
        o_ref[...] = jnp.zeros(o_ref.shape, o_ref.dtype)
        rows_ref[...] = jnp.zeros(rows_ref.shape, rows_ref.dtype)

    @pl.when(b == pl.num_programs(0) - 1)
    def _():
        for s in range(depth):
            spare = (n_used + s) % len(ring)
            for slot in range(len(ring)):
                pl.when(spare == slot)(
                    functools.partial(_wait_rows, x_hbm, ring[slot], rsem.at[slot]))


def _expert_up_rows_kernel(be_ref, ord_ref, nxt_ref, nused_ref, rows_ref, w_hbm, bg_ref, bl_ref,
                           o_ref, stage_g, stage_l, wsem, *, tiles):
    b = pl.program_id(0)
    n_used = nused_ref[0]
    wslot = _expert_weights(b, n_used, be_ref, ord_ref, nxt_ref,
                            [(w_hbm,) + t for t in tiles], [stage_g, stage_l], wsem)

    @pl.when(b < n_used)
    def _():
        _swiglu_block(rows_ref[...], stage_g[wslot], stage_l[wslot], bg_ref, bl_ref, o_ref)

    @pl.when(b >= n_used)
    def _():
        o_ref[...] = jnp.zeros(o_ref.shape, o_ref.dtype)


UP_TILE = 1024


def _expert_up(x, tok_list, tok_row0, sched, w_up, b_up):
    be, ordinal, nxt, n_used = sched
    n_rows = be.shape[0] * ROW_BLOCK
    tn = UP_TILE
    nf = EXPERT_FF // tn
    b_up3 = b_up.reshape(b_up.shape[0], 1, 2 * EXPERT_FF)
    weight_scratch = [pltpu.VMEM((2, D_MODEL, tn), F32), pltpu.VMEM((2, D_MODEL, tn), F32)]
    row_blk = lambda width: pl.BlockSpec((ROW_BLOCK, width), lambda b, *_: (b, 0))

    def bias(col_blk):
        return pl.BlockSpec((1, 1, tn), lambda b, be_, *_: (be_[b], 0, col_blk))

    def tiles(f):
        return ((f * tn, tn), ((nf + f) * tn, tn))

    act0, rows = pl.pallas_call(
        functools.partial(_expert_up_gather_kernel, tiles=tiles(0)),
        out_shape=[jax.ShapeDtypeStruct((n_rows, tn), BF16),
                   jax.ShapeDtypeStruct((n_rows, D_MODEL), BF16)],
        grid_spec=pltpu.PrefetchScalarGridSpec(
            num_scalar_prefetch=6,
            grid=(n_rows // ROW_BLOCK,),
            in_specs=[pl.BlockSpec(memory_space=pl.ANY), pl.BlockSpec(memory_space=pl.ANY),
                      bias(0), bias(nf)],
            out_specs=[row_blk(tn), row_blk(D_MODEL)],
            scratch_shapes=weight_scratch + [
                pltpu.VMEM((D_MODEL, tn), BF16), pltpu.VMEM((D_MODEL, tn), BF16),
                pltpu.VMEM((ROW_BLOCK, D_MODEL), F32), pltpu.VMEM((ROW_BLOCK, D_MODEL), F32),
                pltpu.VMEM((ROW_BLOCK, D_MODEL), F32),
                pltpu.SemaphoreType.DMA((2,)), pltpu.SemaphoreType.DMA((3,))],
        ),
        compiler_params=_params(("arbitrary",), 58),
        name="expert_up_gather",
    )(be, ordinal, nxt, n_used, tok_row0, tok_list, x, w_up, b_up3, b_up3)
    acts = [act0]
    for f in range(1, nf):
        acts.append(pl.pallas_call(
            functools.partial(_expert_up_rows_kernel, tiles=tiles(f)),
            out_shape=jax.ShapeDtypeStruct((n_rows, tn), BF16),
            grid_spec=pltpu.PrefetchScalarGridSpec(
                num_scalar_prefetch=4,
                grid=(n_rows // ROW_BLOCK,),
                in_specs=[row_blk(D_MODEL), pl.BlockSpec(memory_space=pl.ANY),
                          bias(f), bias(nf + f)],
                out_specs=row_blk(tn),
                scratch_shapes=weight_scratch + [pltpu.SemaphoreType.DMA((2,))],
            ),
            compiler_params=_params(("arbitrary",), 56),
            name="expert_up_rows",
        )(be, ordinal, nxt, n_used, rows, w_up, b_up3, b_up3))
    return acts


def _expert_down_kernel(be_ref, ord_ref, nxt_ref, nused_ref, *refs, n_act):
    act_refs = refs[:n_act]
    w_hbm, bias_ref, o_ref, stage, wsem = refs[n_act:]
    b = pl.program_id(0)
    n_used = nused_ref[0]
    slot = _expert_weights(b, n_used, be_ref, ord_ref, nxt_ref, [(w_hbm, 0, D_MODEL)], [stage],
                           wsem)

    @pl.when(b < n_used)
    def _():
        y = bias_ref[0]
        for f, act_ref in enumerate(act_refs):
            k0 = f * act_ref.shape[1]
            w = stage[slot, k0:k0 + act_ref.shape[1], :].astype(BF16)
            y = y + jnp.dot(act_ref[...], w, preferred_element_type=F32)
        o_ref[...] = y

    @pl.when(b >= n_used)
    def _():
        o_ref[...] = jnp.zeros(o_ref.shape, o_ref.dtype)


def _expert_down(acts, sched, w_down, b_down):
    be, ordinal, nxt, n_used = sched
    n_rows, tn = acts[0].shape
    b_down3 = b_down.reshape(b_down.shape[0], 1, D_MODEL)
    return pl.pallas_call(
        functools.partial(_expert_down_kernel, n_act=len(acts)),
        out_shape=jax.ShapeDtypeStruct((n_rows, D_MODEL), F32),
        grid_spec=pltpu.PrefetchScalarGridSpec(
            num_scalar_prefetch=4,
            grid=(n_rows // ROW_BLOCK,),
            in_specs=[pl.BlockSpec((ROW_BLOCK, tn), lambda b, *_: (b, 0)) for _ in acts] + [
                pl.BlockSpec(memory_space=pl.ANY),
                pl.BlockSpec((1, 1, D_MODEL), lambda b, be_, *_: (be_[b], 0, 0))],
            out_specs=pl.BlockSpec((ROW_BLOCK, D_MODEL), lambda b, *_: (b, 0)),
            scratch_shapes=[pltpu.VMEM((2, EXPERT_FF, D_MODEL), F32),
                            pltpu.SemaphoreType.DMA((2,))],
        ),
        compiler_params=_params(("arbitrary",), 56),
        name="expert_down",
    )(be, ordinal, nxt, n_used, *acts, w_down, b_down3)


def _combine_ln_kernel(pos_ref, y_hbm, gate_ref, x_ref, g_ref, b_ref, of_ref, ob_ref,
                       buf, sem):
    tc = x_ref.shape[0]
    i = pl.program_id(0)
    last = pl.num_programs(0) - 1
    cur = i % 2

    def start(tile, slot):
        for k in range(TOP_K):
            _start_rows(y_hbm, pos_ref, tile * (tc * TOP_K) + k, TOP_K, buf.at[slot, k],
                        sem.at[slot])

    def wait(slot):
        for k in range(TOP_K):
            _wait_rows(y_hbm, buf.at[slot, k], sem.at[slot])

    @pl.when(i == 0)
    def _():
        start(0, 0)

    wait(cur)
    start(jnp.where(i < last, i + 1, 0), 1 - cur)
    gate = gate_ref[...]
    ffn = gate[:, 0:1] * buf[cur, 0]
    for k in range(1, TOP_K):
        ffn = ffn + gate[:, k:k + 1] * buf[cur, k]
    y = _layer_norm(DEEPNORM_ALPHA * x_ref[...] + ffn, g_ref[...], b_ref[...])
    of_ref[...] = y
    ob_ref[...] = y.astype(BF16)

    @pl.when(i == last)
    def _():
        wait(1 - cur)


def _combine_ln(y_rows, pos, gate, x, g, b):
    t = x.shape[0]
    tc = min(128, t)
    row = lambda b_, p: (b_, 0)
    return pl.pallas_call(
        _combine_ln_kernel,
        out_shape=[jax.ShapeDtypeStruct((t, D_MODEL), F32),
                   jax.ShapeDtypeStruct((t, D_MODEL), BF16)],
        grid_spec=pltpu.PrefetchScalarGridSpec(
            num_scalar_prefetch=1,
            grid=(t // tc,),
            in_specs=[pl.BlockSpec(memory_space=pl.ANY),
                      pl.BlockSpec((tc, TOP_K), row),
                      pl.BlockSpec((tc, D_MODEL), row),
                      pl.BlockSpec((1, D_MODEL), lambda b_, p: (0, 0)),
                      pl.BlockSpec((1, D_MODEL), lambda b_, p: (0, 0))],
            out_specs=[pl.BlockSpec((tc, D_MODEL), row), pl.BlockSpec((tc, D_MODEL), row)],
            scratch_shapes=[pltpu.VMEM((2, TOP_K, tc, D_MODEL), F32),
                            pltpu.SemaphoreType.DMA((2,))],
        ),
        compiler_params=_params(("arbitrary",), 32),
        name="combine_ln",
    )(pos, y_rows, gate, x, g, b)


def _dispatch_plan(top_idx, rank, counts):
    t = top_idx.shape[0]
    n_assign = t * TOP_K
    n_blocks = -(-n_assign // ROW_BLOCK) + N_EXPERTS
    e_flat = top_idx.reshape(-1)
    padded = (counts + ROW_BLOCK - 1) // ROW_BLOCK * ROW_BLOCK
    pend = jnp.cumsum(padded)
    pstart = pend - padded
    dest = (pstart[e_flat] + rank.reshape(-1)).astype(I32)
    block_row0 = jnp.arange(n_blocks, dtype=I32) * ROW_BLOCK
    block_expert = jnp.minimum(jnp.sum((pend[None, :] <= block_row0[:, None]).astype(I32), axis=1),
                               N_EXPERTS - 1)
    n_used = (pend[-1:] // ROW_BLOCK).astype(I32)
    _, tok_list = lax.sort((dest, jnp.arange(n_assign, dtype=I32) // TOP_K), num_keys=1)
    tok_list = jnp.concatenate([tok_list, jnp.zeros((ROW_BLOCK,), I32)])
    pad_before = pstart - (jnp.cumsum(counts) - counts)
    tok_row0 = (block_row0 - pad_before[block_expert]).astype(I32)
    return dest, tok_list, tok_row0, block_expert, n_used


IN_FOX = N_BRANCHES * D_MODEL + 2 * RNN_WIDTH
IN_TAIL = IN_FOX + 3 * FOX_WIDTH


def _prep_in_tail(w_in, layer):
    tail = w_in[layer, :, IN_TAIL:]
    pad = LANES - MLA_ROPE_DIM - HEADS
    return jnp.concatenate([tail[:, HEADS:], tail[:, :HEADS], jnp.zeros((D_MODEL, pad), F32)],
                           axis=1).astype(BF16)


def _prep_uq(w_uq_l):
    w = w_uq_l.reshape(MLA_RANK, HEADS, MLA_QK_DIM) * (MLA_QK_DIM ** -0.5 * LOG2E)
    pad = jnp.zeros((MLA_RANK, HEADS, MLA_PAD_DIM - MLA_QK_DIM), F32)
    return jnp.concatenate([w, pad], axis=2).reshape(MLA_RANK, HEADS * MLA_PAD_DIM).astype(BF16)


def _layer(x, x_bf, tabs, p):
    t = x.shape[0]
    w_in, layer = p['w_in_all'], p['layer']
    w_in_t = jnp.swapaxes(w_in, 1, 2)
    u_a = _in_proj(x_bf, w_in_t, layer, 0, IN_FOX, jnp.ones((IN_FOX,), F32), F32,
                   "in_proj_gates_lru")
    q_scale = jnp.concatenate([jnp.full((FOX_WIDTH,), HEAD_DIM ** -0.5 * LOG2E, F32),
                               jnp.ones((2 * FOX_WIDTH,), F32)])
    u_b = _in_proj(x_bf, w_in_t, layer, IN_FOX, 3 * FOX_WIDTH, q_scale, BF16, "in_proj_fox")
    w_c = _prep_in_tail(w_in, layer)
    u_c = _matmul(x_bf, w_c, F32, 1024, w_c.shape[1], "in_proj_small")

    gate_blocks = N_BRANCHES * D_MODEL // RNN_WIDTH
    y_a = _rglru(u_a, gate_blocks, gate_blocks + 1,
                 p['conv_w'], p['conv_b'].reshape(1, RNN_WIDTH),
                 p['w_rec_gate'].astype(BF16), p['b_rec_gate'].reshape(1, RNN_WIDTH),
                 p['w_inp_gate'].astype(BF16), p['b_inp_gate'].reshape(1, RNN_WIDTH),
                 p['lru_lambda'].reshape(1, RNN_WIDTH))

    fl_col = 2 * MLA_RANK + MLA_ROPE_DIM
    cum = _fox_cum(u_c[:, fl_col:fl_col + HEADS], p['b_forget'])
    q_x, k_x = _fox_prep(u_b, cum)
    y_b = _causal_attention(q_x, k_x, u_b, 2 * FOX_WIDTH, "fox_attention")

    q_f, k_f, v_c = _mla_prep(u_c, p['g_cq'].reshape(1, MLA_RANK),
                              p['g_ckv'].reshape(1, MLA_RANK), _prep_uq(p['w_uq']),
                              p['w_ukv'].astype(BF16), tabs)
    y_c = _causal_attention(q_f, k_f, v_c, 0, "mla_attention")

    mixed = _merge(y_a, y_b, y_c, p['w_proj_lru_all'], p['w_proj_fox_all'], p['w_proj_mla_all'],
                   layer, u_a, p['b_merge'])
    x = _outproj_ln(mixed, p['w_out_all'], layer, x,
                    p['ln1_g'].reshape(1, D_MODEL), p['ln1_b'].reshape(1, D_MODEL))
    top_idx, gate, rank, counts = _router(x, p['w_router'].T, p['b_router'])
    dest, tok_list, tok_row0, block_expert, n_used = _dispatch_plan(top_idx, rank, counts)
    ordinal, nxt = _expert_schedule(block_expert, counts, n_used)
    base = p['expert_base']
    sched = (block_expert + base, ordinal, jnp.where(nxt >= 0, nxt + base, -1), n_used)
    acts = _expert_up(x, tok_list, tok_row0, sched, p['w_up_all'], p['b_up_all'])
    y_rows = _expert_down(acts, sched, p['w_down_all'], p['b_down_all'])
    del t
    return _combine_ln(y_rows, dest, gate, x, p['ln2_g'].reshape(1, D_MODEL),
                       p['ln2_b'].reshape(1, D_MODEL))


_LAYER_PARAMS = ('w_in', 'b_merge', 'b_forget', 'conv_w', 'conv_b', 'w_rec_gate', 'b_rec_gate',
                 'w_inp_gate', 'b_inp_gate', 'lru_lambda', 'g_cq', 'g_ckv', 'w_uq', 'w_ukv',
                 'w_proj_lru', 'w_proj_fox', 'w_proj_mla', 'w_out', 'ln1_g', 'ln1_b',
                 'w_router', 'b_router', 'w_up', 'b_up', 'w_down', 'b_down', 'ln2_g', 'ln2_b')


def kernel(x, positions, w_in, b_merge, b_forget, conv_w, conv_b, w_rec_gate, b_rec_gate,
           w_inp_gate, b_inp_gate, lru_lambda, g_cq, g_ckv, w_uq, w_ukv, w_proj_lru,
           w_proj_fox, w_proj_mla, w_out, ln1_g, ln1_b, w_router, b_router, w_up, b_up,
           w_down, b_down, ln2_g, ln2_b):
    stacked = dict(zip(_LAYER_PARAMS, (
        w_in, b_merge, b_forget, conv_w, conv_b, w_rec_gate, b_rec_gate, w_inp_gate,
        b_inp_gate, lru_lambda, g_cq, g_ckv, w_uq, w_ukv, w_proj_lru, w_proj_fox,
        w_proj_mla, w_out, ln1_g, ln1_b, w_router, b_router, w_up, b_up, w_down, b_down,
        ln2_g, ln2_b)))
    batch, seq, _ = x.shape
    assert batch == 1
    xt = x.reshape(seq, D_MODEL)
    x_bf = xt.astype(BF16)
    tabs = _rope_tables(positions.reshape(seq))
    expert_stack = {'w_up', 'b_up', 'w_down', 'b_down'}
    shared = {k + '_all': stacked[k].reshape((-1,) + stacked[k].shape[2:]) for k in expert_stack}
    dense_stack = {'w_in', 'w_proj_lru', 'w_proj_fox', 'w_proj_mla', 'w_out'}
    shared.update({k + '_all': stacked[k] for k in dense_stack})
    in_place = expert_stack | dense_stack
    for l in range(w_in.shape[0]):
        p = {k: v[l] for k, v in stacked.items() if k not in in_place}
        xt, x_bf = _layer(xt, x_bf, tabs,
                          dict(p, layer=l, expert_base=l * N_EXPERTS, **shared))
    return xt.reshape(batch, seq, D_MODEL)
```

```python
import functools

import jax
import jax.numpy as jnp
import numpy as np
from jax import lax
from jax.experimental import pallas as pl
from jax.experimental.pallas import tpu as pltpu

F32 = jnp.float32
BF16 = jnp.bfloat16
I32 = jnp.int32

D_MODEL = 2048
N_BRANCHES = 3
DEEPNORM_ALPHA = (2 * 2) ** 0.25
LN_EPS = 1e-5
RMS_EPS = 1e-6
RNN_WIDTH = 1024
RNN_BLOCKS = 8
RNN_BLOCK_W = RNN_WIDTH // RNN_BLOCKS
CONV_WIDTH = 4
RG_LRU_C = 8.0
HEADS = 8
HEAD_DIM = 128
FOX_WIDTH = HEADS * HEAD_DIM
MLA_RANK = 512
MLA_ROPE_DIM = 64
MLA_QK_DIM = HEAD_DIM + MLA_ROPE_DIM
ATTN_QK_DIM = 256
MLA_PAD_DIM = ATTN_QK_DIM
LOG2E = float(np.log2(np.e))
ROPE_THETA = 10000.0
N_EXPERTS = 32
TOP_K = 4
EXPERT_FF = D_MODEL
SWIGLU_ALPHA = 1.702
SWIGLU_LIMIT = 7.0

LANES = 128
SUBLANES = 8
V7X_VMEM_BYTES = 64 * 1024 * 1024

ROW_BLOCK = 256
ATTN_BLOCK = 512
GATHER_UNROLL = 8
ATTN_GROUP = 8
MASK_VALUE = -1e30


def _params(semantics, vmem_mib):
    assert vmem_mib * 1024 * 1024 < V7X_VMEM_BYTES
    return pltpu.CompilerParams(dimension_semantics=semantics,
                                vmem_limit_bytes=vmem_mib * 1024 * 1024)


def _split3(x):
    hi = x.astype(BF16)
    r = x - hi.astype(F32)
    mid = r.astype(BF16)
    lo = (r - mid.astype(F32)).astype(BF16)
    return hi, mid, lo


def _sigmoid(x):
    return 1.0 / (1.0 + jnp.exp(-x))


def _softplus(z):
    return jnp.maximum(z, 0.0) + jnp.log1p(jnp.exp(-jnp.abs(z)))


def _layer_norm(v, g, b):
    mu = jnp.mean(v, axis=-1, keepdims=True)
    c = v - mu
    var = jnp.mean(c * c, axis=-1, keepdims=True)
    return c * lax.rsqrt(var + LN_EPS) * g + b


def _mm_kernel(x_ref, w_ref, o_ref):
    o_ref[...] = jnp.dot(x_ref[...], w_ref[...],
                         preferred_element_type=F32).astype(o_ref.dtype)


def _matmul(x, w, out_dtype, tm, tn, name):
    m, k = x.shape
    n = w.shape[1]
    tm, tn = min(tm, m), min(tn, n)
    return pl.pallas_call(
        _mm_kernel,
        out_shape=jax.ShapeDtypeStruct((m, n), out_dtype),
        grid=(m // tm, n // tn),
        in_specs=[pl.BlockSpec((tm, k), lambda i, j: (i, 0)),
                  pl.BlockSpec((k, tn), lambda i, j: (0, j))],
        out_specs=pl.BlockSpec((tm, tn), lambda i, j: (i, j)),
        compiler_params=_params(("parallel", "arbitrary"), 48),
        name=name,
    )(x, w)


def _in_proj_kernel(x_ref, w_ref, s_ref, o_ref, w_sc):
    @pl.when(pl.program_id(1) == 0)
    def _():
        w_sc[...] = (w_ref[0] * s_ref[...]).T.astype(BF16)

    o_ref[...] = jnp.dot(x_ref[...], w_sc[...],
                         preferred_element_type=F32).astype(o_ref.dtype)


def _in_proj(x, w_stack_t, layer, col0, ncols, col_scale, out_dtype, name):
    m, k = x.shape
    tm, tn = min(1024, m), 1024
    assert col0 % tn == 0 and ncols % tn == 0
    blk0 = col0 // tn
    return pl.pallas_call(
        _in_proj_kernel,
        out_shape=jax.ShapeDtypeStruct((m, ncols), out_dtype),
        grid=(ncols // tn, m // tm),
        in_specs=[pl.BlockSpec((tm, k), lambda j, i: (i, 0)),
                  pl.BlockSpec((1, tn, k), lambda j, i: (layer, blk0 + j, 0)),
                  pl.BlockSpec((tn, 1), lambda j, i: (j, 0))],
        out_specs=pl.BlockSpec((tm, tn), lambda j, i: (i, j)),
        scratch_shapes=[pltpu.VMEM((k, tn), BF16)],
        compiler_params=_params(("arbitrary", "arbitrary"), 48),
        name=name,
    )(x, w_stack_t, col_scale.reshape(ncols, 1))


def _fox_cum_kernel(fl_ref, b_ref, o_ref, *, chunks):
    z = fl_ref[...] + b_ref[...]
    lf = jnp.minimum(z, 0.0) - jnp.log1p(jnp.exp(-jnp.abs(z)))
    rows = lf.shape[0]
    s = lax.broadcasted_iota(I32, (LANES, LANES), 0)
    t = lax.broadcasted_iota(I32, (LANES, LANES), 1)
    tri = (s <= t).astype(BF16)
    incl = None
    for part in _split3(lf):
        d = jnp.dot(part, tri, preferred_element_type=F32)
        incl = d if incl is None else incl + d
    i = lax.broadcasted_iota(I32, (rows, rows), 0)
    j = lax.broadcasted_iota(I32, (rows, rows), 1)
    lower = ((i // chunks == j // chunks) & (j < i)).astype(BF16)
    offs = None
    for part in _split3(incl):
        d = jnp.dot(lower, part, preferred_element_type=F32)
        offs = d if offs is None else offs + d
    o_ref[...] = incl + offs[:, LANES - 1:LANES]


def _fox_cum(f_logit, b_f):
    t = f_logit.shape[0]
    chunks = t // LANES
    fl = f_logit.T.reshape(HEADS * chunks, LANES)
    b = jnp.repeat(b_f.astype(F32), chunks).reshape(HEADS * chunks, 1)
    out = pl.pallas_call(
        functools.partial(_fox_cum_kernel, chunks=chunks),
        out_shape=jax.ShapeDtypeStruct((HEADS * chunks, LANES), F32),
        name="fox_cum",
    )(fl, b)
    return out.reshape(HEADS, t)


def _attn_kernel(qi_ref, ki_ref, q_ref, k_ref, v_ref, o_ref, m_sc, l_sc, acc_sc, *, group):
    p = pl.program_id(1)
    qi = qi_ref[p]
    ki = ki_ref[p]
    blk = q_ref.shape[0]

    @pl.when(ki == 0)
    def _():
        m_sc[...] = jnp.full(m_sc.shape, MASK_VALUE, F32)
        l_sc[...] = jnp.zeros(l_sc.shape, F32)
        acc_sc[...] = jnp.zeros(acc_sc.shape, F32)

    def step(masked):
        for g in range(group):
            qk = slice(g * ATTN_QK_DIM, (g + 1) * ATTN_QK_DIM)
            vo = slice(g * HEAD_DIM, (g + 1) * HEAD_DIM)
            s = lax.dot_general(q_ref[:, qk], k_ref[:, qk], (((1,), (1,)), ((), ())),
                                preferred_element_type=F32)
            if masked:
                row = lax.broadcasted_iota(I32, s.shape, 0)
                col = lax.broadcasted_iota(I32, s.shape, 1)
                s = jnp.where(col <= row, s, MASK_VALUE)
            chunks = [s[:, c * LANES:(c + 1) * LANES] for c in range(blk // LANES)]
            cmax = functools.reduce(jnp.maximum, chunks)
            m_prev = m_sc[g]
            m_new = jnp.maximum(m_prev, jnp.max(cmax, axis=1, keepdims=True))
            alpha = jnp.exp2(m_prev - m_new)
            probs = [jnp.exp2(c - m_new) for c in chunks]
            l_sc[g] = alpha * l_sc[g] + functools.reduce(jnp.add, probs)
            pb = jnp.concatenate(probs, axis=1).astype(BF16)
            acc_sc[g] = alpha * acc_sc[g] + jnp.dot(pb, v_ref[:, vo],
                                                    preferred_element_type=F32)
            m_sc[g] = m_new

    @pl.when(ki < qi)
    def _():
        step(False)

    @pl.when(ki == qi)
    def _():
        step(True)
        for g in range(group):
            denom = jnp.sum(l_sc[g], axis=1, keepdims=True)
            o_ref[:, g * HEAD_DIM:(g + 1) * HEAD_DIM] = (acc_sc[g] / denom).astype(o_ref.dtype)


def _causal_attention(q, k, v, v_col, name):
    t = q.shape[0]
    blk = min(ATTN_BLOCK, t)
    nq = t // blk
    group = ATTN_GROUP
    pairs = [(a, b) for a in range(nq) for b in range(a + 1)]
    qi_tab = jnp.asarray(np.array([a for a, _ in pairs], np.int32))
    ki_tab = jnp.asarray(np.array([b for _, b in pairs], np.int32))
    v_blk = v_col // (group * HEAD_DIM)
    return pl.pallas_call(
        functools.partial(_attn_kernel, group=group),
        out_shape=jax.ShapeDtypeStruct((t, HEADS * HEAD_DIM), BF16),
        grid_spec=pltpu.PrefetchScalarGridSpec(
            num_scalar_prefetch=2,
            grid=(HEADS // group, len(pairs)),
            in_specs=[
                pl.BlockSpec((blk, group * ATTN_QK_DIM), lambda h, p, qi, ki: (qi[p], h)),
                pl.BlockSpec((blk, group * ATTN_QK_DIM), lambda h, p, qi, ki: (ki[p], h)),
                pl.BlockSpec((blk, group * HEAD_DIM), lambda h, p, qi, ki: (ki[p], v_blk + h)),
            ],
            out_specs=pl.BlockSpec((blk, group * HEAD_DIM), lambda h, p, qi, ki: (qi[p], h)),
            scratch_shapes=[pltpu.VMEM((group, blk, LANES), F32),
                            pltpu.VMEM((group, blk, LANES), F32),
                            pltpu.VMEM((group, blk, HEAD_DIM), F32)],
        ),
        compiler_params=_params(("parallel", "arbitrary"), 32),
        name=name,
    )(qi_tab, ki_tab, q, k, v)


def _fox_prep_kernel(q_ref, k_ref, c_ref, sel_ref, qo_ref, ko_ref):
    bias = None
    for j, part in enumerate(_split3(c_ref[...])):
        d = jnp.dot(part, sel_ref[j], preferred_element_type=F32)
        bias = d if bias is None else bias + d
    lane = lax.broadcasted_iota(I32, (q_ref.shape[0], LANES), 1)
    for h in range(HEADS):
        src = slice(h * HEAD_DIM, (h + 1) * HEAD_DIM)
        lo = h * ATTN_QK_DIM
        g = bias[:, src]
        qo_ref[:, lo:lo + HEAD_DIM] = q_ref[:, src]
        ko_ref[:, lo:lo + HEAD_DIM] = k_ref[:, src]
        qo_ref[:, lo + HEAD_DIM:lo + ATTN_QK_DIM] = jnp.where(
            (lane >= 3) & (lane < 6), 1.0, g).astype(BF16)
        ko_ref[:, lo + HEAD_DIM:lo + ATTN_QK_DIM] = jnp.where(
            lane < 3, 1.0, -pltpu.roll(g, 3, axis=1)).astype(BF16)


def _fox_prep(u_b, cum):
    t = u_b.shape[0]
    tm = min(512, t)
    c = jnp.pad(cum.T * LOG2E, ((0, 0), (0, LANES - HEADS)))
    sel = np.zeros((3, LANES, FOX_WIDTH), np.float32)
    for j in range(3):
        for h in range(HEADS):
            sel[j, h, h * HEAD_DIM + j] = 1.0
    wide = HEADS * ATTN_QK_DIM
    return pl.pallas_call(
        _fox_prep_kernel,
        out_shape=[jax.ShapeDtypeStruct((t, wide), BF16)] * 2,
        grid=(t // tm,),
        in_specs=[pl.BlockSpec((tm, FOX_WIDTH), lambda i: (i, 0)),
                  pl.BlockSpec((tm, FOX_WIDTH), lambda i: (i, 1)),
                  pl.BlockSpec((tm, LANES), lambda i: (i, 0)),
                  pl.BlockSpec((3, LANES, FOX_WIDTH), lambda i: (0, 0, 0))],
        out_specs=[pl.BlockSpec((tm, wide), lambda i: (i, 0))] * 2,
        compiler_params=_params(("arbitrary",), 32),
        name="fox_prep",
    )(u_b, u_b, c, jnp.asarray(sel, BF16))


def _rope_table_kernel(pos_ref, inv_ref, c_ref, s1_ref, s2_ref):
    half = MLA_ROPE_DIM // 2
    ang = pos_ref[...].astype(F32) * inv_ref[...]
    c = jnp.cos(ang)
    s = jnp.sin(ang)
    lane = lax.broadcasted_iota(I32, ang.shape, 1)
    c_ref[...] = jnp.where(lane < MLA_ROPE_DIM, c, 0.0)
    s1_ref[...] = jnp.where(lane < half, -s, 0.0)
    s2_ref[...] = jnp.where((lane >= half) & (lane < MLA_ROPE_DIM), s, 0.0)


def _rope_tables(positions):
    t = positions.shape[0]
    half = MLA_ROPE_DIM // 2
    inv_freq = ROPE_THETA ** (-jnp.arange(half, dtype=F32) / half)
    inv = jnp.concatenate([inv_freq, inv_freq, jnp.zeros((LANES - 2 * half,), F32)])
    tm = min(1024, t)
    spec = pl.BlockSpec((tm, LANES), lambda i: (i, 0))
    return pl.pallas_call(
        _rope_table_kernel,
        out_shape=[jax.ShapeDtypeStruct((t, LANES), F32)] * 3,
        grid=(t // tm,),
        in_specs=[pl.BlockSpec((tm, 1), lambda i: (i, 0)),
                  pl.BlockSpec((1, LANES), lambda i: (0, 0))],
        out_specs=[spec, spec, spec],
        name="rope_tables",
    )(positions.reshape(t, 1), inv.reshape(1, LANES))


def _rope_group(g, c, s1, s2):
    half = MLA_ROPE_DIM // 2
    return (g * c + pltpu.roll(g, LANES - half, axis=1) * s1
            + pltpu.roll(g, half, axis=1) * s2)


def _mla_prep_kernel(u_ref, gq_ref, gkv_ref, wq_ref, wkv_ref, c_ref, s1_ref, s2_ref,
                     q_ref, k_ref, v_ref):
    def rms(v, g):
        ms = jnp.mean(v * v, axis=-1, keepdims=True)
        return (v * lax.rsqrt(ms + RMS_EPS) * g).astype(BF16)

    c, s1, s2 = c_ref[...], s1_ref[...], s2_ref[...]
    cq = rms(u_ref[:, 0:MLA_RANK], gq_ref[...])
    ckv = rms(u_ref[:, MLA_RANK:2 * MLA_RANK], gkv_ref[...])
    q_pre = jnp.dot(cq, wq_ref[...], preferred_element_type=F32)
    kv = jnp.dot(ckv, wkv_ref[...], preferred_element_type=F32)
    k_rot = _rope_group(u_ref[:, 2 * MLA_RANK:2 * MLA_RANK + LANES], c, s1, s2).astype(BF16)
    for h in range(HEADS):
        lo = h * MLA_PAD_DIM
        mid = lo + HEAD_DIM
        hi = lo + MLA_PAD_DIM
        q_ref[:, lo:mid] = q_pre[:, lo:mid].astype(BF16)
        q_ref[:, mid:hi] = _rope_group(q_pre[:, mid:hi], c, s1, s2).astype(BF16)
        k_ref[:, lo:mid] = kv[:, lo:mid].astype(BF16)
        k_ref[:, mid:hi] = k_rot
        v_ref[:, h * HEAD_DIM:(h + 1) * HEAD_DIM] = kv[:, mid:hi].astype(BF16)


def _mla_prep(u_small, g_cq, g_ckv, w_uq_r, w_ukv, tabs):
    t = u_small.shape[0]
    tm = min(512, t)
    wide = HEADS * MLA_PAD_DIM
    row = lambda w: pl.BlockSpec((tm, w), lambda i: (i, 0))
    const = lambda a: pl.BlockSpec(a.shape, lambda i: (0, 0))
    return pl.pallas_call(
        _mla_prep_kernel,
        out_shape=[jax.ShapeDtypeStruct((t, wide), BF16),
                   jax.ShapeDtypeStruct((t, wide), BF16),
                   jax.ShapeDtypeStruct((t, HEADS * HEAD_DIM), BF16)],
        grid=(t // tm,),
        in_specs=[row(u_small.shape[1]), const(g_cq), const(g_ckv), const(w_uq_r),
                  const(w_ukv), row(LANES), row(LANES), row(LANES)],
        out_specs=[row(wide), row(wide), row(HEADS * HEAD_DIM)],
        compiler_params=_params(("arbitrary",), 48),
        name="mla_prep",
    )(u_small, g_cq, g_ckv, w_uq_r, w_ukv, *tabs)


def _rglru_kernel(y_ref, x_ref, cw_ref, cb_ref, wr_ref, br_ref, wi_ref, bi_ref, lam_ref,
                  o_ref, xprev_sc, h_sc, a_sc, g_sc, hs_sc):
    tt = x_ref.shape[0]

    @pl.when(pl.program_id(0) == 0)
    def _():
        xprev_sc[...] = jnp.zeros(xprev_sc.shape, F32)
        h_sc[...] = jnp.zeros(h_sc.shape, F32)

    x = x_ref[...]
    xext = jnp.concatenate([xprev_sc[...], x], axis=0)
    xprev_sc[...] = x[tt - SUBLANES:, :]
    xc = cb_ref[...]
    for j in range(CONV_WIDTH):
        off = SUBLANES - (CONV_WIDTH - 1) + j
        xc = xc + cw_ref[j:j + 1, :] * xext[off:off + tt, :]

    r_parts, i_parts = [], []
    for n in range(RNN_BLOCKS):
        sl = slice(n * RNN_BLOCK_W, (n + 1) * RNN_BLOCK_W)
        xb = xc[:, sl].astype(BF16)
        r_parts.append(_sigmoid(jnp.dot(xb, wr_ref[n], preferred_element_type=F32)
                                + br_ref[:, sl]))
        i_parts.append(_sigmoid(jnp.dot(xb, wi_ref[n], preferred_element_type=F32)
                                + bi_ref[:, sl]))
    r = jnp.concatenate(r_parts, axis=1)
    gate_i = jnp.concatenate(i_parts, axis=1)
    log_a = (-RG_LRU_C * _softplus(-lam_ref[...])) * r
    a = jnp.exp(log_a)
    a_sc[...] = a
    g_sc[...] = jnp.sqrt(-jnp.tanh(log_a) * (a * a + 1.0)) * (gate_i * xc)

    sub = lax.broadcasted_iota(I32, (SUBLANES, RNN_WIDTH), 0)

    def tile_scan(k, h_in):
        rows = pl.ds(pl.multiple_of(k * SUBLANES, SUBLANES), SUBLANES)
        a = a_sc[rows, :]
        g = g_sc[rows, :]
        for d in (1, 2, 4):
            keep = sub >= d
            g = jnp.where(keep, a * pltpu.roll(g, d, axis=0) + g, g)
            a = jnp.where(keep, a * pltpu.roll(a, d, axis=0), a)
        hs = a * h_in + g
        hs_sc[rows, :] = hs
        return jnp.broadcast_to(hs[SUBLANES - 1:SUBLANES, :], (SUBLANES, RNN_WIDTH))

    h_sc[...] = lax.fori_loop(0, tt // SUBLANES, tile_scan, h_sc[...])

    y = y_ref[...]
    gelu = 0.5 * y * (1.0 + jnp.tanh(np.sqrt(2.0 / np.pi) * (y + 0.044715 * (y * y * y))))
    o_ref[...] = (hs_sc[...] * gelu).astype(o_ref.dtype)


def _rglru(u_a, y_blk, x_blk, conv_w, conv_b, w_rg, b_rg, w_ig, b_ig, lam):
    t = u_a.shape[0]
    tt = min(256, t)
    const2 = lambda a: pl.BlockSpec(a.shape, lambda i: (0, 0))
    const3 = lambda a: pl.BlockSpec(a.shape, lambda i: (0, 0, 0))
    return pl.pallas_call(
        _rglru_kernel,
        out_shape=jax.ShapeDtypeStruct((t, RNN_WIDTH), BF16),
        grid=(t // tt,),
        in_specs=[pl.BlockSpec((tt, RNN_WIDTH), lambda i: (i, y_blk)),
                  pl.BlockSpec((tt, RNN_WIDTH), lambda i: (i, x_blk)),
                  const2(conv_w), const2(conv_b), const3(w_rg), const2(b_rg),
                  const3(w_ig), const2(b_ig), const2(lam)],
        out_specs=pl.BlockSpec((tt, RNN_WIDTH), lambda i: (i, 0)),
        scratch_shapes=[pltpu.VMEM((SUBLANES, RNN_WIDTH), F32),
                        pltpu.VMEM((SUBLANES, RNN_WIDTH), F32),
                        pltpu.VMEM((tt, RNN_WIDTH), F32),
                        pltpu.VMEM((tt, RNN_WIDTH), F32),
                        pltpu.VMEM((tt, RNN_WIDTH), F32)],
        compiler_params=_params(("arbitrary",), 32),
        name="rglru",
    )(u_a, u_a, conv_w, conv_b, w_rg, b_rg, w_ig, b_ig, lam)


def _merge_kernel(a_ref, b_ref, c_ref, wa_ref, wb_ref, wc_ref, g0_ref, g1_ref, g2_ref,
                  bm_ref, o_ref):
    def branch(x_ref, w_ref, g_ref, n):
        y = jnp.dot(x_ref[...], w_ref[0].astype(BF16), preferred_element_type=F32)
        return _sigmoid(g_ref[...] + bm_ref[n:n + 1, :]) * y

    mixed = (branch(a_ref, wa_ref, g0_ref, 0) + branch(b_ref, wb_ref, g1_ref, 1)
             + branch(c_ref, wc_ref, g2_ref, 2))
    o_ref[...] = mixed.astype(o_ref.dtype)


def _merge(ya, yb, yc, wa, wb, wc, layer, u_a, b_merge):
    t = ya.shape[0]
    tm, tn = min(512, t), 1024
    nn = D_MODEL // tn
    xin = pl.BlockSpec((tm, ya.shape[1]), lambda j, i: (i, 0))
    win = pl.BlockSpec((1, ya.shape[1], tn), lambda j, i: (layer, 0, j))
    gate = lambda n: pl.BlockSpec((tm, tn), lambda j, i: (i, n * nn + j))
    return pl.pallas_call(
        _merge_kernel,
        out_shape=jax.ShapeDtypeStruct((t, D_MODEL), BF16),
        grid=(nn, t // tm),
        in_specs=[xin, xin, xin, win, win, win, gate(0), gate(1), gate(2),
                  pl.BlockSpec((N_BRANCHES, tn), lambda j, i: (0, j))],
        out_specs=pl.BlockSpec((tm, tn), lambda j, i: (i, j)),
        compiler_params=_params(("arbitrary", "arbitrary"), 56),
        name="merge",
    )(ya, yb, yc, wa, wb, wc, u_a, u_a, u_a, b_merge)


def _outproj_ln_kernel(m_ref, w_ref, x_ref, g_ref, b_ref, o_ref):
    mix = jnp.dot(m_ref[...], w_ref[0].astype(BF16), preferred_element_type=F32)
    o_ref[...] = _layer_norm(DEEPNORM_ALPHA * x_ref[...] + mix, g_ref[...], b_ref[...])


def _outproj_ln(mixed, w_out, layer, x, g, b):
    t = x.shape[0]
    tm = min(256, t)
    row = pl.BlockSpec((tm, D_MODEL), lambda i: (i, 0))
    vec = pl.BlockSpec((1, D_MODEL), lambda i: (0, 0))
    return pl.pallas_call(
        _outproj_ln_kernel,
        out_shape=jax.ShapeDtypeStruct((t, D_MODEL), F32),
        grid=(t // tm,),
        in_specs=[row, pl.BlockSpec((1, D_MODEL, D_MODEL), lambda i: (layer, 0, 0)), row, vec,
                  vec],
        out_specs=row,
        compiler_params=_params(("arbitrary",), 56),
        name="outproj_ln",
    )(mixed, w_out, x, g, b)


def _route(x, w_ref, b_ref, idx_ref, gate_ref, rank_ref, count_ref, count_sc):
    nt = (((1,), (1,)), ((), ()))
    x_hi = x.astype(BF16)
    x_lo = (x - x_hi.astype(F32)).astype(BF16)
    w = w_ref[...]
    w_hi = w.astype(BF16)
    w_lo = (w - w_hi.astype(F32)).astype(BF16)
    logits = (lax.dot_general(w_hi, x_hi, nt, preferred_element_type=F32)
              + lax.dot_general(w_hi, x_lo, nt, preferred_element_type=F32)
              + lax.dot_general(w_lo, x_hi, nt, preferred_element_type=F32)
              + b_ref[...])
    eidx = lax.broadcasted_iota(I32, logits.shape, 0)
    vals, idxs = [], []
    for _ in range(TOP_K):
        m = jnp.max(logits, axis=0, keepdims=True)
        idx = jnp.min(jnp.where(logits == m, eidx, N_EXPERTS), axis=0, keepdims=True)
        vals.append(m)
        idxs.append(idx)
        logits = jnp.where(eidx == idx, -jnp.inf, logits)
    exps = [jnp.exp(v - vals[0]) for v in vals]
    denom = exps[0] + exps[1] + exps[2] + exps[3]
    pad = SUBLANES - TOP_K
    tokens = logits.shape[1]
    idx_ref[...] = jnp.concatenate(idxs + [jnp.zeros((pad, tokens), I32)], axis=0)
    gate_ref[...] = jnp.concatenate([e / denom for e in exps]
                                    + [jnp.zeros((pad, tokens), F32)], axis=0)

    @pl.when(pl.program_id(0) == 0)
    def _():
        count_sc[...] = jnp.zeros(count_sc.shape, F32)

    src = lax.broadcasted_iota(I32, (tokens, tokens), 0)
    dst = lax.broadcasted_iota(I32, (tokens, tokens), 1)
    before = (src < dst).astype(BF16)
    seen = count_sc[...]
    ranks = []
    for idx in idxs:
        hit = eidx == idx
        prefix = jnp.dot(hit.astype(BF16), before, preferred_element_type=F32)
        ranks.append(jnp.sum(jnp.where(hit, seen + prefix, 0.0), axis=0, keepdims=True))
        seen = seen + jnp.sum(hit.astype(F32), axis=1, keepdims=True)
    count_sc[...] = seen
    rank_ref[...] = jnp.concatenate(ranks + [jnp.zeros((pad, tokens), F32)],
                                    axis=0).astype(I32)
    count_ref[...] = seen.astype(I32)


def _router_kernel(x_ref, w_ref, b_ref, idx_ref, gate_ref, rank_ref, count_ref, count_sc):
    _route(x_ref[...], w_ref, b_ref, idx_ref, gate_ref, rank_ref, count_ref, count_sc)


def _router(x, w_router_t, b_router):
    t = x.shape[0]
    tm = min(1024, t)
    tok = pl.BlockSpec((SUBLANES, tm), lambda i: (0, i))
    idx, gate, rank, count = pl.pallas_call(
        _router_kernel,
        out_shape=[jax.ShapeDtypeStruct((SUBLANES, t), I32),
                   jax.ShapeDtypeStruct((SUBLANES, t), F32),
                   jax.ShapeDtypeStruct((SUBLANES, t), I32),
                   jax.ShapeDtypeStruct((N_EXPERTS, 1), I32)],
        grid=(t // tm,),
        in_specs=[pl.BlockSpec((tm, D_MODEL), lambda i: (i, 0)),
                  pl.BlockSpec((N_EXPERTS, D_MODEL), lambda i: (0, 0)),
                  pl.BlockSpec((N_EXPERTS, 1), lambda i: (0, 0))],
        out_specs=[tok, tok, tok, pl.BlockSpec((N_EXPERTS, 1), lambda i: (0, 0))],
        scratch_shapes=[pltpu.VMEM((N_EXPERTS, 1), F32)],
        compiler_params=_params(("arbitrary",), 48),
        name="router",
    )(x, w_router_t, b_router.reshape(N_EXPERTS, 1))
    return idx[:TOP_K].T, gate[:TOP_K].T, rank[:TOP_K].T, count[:, 0]


def _row_copy(src_hbm, row, dst, dst_row, sem):
    return pltpu.make_async_copy(src_hbm.at[pl.ds(row, 1), :],
                                 dst.at[pl.ds(dst_row, 1), :], sem)


def _start_rows(src_hbm, idx_ref, idx0, stride, dst, sem):
    for r in range(dst.shape[0]):
        _row_copy(src_hbm, idx_ref[idx0 + r * stride], dst, r, sem).start()


def _wait_rows(src_hbm, dst, sem):
    pltpu.make_async_copy(src_hbm.at[pl.ds(0, dst.shape[0]), :], dst, sem).wait()


def _per_expert(table, expert_ids):
    hit = expert_ids[..., None] == jnp.arange(N_EXPERTS, dtype=I32)
    return jnp.sum(jnp.where(hit, table, 0), axis=-1)


def _expert_schedule(block_expert, counts, n_used):
    b = jnp.arange(block_expert.shape[0], dtype=I32)
    prev = jnp.concatenate([block_expert[:1] - 1, block_expert[:-1]])
    first = (b < n_used[0]) & (block_expert != prev)
    ordinal = jnp.cumsum(first.astype(I32)) - 1
    e = jnp.arange(N_EXPERTS, dtype=I32)
    later_live = (counts > 0)[None, :] & (e[None, :] > e[:, None])
    nxt = jnp.min(jnp.where(later_live, e[None, :], N_EXPERTS), axis=1)
    nxt = jnp.where(nxt == N_EXPERTS, -1, nxt).astype(I32)
    return ordinal, _per_expert(nxt, block_expert)


def _expert_weights(b, n_used, be_ref, ord_ref, nxt_ref, tiles, stages, sem, casts=()):
    def copies(e, slot):
        return [pltpu.make_async_copy(w.at[e, :, pl.ds(c0, width)], st.at[slot], sem.at[slot])
                for (w, c0, width), st in zip(tiles, stages)]

    e = be_ref[b]
    first = (b < n_used) & ((b == 0) | (e != be_ref[jnp.maximum(b - 1, 0)]))

    @pl.when(b == 0)
    def _():
        for c in copies(e, 0):
            c.start()

    @pl.when(first)
    def _():
        slot = ord_ref[b] % 2
        for c in copies(e, slot):
            c.wait()
        nxt = nxt_ref[b]

        @pl.when(nxt >= 0)
        def _():
            for c in copies(nxt, 1 - slot):
                c.start()

        for st, dst in zip(stages, casts):
            dst[...] = st[slot].astype(BF16)

    return ord_ref[b] % 2


def _swiglu_block(x, w_glu, w_lin, bg_ref, bl_ref, o_ref):
    h_glu = jnp.dot(x, w_glu.astype(BF16), preferred_element_type=F32) + bg_ref[0]
    h_lin = jnp.dot(x, w_lin.astype(BF16), preferred_element_type=F32) + bl_ref[0]
    h_glu = jnp.minimum(h_glu, SWIGLU_LIMIT)
    h_lin = jnp.clip(h_lin, -SWIGLU_LIMIT, SWIGLU_LIMIT)
    act = h_glu * _sigmoid(SWIGLU_ALPHA * h_glu) * (h_lin + 1.0)
    o_ref[...] = act.astype(o_ref.dtype)


def _expert_up_gather_kernel(be_ref, ord_ref, nxt_ref, nused_ref, row0_ref, tok_ref, x_hbm, w_hbm,
                             bg_ref, bl_ref, o_ref, rows_ref, stage_g, stage_l, wg_sc, wl_sc,
                             rows_a, rows_b, rows_c, wsem, rsem, *, tiles):
    ring = (rows_a, rows_b, rows_c)
    depth = len(ring) - 1
    b = pl.program_id(0)
    n_used = nused_ref[0]

    def start(blk, slot):
        _start_rows(x_hbm, tok_ref, row0_ref[blk], 1, ring[slot], rsem.at[slot])

    @pl.when(b == 0)
    def _():
        for s in range(depth):
            start(jnp.minimum(s, n_used - 1), s)

    _expert_weights(b, n_used, be_ref, ord_ref, nxt_ref, [(w_hbm,) + t for t in tiles],
                    [stage_g, stage_l], wsem, casts=[wg_sc, wl_sc])

    def block(slot):
        cur = ring[slot]
        _wait_rows(x_hbm, cur, rsem.at[slot])
        start(jnp.minimum(b + depth, n_used - 1), (slot + depth) % len(ring))
        x = cur[...].astype(BF16)
        rows_ref[...] = x
        _swiglu_block(x, wg_sc[...], wl_sc[...], bg_ref, bl_ref, o_ref)

    for slot in range(len(ring)):
        pl.when((b < n_used) & (b % len(ring) == slot))(functools.partial(block, slot))

    @pl.when(b >= n_used)
    def _():
        o_ref[...] = jnp.zeros(o_ref.shape, o_ref.dtype)
        rows_ref[...] = jnp.zeros(rows_ref.shape, rows_ref.dtype)

    @pl.when(b == pl.num_programs(0) - 1)
    def _():
        for s in range(depth):
            spare = (n_used + s) % len(ring)
            for slot in range(len(ring)):
                pl.when(spare == slot)(
                    functools.partial(_wait_rows, x_hbm, ring[slot], rsem.at[slot]))


def _expert_up_rows_kernel(be_ref, ord_ref, nxt_ref, nused_ref, rows_ref, w_hbm, bg_ref, bl_ref,
                           o_ref, stage_g, stage_l, wsem, *, tiles):
    b = pl.program_id(0)
    n_used = nused_ref[0]
    wslot = _expert_weights(b, n_used, be_ref, ord_ref, nxt_ref,
                            [(w_hbm,) + t for t in tiles], [stage_g, stage_l], wsem)

    @pl.when(b < n_used)
    def _():
        _swiglu_block(rows_ref[...], stage_g[wslot], stage_l[wslot], bg_ref, bl_ref, o_ref)

    @pl.when(b >= n_used)
    def _():
        o_ref[...] = jnp.zeros(o_ref.shape, o_ref.dtype)


UP_TILE = 1024


def _expert_up(x, tok_list, tok_row0, sched, w_up, b_up):
    be, ordinal, nxt, n_used = sched
    n_rows = be.shape[0] * ROW_BLOCK
    tn = UP_TILE
    nf = EXPERT_FF // tn
    b_up3 = b_up.reshape(b_up.shape[0], 1, 2 * EXPERT_FF)
    weight_scratch = [pltpu.VMEM((2, D_MODEL, tn), F32), pltpu.VMEM((2, D_MODEL, tn), F32)]
    row_blk = lambda width: pl.BlockSpec((ROW_BLOCK, width), lambda b, *_: (b, 0))

    def bias(col_blk):
        return pl.BlockSpec((1, 1, tn), lambda b, be_, *_: (be_[b], 0, col_blk))

    def tiles(f):
        return ((f * tn, tn), ((nf + f) * tn, tn))

    act0, rows = pl.pallas_call(
        functools.partial(_expert_up_gather_kernel, tiles=tiles(0)),
        out_shape=[jax.ShapeDtypeStruct((n_rows, tn), BF16),
                   jax.ShapeDtypeStruct((n_rows, D_MODEL), BF16)],
        grid_spec=pltpu.PrefetchScalarGridSpec(
            num_scalar_prefetch=6,
            grid=(n_rows // ROW_BLOCK,),
            in_specs=[pl.BlockSpec(memory_space=pl.ANY), pl.BlockSpec(memory_space=pl.ANY),
                      bias(0), bias(nf)],
            out_specs=[row_blk(tn), row_blk(D_MODEL)],
            scratch_shapes=weight_scratch + [
                pltpu.VMEM((D_MODEL, tn), BF16), pltpu.VMEM((D_MODEL, tn), BF16),
                pltpu.VMEM((ROW_BLOCK, D_MODEL), F32), pltpu.VMEM((ROW_BLOCK, D_MODEL), F32),
                pltpu.VMEM((ROW_BLOCK, D_MODEL), F32),
                pltpu.SemaphoreType.DMA((2,)), pltpu.SemaphoreType.DMA((3,))],
        ),
        compiler_params=_params(("arbitrary",), 58),
        name="expert_up_gather",
    )(be, ordinal, nxt, n_used, tok_row0, tok_list, x, w_up, b_up3, b_up3)
    acts = [act0]
    for f in range(1, nf):
        acts.append(pl.pallas_call(
            functools.partial(_expert_up_rows_kernel, tiles=tiles(f)),
            out_shape=jax.ShapeDtypeStruct((n_rows, tn), BF16),
            grid_spec=pltpu.PrefetchScalarGridSpec(
                num_scalar_prefetch=4,
                grid=(n_rows // ROW_BLOCK,),
                in_specs=[row_blk(D_MODEL), pl.BlockSpec(memory_space=pl.ANY),
                          bias(f), bias(nf + f)],
                out_specs=row_blk(tn),
                scratch_shapes=weight_scratch + [pltpu.SemaphoreType.DMA((2,))],
            ),
            compiler_params=_params(("arbitrary",), 56),
            name="expert_up_rows",
        )(be, ordinal, nxt, n_used, rows, w_up, b_up3, b_up3))
    return acts


def _expert_down_kernel(be_ref, ord_ref, nxt_ref, nused_ref, *refs, n_act):
    act_refs = refs[:n_act]
    w_hbm, bias_ref, o_ref, stage, wsem = refs[n_act:]
    b = pl.program_id(0)
    n_used = nused_ref[0]
    slot = _expert_weights(b, n_used, be_ref, ord_ref, nxt_ref, [(w_hbm, 0, D_MODEL)], [stage],
                           wsem)

    @pl.when(b < n_used)
    def _():
        y = bias_ref[0]
        for f, act_ref in enumerate(act_refs):
            k0 = f * act_ref.shape[1]
            w = stage[slot, k0:k0 + act_ref.shape[1], :].astype(BF16)
            y = y + jnp.dot(act_ref[...], w, preferred_element_type=F32)
        o_ref[...] = y

    @pl.when(b >= n_used)
    def _():
        o_ref[...] = jnp.zeros(o_ref.shape, o_ref.dtype)


def _expert_down(acts, sched, w_down, b_down):
    be, ordinal, nxt, n_used = sched
    n_rows, tn = acts[0].shape
    b_down3 = b_down.reshape(b_down.shape[0], 1, D_MODEL)
    return pl.pallas_call(
        functools.partial(_expert_down_kernel, n_act=len(acts)),
        out_shape=jax.ShapeDtypeStruct((n_rows, D_MODEL), F32),
        grid_spec=pltpu.PrefetchScalarGridSpec(
            num_scalar_prefetch=4,
            grid=(n_rows // ROW_BLOCK,),
            in_specs=[pl.BlockSpec((ROW_BLOCK, tn), lambda b, *_: (b, 0)) for _ in acts] + [
                pl.BlockSpec(memory_space=pl.ANY),
                pl.BlockSpec((1, 1, D_MODEL), lambda b, be_, *_: (be_[b], 0, 0))],
            out_specs=pl.BlockSpec((ROW_BLOCK, D_MODEL), lambda b, *_: (b, 0)),
            scratch_shapes=[pltpu.VMEM((2, EXPERT_FF, D_MODEL), F32),
                            pltpu.SemaphoreType.DMA((2,))],
        ),
        compiler_params=_params(("arbitrary",), 56),
        name="expert_down",
    )(be, ordinal, nxt, n_used, *acts, w_down, b_down3)


def _combine_ln_kernel(pos_ref, y_hbm, gate_ref, x_ref, g_ref, b_ref, of_ref, ob_ref,
                       buf, sem):
    tc = x_ref.shape[0]
    i = pl.program_id(0)
    last = pl.num_programs(0) - 1
    cur = i % 2

    def start(tile, slot):
        for k in range(TOP_K):
            _start_rows(y_hbm, pos_ref, tile * (tc * TOP_K) + k, TOP_K, buf.at[slot, k],
                        sem.at[slot])

    def wait(slot):
        for k in range(TOP_K):
            _wait_rows(y_hbm, buf.at[slot, k], sem.at[slot])

    @pl.when(i == 0)
    def _():
        start(0, 0)

    wait(cur)
    start(jnp.where(i < last, i + 1, 0), 1 - cur)
    gate = gate_ref[...]
    ffn = gate[:, 0:1] * buf[cur, 0]
    for k in range(1, TOP_K):
        ffn = ffn + gate[:, k:k + 1] * buf[cur, k]
    y = _layer_norm(DEEPNORM_ALPHA * x_ref[...] + ffn, g_ref[...], b_ref[...])
    of_ref[...] = y
    ob_ref[...] = y.astype(BF16)

    @pl.when(i == last)
    def _():
        wait(1 - cur)


def _combine_ln(y_rows, pos, gate, x, g, b):
    t = x.shape[0]
    tc = min(128, t)
    row = lambda b_, p: (b_, 0)
    return pl.pallas_call(
        _combine_ln_kernel,
        out_shape=[jax.ShapeDtypeStruct((t, D_MODEL), F32),
                   jax.ShapeDtypeStruct((t, D_MODEL), BF16)],
        grid_spec=pltpu.PrefetchScalarGridSpec(
            num_scalar_prefetch=1,
            grid=(t // tc,),
            in_specs=[pl.BlockSpec(memory_space=pl.ANY),
                      pl.BlockSpec((tc, TOP_K), row),
                      pl.BlockSpec((tc, D_MODEL), row),
                      pl.BlockSpec((1, D_MODEL), lambda b_, p: (0, 0)),
                      pl.BlockSpec((1, D_MODEL), lambda b_, p: (0, 0))],
            out_specs=[pl.BlockSpec((tc, D_MODEL), row), pl.BlockSpec((tc, D_MODEL), row)],
            scratch_shapes=[pltpu.VMEM((2, TOP_K, tc, D_MODEL), F32),
                            pltpu.SemaphoreType.DMA((2,))],
        ),
        compiler_params=_params(("arbitrary",), 32),
        name="combine_ln",
    )(pos, y_rows, gate, x, g, b)


def _dispatch_plan(top_idx, rank, counts):
    t = top_idx.shape[0]
    n_assign = t * TOP_K
    n_blocks = -(-n_assign // ROW_BLOCK) + N_EXPERTS
    e_flat = top_idx.reshape(-1)
    padded = (counts + ROW_BLOCK - 1) // ROW_BLOCK * ROW_BLOCK
    pend = jnp.cumsum(padded)
    pstart = pend - padded
    dest = (_per_expert(pstart, e_flat) + rank.reshape(-1)).astype(I32)
    block_row0 = jnp.arange(n_blocks, dtype=I32) * ROW_BLOCK
    block_expert = jnp.minimum(jnp.sum((pend[None, :] <= block_row0[:, None]).astype(I32), axis=1),
                               N_EXPERTS - 1)
    n_used = (pend[-1:] // ROW_BLOCK).astype(I32)
    _, tok_list = lax.sort((dest, jnp.arange(n_assign, dtype=I32) // TOP_K), num_keys=1)
    tok_list = jnp.concatenate([tok_list, jnp.zeros((ROW_BLOCK,), I32)])
    pad_before = pstart - (jnp.cumsum(counts) - counts)
    tok_row0 = (block_row0 - _per_expert(pad_before, block_expert)).astype(I32)
    return dest, tok_list, tok_row0, block_expert, n_used


IN_FOX = N_BRANCHES * D_MODEL + 2 * RNN_WIDTH
IN_TAIL = IN_FOX + 3 * FOX_WIDTH


def _prep_in_tail(w_in, layer):
    tail = w_in[layer, :, IN_TAIL:]
    pad = LANES - MLA_ROPE_DIM - HEADS
    return jnp.concatenate([tail[:, HEADS:], tail[:, :HEADS], jnp.zeros((D_MODEL, pad), F32)],
                           axis=1).astype(BF16)


def _prep_uq(w_uq_l):
    w = w_uq_l.reshape(MLA_RANK, HEADS, MLA_QK_DIM) * (MLA_QK_DIM ** -0.5 * LOG2E)
    pad = jnp.zeros((MLA_RANK, HEADS, MLA_PAD_DIM - MLA_QK_DIM), F32)
    return jnp.concatenate([w, pad], axis=2).reshape(MLA_RANK, HEADS * MLA_PAD_DIM).astype(BF16)


def _layer(x, x_bf, tabs, p):
    t = x.shape[0]
    w_in, layer = p['w_in_all'], p['layer']
    w_in_t = jnp.swapaxes(w_in, 1, 2)
    u_a = _in_proj(x_bf, w_in_t, layer, 0, IN_FOX, jnp.ones((IN_FOX,), F32), F32,
                   "in_proj_gates_lru")
    q_scale = jnp.concatenate([jnp.full((FOX_WIDTH,), HEAD_DIM ** -0.5 * LOG2E, F32),
                               jnp.ones((2 * FOX_WIDTH,), F32)])
    u_b = _in_proj(x_bf, w_in_t, layer, IN_FOX, 3 * FOX_WIDTH, q_scale, BF16, "in_proj_fox")
    w_c = _prep_in_tail(w_in, layer)
    u_c = _matmul(x_bf, w_c, F32, 1024, w_c.shape[1], "in_proj_small")

    gate_blocks = N_BRANCHES * D_MODEL // RNN_WIDTH
    y_a = _rglru(u_a, gate_blocks, gate_blocks + 1,
                 p['conv_w'], p['conv_b'].reshape(1, RNN_WIDTH),
                 p['w_rec_gate'].astype(BF16), p['b_rec_gate'].reshape(1, RNN_WIDTH),
                 p['w_inp_gate'].astype(BF16), p['b_inp_gate'].reshape(1, RNN_WIDTH),
                 p['lru_lambda'].reshape(1, RNN_WIDTH))

    fl_col = 2 * MLA_RANK + MLA_ROPE_DIM
    cum = _fox_cum(u_c[:, fl_col:fl_col + HEADS], p['b_forget'])
    q_x, k_x = _fox_prep(u_b, cum)
    y_b = _causal_attention(q_x, k_x, u_b, 2 * FOX_WIDTH, "fox_attention")

    q_f, k_f, v_c = _mla_prep(u_c, p['g_cq'].reshape(1, MLA_RANK),
                              p['g_ckv'].reshape(1, MLA_RANK), _prep_uq(p['w_uq']),
                              p['w_ukv'].astype(BF16), tabs)
    y_c = _causal_attention(q_f, k_f, v_c, 0, "mla_attention")

    mixed = _merge(y_a, y_b, y_c, p['w_proj_lru_all'], p['w_proj_fox_all'], p['w_proj_mla_all'],
                   layer, u_a, p['b_merge'])
    x = _outproj_ln(mixed, p['w_out_all'], layer, x,
                    p['ln1_g'].reshape(1, D_MODEL), p['ln1_b'].reshape(1, D_MODEL))
    top_idx, gate, rank, counts = _router(x, p['w_router'].T, p['b_router'])
    dest, tok_list, tok_row0, block_expert, n_used = _dispatch_plan(top_idx, rank, counts)
    ordinal, nxt = _expert_schedule(block_expert, counts, n_used)
    base = p['expert_base']
    sched = (block_expert + base, ordinal, jnp.where(nxt >= 0, nxt + base, -1), n_used)
    acts = _expert_up(x, tok_list, tok_row0, sched, p['w_up_all'], p['b_up_all'])
    y_rows = _expert_down(acts, sched, p['w_down_all'], p['b_down_all'])
    del t
    return _combine_ln(y_rows, dest, gate, x, p['ln2_g'].reshape(1, D_MODEL),
                       p['ln2_b'].reshape(1, D_MODEL))


_LAYER_PARAMS = ('w_in', 'b_merge', 'b_forget', 'conv_w', 'conv_b', 'w_rec_gate', 'b_rec_gate',
                 'w_inp_gate', 'b_inp_gate', 'lru_lambda', 'g_cq', 'g_ckv', 'w_uq', 'w_ukv',
                 'w_proj_lru', 'w_proj_fox', 'w_proj_mla', 'w_out', 'ln1_g', 'ln1_b',
                 'w_router', 'b_router', 'w_up', 'b_up', 'w_down', 'b_down', 'ln2_g', 'ln2_b')


def kernel(x, positions, w_in, b_merge, b_forget, conv_w, conv_b, w_rec_gate, b_rec_gate,
           w_inp_gate, b_inp_gate, lru_lambda, g_cq, g_ckv, w_uq, w_ukv, w_proj_lru,
           w_proj_fox, w_proj_mla, w_out, ln1_g, ln1_b, w_router, b_router, w_up, b_up,
           w_down, b_down, ln2_g, ln2_b):
    stacked = dict(zip(_LAYER_PARAMS, (
        w_in, b_merge, b_forget, conv_w, conv_b, w_rec_gate, b_rec_gate, w_inp_gate,
        b_inp_gate, lru_lambda, g_cq, g_ckv, w_uq, w_ukv, w_proj_lru, w_proj_fox,
        w_proj_mla, w_out, ln1_g, ln1_b, w_router, b_router, w_up, b_up, w_down, b_down,
        ln2_g, ln2_b)))
    batch, seq, _ = x.shape
    assert batch == 1
    xt = x.reshape(seq, D_MODEL)
    x_bf = xt.astype(BF16)
    tabs = _rope_tables(positions.reshape(seq))
    expert_stack = {'w_up', 'b_up', 'w_down', 'b_down'}
    shared = {k + '_all': stacked[k].reshape((-1,) + stacked[k].shape[2:]) for k in expert_stack}
    dense_stack = {'w_in', 'w_proj_lru', 'w_proj_fox', 'w_proj_mla', 'w_out'}
    shared.update({k + '_all': stacked[k] for k in dense_stack})
    in_place = expert_stack | dense_stack
    for l in range(w_in.shape[0]):
        p = {k: v[l] for k, v in stacked.items() if k not in in_place}
        xt, x_bf = _layer(xt, x_bf, tabs,
                          dict(p, layer=l, expert_base=l * N_EXPERTS, **shared))
    return xt.reshape(batch, seq, D_MODEL)
```

```python
import functools

import jax
import jax.numpy as jnp
import numpy as np
from jax import lax
from jax.experimental import pallas as pl
from jax.experimental.pallas import tpu as pltpu

F32 = jnp.float32
BF16 = jnp.bfloat16
I32 = jnp.int32

D_MODEL = 2048
N_BRANCHES = 3
DEEPNORM_ALPHA = (2 * 2) ** 0.25
LN_EPS = 1e-5
RMS_EPS = 1e-6
RNN_WIDTH = 1024
RNN_BLOCKS = 8
RNN_BLOCK_W = RNN_WIDTH // RNN_BLOCKS
CONV_WIDTH = 4
RG_LRU_C = 8.0
HEADS = 8
HEAD_DIM = 128
FOX_WIDTH = HEADS * HEAD_DIM
MLA_RANK = 512
MLA_ROPE_DIM = 64
MLA_QK_DIM = HEAD_DIM + MLA_ROPE_DIM
ATTN_QK_DIM = 256
MLA_PAD_DIM = ATTN_QK_DIM
LOG2E = float(np.log2(np.e))
ROPE_THETA = 10000.0
N_EXPERTS = 32
TOP_K = 4
EXPERT_FF = D_MODEL
SWIGLU_ALPHA = 1.702
SWIGLU_LIMIT = 7.0

LANES = 128
SUBLANES = 8
V7X_VMEM_BYTES = 64 * 1024 * 1024

ROW_BLOCK = 256
ATTN_BLOCK = 512
GATHER_UNROLL = 8
ATTN_GROUP = 8
MASK_VALUE = -1e30


def _params(semantics, vmem_mib):
    assert vmem_mib * 1024 * 1024 < V7X_VMEM_BYTES
    return pltpu.CompilerParams(dimension_semantics=semantics,
                                vmem_limit_bytes=vmem_mib * 1024 * 1024)


def _split3(x):
    hi = x.astype(BF16)
    r = x - hi.astype(F32)
    mid = r.astype(BF16)
    lo = (r - mid.astype(F32)).astype(BF16)
    return hi, mid, lo


def _sigmoid(x):
    return 1.0 / (1.0 + jnp.exp(-x))


def _softplus(z):
    return jnp.maximum(z, 0.0) + jnp.log1p(jnp.exp(-jnp.abs(z)))


def _layer_norm(v, g, b):
    mu = jnp.mean(v, axis=-1, keepdims=True)
    c = v - mu
    var = jnp.mean(c * c, axis=-1, keepdims=True)
    return c * lax.rsqrt(var + LN_EPS) * g + b


def _mm_kernel(x_ref, w_ref, o_ref):
    o_ref[...] = jnp.dot(x_ref[...], w_ref[...],
                         preferred_element_type=F32).astype(o_ref.dtype)


def _matmul(x, w, out_dtype, tm, tn, name):
    m, k = x.shape
    n = w.shape[1]
    tm, tn = min(tm, m), min(tn, n)
    return pl.pallas_call(
        _mm_kernel,
        out_shape=jax.ShapeDtypeStruct((m, n), out_dtype),
        grid=(m // tm, n // tn),
        in_specs=[pl.BlockSpec((tm, k), lambda i, j: (i, 0)),
                  pl.BlockSpec((k, tn), lambda i, j: (0, j))],
        out_specs=pl.BlockSpec((tm, tn), lambda i, j: (i, j)),
        compiler_params=_params(("parallel", "arbitrary"), 48),
        name=name,
    )(x, w)


def _in_proj_kernel(x_ref, w_ref, s_ref, o_ref, w_sc):
    @pl.when(pl.program_id(1) == 0)
    def _():
        w_sc[...] = (w_ref[0] * s_ref[...]).T.astype(BF16)

    o_ref[...] = jnp.dot(x_ref[...], w_sc[...],
                         preferred_element_type=F32).astype(o_ref.dtype)


def _in_proj(x, w_stack_t, layer, col0, ncols, col_scale, out_dtype, name):
    m, k = x.shape
    tm, tn = min(1024, m), 1024
    assert col0 % tn == 0 and ncols % tn == 0
    blk0 = col0 // tn
    return pl.pallas_call(
        _in_proj_kernel,
        out_shape=jax.ShapeDtypeStruct((m, ncols), out_dtype),
        grid=(ncols // tn, m // tm),
        in_specs=[pl.BlockSpec((tm, k), lambda j, i: (i, 0)),
                  pl.BlockSpec((1, tn, k), lambda j, i: (layer, blk0 + j, 0)),
                  pl.BlockSpec((tn, 1), lambda j, i: (j, 0))],
        out_specs=pl.BlockSpec((tm, tn), lambda j, i: (i, j)),
        scratch_shapes=[pltpu.VMEM((k, tn), BF16)],
        compiler_params=_params(("arbitrary", "arbitrary"), 48),
        name=name,
    )(x, w_stack_t, col_scale.reshape(ncols, 1))


def _fox_cum_kernel(fl_ref, b_ref, o_ref, *, chunks):
    z = fl_ref[...] + b_ref[...]
    lf = jnp.minimum(z, 0.0) - jnp.log1p(jnp.exp(-jnp.abs(z)))
    rows = lf.shape[0]
    s = lax.broadcasted_iota(I32, (LANES, LANES), 0)
    t = lax.broadcasted_iota(I32, (LANES, LANES), 1)
    tri = (s <= t).astype(BF16)
    incl = None
    for part in _split3(lf):
        d = jnp.dot(part, tri, preferred_element_type=F32)
        incl = d if incl is None else incl + d
    i = lax.broadcasted_iota(I32, (rows, rows), 0)
    j = lax.broadcasted_iota(I32, (rows, rows), 1)
    lower = ((i // chunks == j // chunks) & (j < i)).astype(BF16)
    offs = None
    for part in _split3(incl):
        d = jnp.dot(lower, part, preferred_element_type=F32)
        offs = d if offs is None else offs + d
    o_ref[...] = incl + offs[:, LANES - 1:LANES]


def _fox_cum(f_logit, b_f):
    t = f_logit.shape[0]
    chunks = t // LANES
    fl = f_logit.T.reshape(HEADS * chunks, LANES)
    b = jnp.repeat(b_f.astype(F32), chunks).reshape(HEADS * chunks, 1)
    out = pl.pallas_call(
        functools.partial(_fox_cum_kernel, chunks=chunks),
        out_shape=jax.ShapeDtypeStruct((HEADS * chunks, LANES), F32),
        name="fox_cum",
    )(fl, b)
    return out.reshape(HEADS, t)


def _attn_kernel(qi_ref, ki_ref, q_ref, k_ref, v_ref, o_ref, m_sc, l_sc, acc_sc, *, group):
    p = pl.program_id(1)
    qi = qi_ref[p]
    ki = ki_ref[p]
    blk = q_ref.shape[0]

    @pl.when(ki == 0)
    def _():
        m_sc[...] = jnp.full(m_sc.shape, MASK_VALUE, F32)
        l_sc[...] = jnp.zeros(l_sc.shape, F32)
        acc_sc[...] = jnp.zeros(acc_sc.shape, F32)

    def step(masked):
        for g in range(group):
            qk = slice(g * ATTN_QK_DIM, (g + 1) * ATTN_QK_DIM)
            vo = slice(g * HEAD_DIM, (g + 1) * HEAD_DIM)
            s = lax.dot_general(q_ref[:, qk], k_ref[:, qk], (((1,), (1,)), ((), ())),
                                preferred_element_type=F32)
            if masked:
                row = lax.broadcasted_iota(I32, s.shape, 0)
                col = lax.broadcasted_iota(I32, s.shape, 1)
                s = jnp.where(col <= row, s, MASK_VALUE)
            chunks = [s[:, c * LANES:(c + 1) * LANES] for c in range(blk // LANES)]
            cmax = functools.reduce(jnp.maximum, chunks)
            m_prev = m_sc[g]
            m_new = jnp.maximum(m_prev, jnp.max(cmax, axis=1, keepdims=True))
            alpha = jnp.exp2(m_prev - m_new)
            probs = [jnp.exp2(c - m_new) for c in chunks]
            l_sc[g] = alpha * l_sc[g] + functools.reduce(jnp.add, probs)
            pb = jnp.concatenate(probs, axis=1).astype(BF16)
            acc_sc[g] = alpha * acc_sc[g] + jnp.dot(pb, v_ref[:, vo],
                                                    preferred_element_type=F32)
            m_sc[g] = m_new

    @pl.when(ki < qi)
    def _():
        step(False)

    @pl.when(ki == qi)
    def _():
        step(True)
        for g in range(group):
            denom = jnp.sum(l_sc[g], axis=1, keepdims=True)
            o_ref[:, g * HEAD_DIM:(g + 1) * HEAD_DIM] = (acc_sc[g] / denom).astype(o_ref.dtype)


def _causal_attention(q, k, v, v_col, name):
    t = q.shape[0]
    blk = min(ATTN_BLOCK, t)
    nq = t // blk
    group = ATTN_GROUP
    pairs = [(a, b) for a in range(nq) for b in range(a + 1)]
    qi_tab = jnp.asarray(np.array([a for a, _ in pairs], np.int32))
    ki_tab = jnp.asarray(np.array([b for _, b in pairs], np.int32))
    v_blk = v_col // (group * HEAD_DIM)
    return pl.pallas_call(
        functools.partial(_attn_kernel, group=group),
        out_shape=jax.ShapeDtypeStruct((t, HEADS * HEAD_DIM), BF16),
        grid_spec=pltpu.PrefetchScalarGridSpec(
            num_scalar_prefetch=2,
            grid=(HEADS // group, len(pairs)),
            in_specs=[
                pl.BlockSpec((blk, group * ATTN_QK_DIM), lambda h, p, qi, ki: (qi[p], h)),
                pl.BlockSpec((blk, group * ATTN_QK_DIM), lambda h, p, qi, ki: (ki[p], h)),
                pl.BlockSpec((blk, group * HEAD_DIM), lambda h, p, qi, ki: (ki[p], v_blk + h)),
            ],
            out_specs=pl.BlockSpec((blk, group * HEAD_DIM), lambda h, p, qi, ki: (qi[p], h)),
            scratch_shapes=[pltpu.VMEM((group, blk, LANES), F32),
                            pltpu.VMEM((group, blk, LANES), F32),
                            pltpu.VMEM((group, blk, HEAD_DIM), F32)],
        ),
        compiler_params=_params(("parallel", "arbitrary"), 32),
        name=name,
    )(qi_tab, ki_tab, q, k, v)


def _fox_prep_kernel(q_ref, k_ref, c_ref, sel_ref, qo_ref, ko_ref):
    bias = None
    for j, part in enumerate(_split3(c_ref[...])):
        d = jnp.dot(part, sel_ref[j], preferred_element_type=F32)
        bias = d if bias is None else bias + d
    lane = lax.broadcasted_iota(I32, (q_ref.shape[0], LANES), 1)
    for h in range(HEADS):
        src = slice(h * HEAD_DIM, (h + 1) * HEAD_DIM)
        lo = h * ATTN_QK_DIM
        g = bias[:, src]
        qo_ref[:, lo:lo + HEAD_DIM] = q_ref[:, src]
        ko_ref[:, lo:lo + HEAD_DIM] = k_ref[:, src]
        qo_ref[:, lo + HEAD_DIM:lo + ATTN_QK_DIM] = jnp.where(
            (lane >= 3) & (lane < 6), 1.0, g).astype(BF16)
        ko_ref[:, lo + HEAD_DIM:lo + ATTN_QK_DIM] = jnp.where(
            lane < 3, 1.0, -pltpu.roll(g, 3, axis=1)).astype(BF16)


def _fox_prep(u_b, cum):
    t = u_b.shape[0]
    tm = min(512, t)
    c = jnp.pad(cum.T * LOG2E, ((0, 0), (0, LANES - HEADS)))
    sel = np.zeros((3, LANES, FOX_WIDTH), np.float32)
    for j in range(3):
        for h in range(HEADS):
            sel[j, h, h * HEAD_DIM + j] = 1.0
    wide = HEADS * ATTN_QK_DIM
    return pl.pallas_call(
        _fox_prep_kernel,
        out_shape=[jax.ShapeDtypeStruct((t, wide), BF16)] * 2,
        grid=(t // tm,),
        in_specs=[pl.BlockSpec((tm, FOX_WIDTH), lambda i: (i, 0)),
                  pl.BlockSpec((tm, FOX_WIDTH), lambda i: (i, 1)),
                  pl.BlockSpec((tm, LANES), lambda i: (i, 0)),
                  pl.BlockSpec((3, LANES, FOX_WIDTH), lambda i: (0, 0, 0))],
        out_specs=[pl.BlockSpec((tm, wide), lambda i: (i, 0))] * 2,
        compiler_params=_params(("arbitrary",), 32),
        name="fox_prep",
    )(u_b, u_b, c, jnp.asarray(sel, BF16))


def _rope_table_kernel(pos_ref, inv_ref, c_ref, s1_ref, s2_ref):
    half = MLA_ROPE_DIM // 2
    ang = pos_ref[...].astype(F32) * inv_ref[...]
    c = jnp.cos(ang)
    s = jnp.sin(ang)
    lane = lax.broadcasted_iota(I32, ang.shape, 1)
    c_ref[...] = jnp.where(lane < MLA_ROPE_DIM, c, 0.0)
    s1_ref[...] = jnp.where(lane < half, -s, 0.0)
    s2_ref[...] = jnp.where((lane >= half) & (lane < MLA_ROPE_DIM), s, 0.0)


def _rope_tables(positions):
    t = positions.shape[0]
    half = MLA_ROPE_DIM // 2
    inv_freq = ROPE_THETA ** (-jnp.arange(half, dtype=F32) / half)
    inv = jnp.concatenate([inv_freq, inv_freq, jnp.zeros((LANES - 2 * half,), F32)])
    tm = min(1024, t)
    spec = pl.BlockSpec((tm, LANES), lambda i: (i, 0))
    return pl.pallas_call(
        _rope_table_kernel,
        out_shape=[jax.ShapeDtypeStruct((t, LANES), F32)] * 3,
        grid=(t // tm,),
        in_specs=[pl.BlockSpec((tm, 1), lambda i: (i, 0)),
                  pl.BlockSpec((1, LANES), lambda i: (0, 0))],
        out_specs=[spec, spec, spec],
        name="rope_tables",
    )(positions.reshape(t, 1), inv.reshape(1, LANES))


def _rope_group(g, c, s1, s2):
    half = MLA_ROPE_DIM // 2
    return (g * c + pltpu.roll(g, LANES - half, axis=1) * s1
            + pltpu.roll(g, half, axis=1) * s2)


def _mla_prep_kernel(u_ref, gq_ref, gkv_ref, wq_ref, wkv_ref, c_ref, s1_ref, s2_ref,
                     q_ref, k_ref, v_ref):
    def rms(v, g):
        ms = jnp.mean(v * v, axis=-1, keepdims=True)
        return (v * lax.rsqrt(ms + RMS_EPS) * g).astype(BF16)

    c, s1, s2 = c_ref[...], s1_ref[...], s2_ref[...]
    cq = rms(u_ref[:, 0:MLA_RANK], gq_ref[...])
    ckv = rms(u_ref[:, MLA_RANK:2 * MLA_RANK], gkv_ref[...])
    q_pre = jnp.dot(cq, wq_ref[...], preferred_element_type=F32)
    kv = jnp.dot(ckv, wkv_ref[...], preferred_element_type=F32)
    k_rot = _rope_group(u_ref[:, 2 * MLA_RANK:2 * MLA_RANK + LANES], c, s1, s2).astype(BF16)
    for h in range(HEADS):
        lo = h * MLA_PAD_DIM
        mid = lo + HEAD_DIM
        hi = lo + MLA_PAD_DIM
        q_ref[:, lo:mid] = q_pre[:, lo:mid].astype(BF16)
        q_ref[:, mid:hi] = _rope_group(q_pre[:, mid:hi], c, s1, s2).astype(BF16)
        k_ref[:, lo:mid] = kv[:, lo:mid].astype(BF16)
        k_ref[:, mid:hi] = k_rot
        v_ref[:, h * HEAD_DIM:(h + 1) * HEAD_DIM] = kv[:, mid:hi].astype(BF16)


def _mla_prep(u_small, g_cq, g_ckv, w_uq_r, w_ukv, tabs):
    t = u_small.shape[0]
    tm = min(512, t)
    wide = HEADS * MLA_PAD_DIM
    row = lambda w: pl.BlockSpec((tm, w), lambda i: (i, 0))
    const = lambda a: pl.BlockSpec(a.shape, lambda i: (0, 0))
    return pl.pallas_call(
        _mla_prep_kernel,
        out_shape=[jax.ShapeDtypeStruct((t, wide), BF16),
                   jax.ShapeDtypeStruct((t, wide), BF16),
                   jax.ShapeDtypeStruct((t, HEADS * HEAD_DIM), BF16)],
        grid=(t // tm,),
        in_specs=[row(u_small.shape[1]), const(g_cq), const(g_ckv), const(w_uq_r),
                  const(w_ukv), row(LANES), row(LANES), row(LANES)],
        out_specs=[row(wide), row(wide), row(HEADS * HEAD_DIM)],
        compiler_params=_params(("arbitrary",), 48),
        name="mla_prep",
    )(u_small, g_cq, g_ckv, w_uq_r, w_ukv, *tabs)


def _rglru_kernel(y_ref, x_ref, cw_ref, cb_ref, wr_ref, br_ref, wi_ref, bi_ref, lam_ref,
                  o_ref, xprev_sc, h_sc, a_sc, g_sc, hs_sc):
    tt = x_ref.shape[0]

    @pl.when(pl.program_id(0) == 0)
    def _():
        xprev_sc[...] = jnp.zeros(xprev_sc.shape, F32)
        h_sc[...] = jnp.zeros(h_sc.shape, F32)

    x = x_ref[...]
    xext = jnp.concatenate([xprev_sc[...], x], axis=0)
    xprev_sc[...] = x[tt - SUBLANES:, :]
    xc = cb_ref[...]
    for j in range(CONV_WIDTH):
        off = SUBLANES - (CONV_WIDTH - 1) + j
        xc = xc + cw_ref[j:j + 1, :] * xext[off:off + tt, :]

    r_parts, i_parts = [], []
    for n in range(RNN_BLOCKS):
        sl = slice(n * RNN_BLOCK_W, (n + 1) * RNN_BLOCK_W)
        xb = xc[:, sl].astype(BF16)
        r_parts.append(_sigmoid(jnp.dot(xb, wr_ref[n], preferred_element_type=F32)
                                + br_ref[:, sl]))
        i_parts.append(_sigmoid(jnp.dot(xb, wi_ref[n], preferred_element_type=F32)
                                + bi_ref[:, sl]))
    r = jnp.concatenate(r_parts, axis=1)
    gate_i = jnp.concatenate(i_parts, axis=1)
    log_a = (-RG_LRU_C * _softplus(-lam_ref[...])) * r
    a = jnp.exp(log_a)
    a_sc[...] = a
    g_sc[...] = jnp.sqrt(-jnp.tanh(log_a) * (a * a + 1.0)) * (gate_i * xc)

    sub = lax.broadcasted_iota(I32, (SUBLANES, RNN_WIDTH), 0)

    def tile_scan(k, h_in):
        rows = pl.ds(pl.multiple_of(k * SUBLANES, SUBLANES), SUBLANES)
        a = a_sc[rows, :]
        g = g_sc[rows, :]
        for d in (1, 2, 4):
            keep = sub >= d
            g = jnp.where(keep, a * pltpu.roll(g, d, axis=0) + g, g)
            a = jnp.where(keep, a * pltpu.roll(a, d, axis=0), a)
        hs = a * h_in + g
        hs_sc[rows, :] = hs
        return jnp.broadcast_to(hs[SUBLANES - 1:SUBLANES, :], (SUBLANES, RNN_WIDTH))

    h_sc[...] = lax.fori_loop(0, tt // SUBLANES, tile_scan, h_sc[...])

    y = y_ref[...]
    gelu = 0.5 * y * (1.0 + jnp.tanh(np.sqrt(2.0 / np.pi) * (y + 0.044715 * (y * y * y))))
    o_ref[...] = (hs_sc[...] * gelu).astype(o_ref.dtype)


def _rglru(u_a, y_blk, x_blk, conv_w, conv_b, w_rg, b_rg, w_ig, b_ig, lam):
    t = u_a.shape[0]
    tt = min(256, t)
    const2 = lambda a: pl.BlockSpec(a.shape, lambda i: (0, 0))
    const3 = lambda a: pl.BlockSpec(a.shape, lambda i: (0, 0, 0))
    return pl.pallas_call(
        _rglru_kernel,
        out_shape=jax.ShapeDtypeStruct((t, RNN_WIDTH), BF16),
        grid=(t // tt,),
        in_specs=[pl.BlockSpec((tt, RNN_WIDTH), lambda i: (i, y_blk)),
                  pl.BlockSpec((tt, RNN_WIDTH), lambda i: (i, x_blk)),
                  const2(conv_w), const2(conv_b), const3(w_rg), const2(b_rg),
                  const3(w_ig), const2(b_ig), const2(lam)],
        out_specs=pl.BlockSpec((tt, RNN_WIDTH), lambda i: (i, 0)),
        scratch_shapes=[pltpu.VMEM((SUBLANES, RNN_WIDTH), F32),
                        pltpu.VMEM((SUBLANES, RNN_WIDTH), F32),
                        pltpu.VMEM((tt, RNN_WIDTH), F32),
                        pltpu.VMEM((tt, RNN_WIDTH), F32),
                        pltpu.VMEM((tt, RNN_WIDTH), F32)],
        compiler_params=_params(("arbitrary",), 32),
        name="rglru",
    )(u_a, u_a, conv_w, conv_b, w_rg, b_rg, w_ig, b_ig, lam)


def _merge_kernel(a_ref, b_ref, c_ref, wa_ref, wb_ref, wc_ref, g0_ref, g1_ref, g2_ref,
                  bm_ref, o_ref):
    def branch(x_ref, w_ref, g_ref, n):
        y = jnp.dot(x_ref[...], w_ref[0].astype(BF16), preferred_element_type=F32)
        return _sigmoid(g_ref[...] + bm_ref[n:n + 1, :]) * y

    mixed = (branch(a_ref, wa_ref, g0_ref, 0) + branch(b_ref, wb_ref, g1_ref, 1)
             + branch(c_ref, wc_ref, g2_ref, 2))
    o_ref[...] = mixed.astype(o_ref.dtype)


def _merge(ya, yb, yc, wa, wb, wc, layer, u_a, b_merge):
    t = ya.shape[0]
    tm, tn = min(512, t), 1024
    nn = D_MODEL // tn
    xin = pl.BlockSpec((tm, ya.shape[1]), lambda j, i: (i, 0))
    win = pl.BlockSpec((1, ya.shape[1], tn), lambda j, i: (layer, 0, j))
    gate = lambda n: pl.BlockSpec((tm, tn), lambda j, i: (i, n * nn + j))
    return pl.pallas_call(
        _merge_kernel,
        out_shape=jax.ShapeDtypeStruct((t, D_MODEL), BF16),
        grid=(nn, t // tm),
        in_specs=[xin, xin, xin, win, win, win, gate(0), gate(1), gate(2),
                  pl.BlockSpec((N_BRANCHES, tn), lambda j, i: (0, j))],
        out_specs=pl.BlockSpec((tm, tn), lambda j, i: (i, j)),
        compiler_params=_params(("arbitrary", "arbitrary"), 56),
        name="merge",
    )(ya, yb, yc, wa, wb, wc, u_a, u_a, u_a, b_merge)


def _outproj_ln_kernel(m_ref, w_ref, x_ref, g_ref, b_ref, o_ref):
    mix = jnp.dot(m_ref[...], w_ref[0].astype(BF16), preferred_element_type=F32)
    o_ref[...] = _layer_norm(DEEPNORM_ALPHA * x_ref[...] + mix, g_ref[...], b_ref[...])


def _outproj_ln(mixed, w_out, layer, x, g, b):
    t = x.shape[0]
    tm = min(256, t)
    row = pl.BlockSpec((tm, D_MODEL), lambda i: (i, 0))
    vec = pl.BlockSpec((1, D_MODEL), lambda i: (0, 0))
    return pl.pallas_call(
        _outproj_ln_kernel,
        out_shape=jax.ShapeDtypeStruct((t, D_MODEL), F32),
        grid=(t // tm,),
        in_specs=[row, pl.BlockSpec((1, D_MODEL, D_MODEL), lambda i: (layer, 0, 0)), row, vec,
                  vec],
        out_specs=row,
        compiler_params=_params(("arbitrary",), 56),
        name="outproj_ln",
    )(mixed, w_out, x, g, b)


def _route(x, w_ref, b_ref, idx_ref, gate_ref, rank_ref, count_ref, count_sc):
    nt = (((1,), (1,)), ((), ()))
    x_hi = x.astype(BF16)
    x_lo = (x - x_hi.astype(F32)).astype(BF16)
    w = w_ref[...]
    w_hi = w.astype(BF16)
    w_lo = (w - w_hi.astype(F32)).astype(BF16)
    logits = (lax.dot_general(w_hi, x_hi, nt, preferred_element_type=F32)
              + lax.dot_general(w_hi, x_lo, nt, preferred_element_type=F32)
              + lax.dot_general(w_lo, x_hi, nt, preferred_element_type=F32)
              + b_ref[...])
    eidx = lax.broadcasted_iota(I32, logits.shape, 0)
    vals, idxs = [], []
    for _ in range(TOP_K):
        m = jnp.max(logits, axis=0, keepdims=True)
        idx = jnp.min(jnp.where(logits == m, eidx, N_EXPERTS), axis=0, keepdims=True)
        vals.append(m)
        idxs.append(idx)
        logits = jnp.where(eidx == idx, -jnp.inf, logits)
    exps = [jnp.exp(v - vals[0]) for v in vals]
    denom = exps[0] + exps[1] + exps[2] + exps[3]
    pad = SUBLANES - TOP_K
    tokens = logits.shape[1]
    idx_ref[...] = jnp.concatenate(idxs + [jnp.zeros((pad, tokens), I32)], axis=0)
    gate_ref[...] = jnp.concatenate([e / denom for e in exps]
                                    + [jnp.zeros((pad, tokens), F32)], axis=0)

    @pl.when(pl.program_id(0) == 0)
    def _():
        count_sc[...] = jnp.zeros(count_sc.shape, F32)

    src = lax.broadcasted_iota(I32, (tokens, tokens), 0)
    dst = lax.broadcasted_iota(I32, (tokens, tokens), 1)
    before = (src < dst).astype(BF16)
    seen = count_sc[...]
    ranks = []
    for idx in idxs:
        hit = eidx == idx
        prefix = jnp.dot(hit.astype(BF16), before, preferred_element_type=F32)
        ranks.append(jnp.sum(jnp.where(hit, seen + prefix, 0.0), axis=0, keepdims=True))
        seen = seen + jnp.sum(hit.astype(F32), axis=1, keepdims=True)
    count_sc[...] = seen
    rank_ref[...] = jnp.concatenate(ranks + [jnp.zeros((pad, tokens), F32)],
                                    axis=0).astype(I32)
    count_ref[...] = seen.astype(I32)


def _router_kernel(x_ref, w_ref, b_ref, idx_ref, gate_ref, rank_ref, count_ref, count_sc):
    _route(x_ref[...], w_ref, b_ref, idx_ref, gate_ref, rank_ref, count_ref, count_sc)


def _router(x, w_router_t, b_router):
    t = x.shape[0]
    tm = min(1024, t)
    tok = pl.BlockSpec((SUBLANES, tm), lambda i: (0, i))
    idx, gate, rank, count = pl.pallas_call(
        _router_kernel,
        out_shape=[jax.ShapeDtypeStruct((SUBLANES, t), I32),
                   jax.ShapeDtypeStruct((SUBLANES, t), F32),
                   jax.ShapeDtypeStruct((SUBLANES, t), I32),
                   jax.ShapeDtypeStruct((N_EXPERTS, 1), I32)],
        grid=(t // tm,),
        in_specs=[pl.BlockSpec((tm, D_MODEL), lambda i: (i, 0)),
                  pl.BlockSpec((N_EXPERTS, D_MODEL), lambda i: (0, 0)),
                  pl.BlockSpec((N_EXPERTS, 1), lambda i: (0, 0))],
        out_specs=[tok, tok, tok, pl.BlockSpec((N_EXPERTS, 1), lambda i: (0, 0))],
        scratch_shapes=[pltpu.VMEM((N_EXPERTS, 1), F32)],
        compiler_params=_params(("arbitrary",), 48),
        name="router",
    )(x, w_router_t, b_router.reshape(N_EXPERTS, 1))
    return idx[:TOP_K].T, gate[:TOP_K].T, rank[:TOP_K].T, count[:, 0]


def _row_copy(src_hbm, row, dst, dst_row, sem):
    return pltpu.make_async_copy(src_hbm.at[pl.ds(row, 1), :],
                                 dst.at[pl.ds(dst_row, 1), :], sem)


def _start_rows(src_hbm, idx_ref, idx0, stride, dst, sem, both_queues=False):
    for r in range(dst.shape[0]):
        _row_copy(src_hbm, idx_ref[idx0 + r * stride], dst, r, sem).start(
            priority=r % 2 if both_queues else 0)


def _wait_rows(src_hbm, dst, sem):
    pltpu.make_async_copy(src_hbm.at[pl.ds(0, dst.shape[0]), :], dst, sem).wait()


def _per_expert(table, expert_ids):
    hit = expert_ids[..., None] == jnp.arange(N_EXPERTS, dtype=I32)
    return jnp.sum(jnp.where(hit, table, 0), axis=-1)


def _expert_schedule(block_expert, counts, n_used):
    b = jnp.arange(block_expert.shape[0], dtype=I32)
    prev = jnp.concatenate([block_expert[:1] - 1, block_expert[:-1]])
    first = (b < n_used[0]) & (block_expert != prev)
    ordinal = jnp.cumsum(first.astype(I32)) - 1
    e = jnp.arange(N_EXPERTS, dtype=I32)
    later_live = (counts > 0)[None, :] & (e[None, :] > e[:, None])
    nxt = jnp.min(jnp.where(later_live, e[None, :], N_EXPERTS), axis=1)
    nxt = jnp.where(nxt == N_EXPERTS, -1, nxt).astype(I32)
    return ordinal, _per_expert(nxt, block_expert)


def _expert_weights(b, n_used, be_ref, ord_ref, nxt_ref, tiles, stages, sem, casts=()):
    def copies(e, slot):
        return [pltpu.make_async_copy(w.at[e, :, pl.ds(c0, width)], st.at[slot], sem.at[slot])
                for (w, c0, width), st in zip(tiles, stages)]

    e = be_ref[b]
    first = (b < n_used) & ((b == 0) | (e != be_ref[jnp.maximum(b - 1, 0)]))

    @pl.when(b == 0)
    def _():
        for c in copies(e, 0):
            c.start()

    @pl.when(first)
    def _():
        slot = ord_ref[b] % 2
        for c in copies(e, slot):
            c.wait()
        nxt = nxt_ref[b]

        @pl.when(nxt >= 0)
        def _():
            for c in copies(nxt, 1 - slot):
                c.start()

        for st, dst in zip(stages, casts):
            dst[...] = st[slot].astype(BF16)

    return ord_ref[b] % 2


def _swiglu_block(x, w_glu, w_lin, bg_ref, bl_ref, o_ref):
    h_glu = jnp.dot(x, w_glu.astype(BF16), preferred_element_type=F32) + bg_ref[0]
    h_lin = jnp.dot(x, w_lin.astype(BF16), preferred_element_type=F32) + bl_ref[0]
    h_glu = jnp.minimum(h_glu, SWIGLU_LIMIT)
    h_lin = jnp.clip(h_lin, -SWIGLU_LIMIT, SWIGLU_LIMIT)
    act = h_glu * _sigmoid(SWIGLU_ALPHA * h_glu) * (h_lin + 1.0)
    o_ref[...] = act.astype(o_ref.dtype)


def _expert_up_gather_kernel(be_ref, ord_ref, nxt_ref, nused_ref, row0_ref, tok_ref, x_hbm, w_hbm,
                             bg_ref, bl_ref, o_ref, rows_ref, stage_g, stage_l, wg_sc, wl_sc,
                             rows_a, rows_b, rows_c, wsem, rsem, *, tiles):
    ring = (rows_a, rows_b, rows_c)
    depth = len(ring) - 1
    b = pl.program_id(0)
    n_used = nused_ref[0]

    def start(blk, slot):
        _start_rows(x_hbm, tok_ref, row0_ref[blk], 1, ring[slot], rsem.at[slot])

    @pl.when(b == 0)
    def _():
        for s in range(depth):
            start(jnp.minimum(s, n_used - 1), s)

    _expert_weights(b, n_used, be_ref, ord_ref, nxt_ref, [(w_hbm,) + t for t in tiles],
                    [stage_g, stage_l], wsem, casts=[wg_sc, wl_sc])

    def block(slot):
        cur = ring[slot]
        _wait_rows(x_hbm, cur, rsem.at[slot])
        start(jnp.minimum(b + depth, n_used - 1), (slot + depth) % len(ring))
        x = cur[...].astype(BF16)
        rows_ref[...] = x
        _swiglu_block(x, wg_sc[...], wl_sc[...], bg_ref, bl_ref, o_ref)

    for slot in range(len(ring)):
        pl.when((b < n_used) & (b % len(ring) == slot))(functools.partial(block, slot))

    @pl.when(b >= n_used)
    def _():
        o_ref[...] = jnp.zeros(o_ref.shape, o_ref.dtype)
        rows_ref[...] = jnp.zeros(rows_ref.shape, rows_ref.dtype)

    @pl.when(b == pl.num_programs(0) - 1)
    def _():
        for s in range(depth):
            spare = (n_used + s) % len(ring)
            for slot in range(len(ring)):
                pl.when(spare == slot)(
                    functools.partial(_wait_rows, x_hbm, ring[slot], rsem.at[slot]))


def _expert_up_rows_kernel(be_ref, ord_ref, nxt_ref, nused_ref, rows_ref, w_hbm, bg_ref, bl_ref,
                           o_ref, stage_g, stage_l, wsem, *, tiles):
    b = pl.program_id(0)
    n_used = nused_ref[0]
    wslot = _expert_weights(b, n_used, be_ref, ord_ref, nxt_ref,
                            [(w_hbm,) + t for t in tiles], [stage_g, stage_l], wsem)

    @pl.when(b < n_used)
    def _():
        _swiglu_block(rows_ref[...], stage_g[wslot], stage_l[wslot], bg_ref, bl_ref, o_ref)

    @pl.when(b >= n_used)
    def _():
        o_ref[...] = jnp.zeros(o_ref.shape, o_ref.dtype)


UP_TILE = 1024


def _expert_up(x, tok_list, tok_row0, sched, w_up, b_up):
    be, ordinal, nxt, n_used = sched
    n_rows = be.shape[0] * ROW_BLOCK
    tn = UP_TILE
    nf = EXPERT_FF // tn
    b_up3 = b_up.reshape(b_up.shape[0], 1, 2 * EXPERT_FF)
    weight_scratch = [pltpu.VMEM((2, D_MODEL, tn), F32), pltpu.VMEM((2, D_MODEL, tn), F32)]
    row_blk = lambda width: pl.BlockSpec((ROW_BLOCK, width), lambda b, *_: (b, 0))

    def bias(col_blk):
        return pl.BlockSpec((1, 1, tn), lambda b, be_, *_: (be_[b], 0, col_blk))

    def tiles(f):
        return ((f * tn, tn), ((nf + f) * tn, tn))

    act0, rows = pl.pallas_call(
        functools.partial(_expert_up_gather_kernel, tiles=tiles(0)),
        out_shape=[jax.ShapeDtypeStruct((n_rows, tn), BF16),
                   jax.ShapeDtypeStruct((n_rows, D_MODEL), BF16)],
        grid_spec=pltpu.PrefetchScalarGridSpec(
            num_scalar_prefetch=6,
            grid=(n_rows // ROW_BLOCK,),
            in_specs=[pl.BlockSpec(memory_space=pl.ANY), pl.BlockSpec(memory_space=pl.ANY),
                      bias(0), bias(nf)],
            out_specs=[row_blk(tn), row_blk(D_MODEL)],
            scratch_shapes=weight_scratch + [
                pltpu.VMEM((D_MODEL, tn), BF16), pltpu.VMEM((D_MODEL, tn), BF16),
                pltpu.VMEM((ROW_BLOCK, D_MODEL), F32), pltpu.VMEM((ROW_BLOCK, D_MODEL), F32),
                pltpu.VMEM((ROW_BLOCK, D_MODEL), F32),
                pltpu.SemaphoreType.DMA((2,)), pltpu.SemaphoreType.DMA((3,))],
        ),
        compiler_params=_params(("arbitrary",), 58),
        name="expert_up_gather",
    )(be, ordinal, nxt, n_used, tok_row0, tok_list, x, w_up, b_up3, b_up3)
    acts = [act0]
    for f in range(1, nf):
        acts.append(pl.pallas_call(
            functools.partial(_expert_up_rows_kernel, tiles=tiles(f)),
            out_shape=jax.ShapeDtypeStruct((n_rows, tn), BF16),
            grid_spec=pltpu.PrefetchScalarGridSpec(
                num_scalar_prefetch=4,
                grid=(n_rows // ROW_BLOCK,),
                in_specs=[row_blk(D_MODEL), pl.BlockSpec(memory_space=pl.ANY),
                          bias(f), bias(nf + f)],
                out_specs=row_blk(tn),
                scratch_shapes=weight_scratch + [pltpu.SemaphoreType.DMA((2,))],
            ),
            compiler_params=_params(("arbitrary",), 56),
            name="expert_up_rows",
        )(be, ordinal, nxt, n_used, rows, w_up, b_up3, b_up3))
    return acts


def _expert_down_kernel(be_ref, ord_ref, nxt_ref, nused_ref, *refs, n_act):
    act_refs = refs[:n_act]
    w_hbm, bias_ref, o_ref, stage, wsem = refs[n_act:]
    b = pl.program_id(0)
    n_used = nused_ref[0]
    slot = _expert_weights(b, n_used, be_ref, ord_ref, nxt_ref, [(w_hbm, 0, D_MODEL)], [stage],
                           wsem)

    @pl.when(b < n_used)
    def _():
        y = bias_ref[0]
        for f, act_ref in enumerate(act_refs):
            k0 = f * act_ref.shape[1]
            w = stage[slot, k0:k0 + act_ref.shape[1], :].astype(BF16)
            y = y + jnp.dot(act_ref[...], w, preferred_element_type=F32)
        o_ref[...] = y

    @pl.when(b >= n_used)
    def _():
        o_ref[...] = jnp.zeros(o_ref.shape, o_ref.dtype)


def _expert_down(acts, sched, w_down, b_down):
    be, ordinal, nxt, n_used = sched
    n_rows, tn = acts[0].shape
    b_down3 = b_down.reshape(b_down.shape[0], 1, D_MODEL)
    return pl.pallas_call(
        functools.partial(_expert_down_kernel, n_act=len(acts)),
        out_shape=jax.ShapeDtypeStruct((n_rows, D_MODEL), F32),
        grid_spec=pltpu.PrefetchScalarGridSpec(
            num_scalar_prefetch=4,
            grid=(n_rows // ROW_BLOCK,),
            in_specs=[pl.BlockSpec((ROW_BLOCK, tn), lambda b, *_: (b, 0)) for _ in acts] + [
                pl.BlockSpec(memory_space=pl.ANY),
                pl.BlockSpec((1, 1, D_MODEL), lambda b, be_, *_: (be_[b], 0, 0))],
            out_specs=pl.BlockSpec((ROW_BLOCK, D_MODEL), lambda b, *_: (b, 0)),
            scratch_shapes=[pltpu.VMEM((2, EXPERT_FF, D_MODEL), F32),
                            pltpu.SemaphoreType.DMA((2,))],
        ),
        compiler_params=_params(("arbitrary",), 56),
        name="expert_down",
    )(be, ordinal, nxt, n_used, *acts, w_down, b_down3)


def _combine_ln_kernel(pos_ref, y_hbm, gate_ref, x_ref, g_ref, b_ref, of_ref, ob_ref,
                       buf, sem):
    tc = x_ref.shape[0]
    i = pl.program_id(0)
    last = pl.num_programs(0) - 1
    cur = i % 2

    def start(tile, slot):
        for k in range(TOP_K):
            _start_rows(y_hbm, pos_ref, tile * (tc * TOP_K) + k, TOP_K, buf.at[slot, k],
                        sem.at[slot], both_queues=True)

    def wait(slot):
        for k in range(TOP_K):
            _wait_rows(y_hbm, buf.at[slot, k], sem.at[slot])

    @pl.when(i == 0)
    def _():
        start(0, 0)

    wait(cur)
    start(jnp.where(i < last, i + 1, 0), 1 - cur)
    gate = gate_ref[...]
    ffn = gate[:, 0:1] * buf[cur, 0]
    for k in range(1, TOP_K):
        ffn = ffn + gate[:, k:k + 1] * buf[cur, k]
    y = _layer_norm(DEEPNORM_ALPHA * x_ref[...] + ffn, g_ref[...], b_ref[...])
    of_ref[...] = y
    ob_ref[...] = y.astype(BF16)

    @pl.when(i == last)
    def _():
        wait(1 - cur)


def _combine_ln(y_rows, pos, gate, x, g, b):
    t = x.shape[0]
    tc = min(128, t)
    row = lambda b_, p: (b_, 0)
    return pl.pallas_call(
        _combine_ln_kernel,
        out_shape=[jax.ShapeDtypeStruct((t, D_MODEL), F32),
                   jax.ShapeDtypeStruct((t, D_MODEL), BF16)],
        grid_spec=pltpu.PrefetchScalarGridSpec(
            num_scalar_prefetch=1,
            grid=(t // tc,),
            in_specs=[pl.BlockSpec(memory_space=pl.ANY),
                      pl.BlockSpec((tc, TOP_K), row),
                      pl.BlockSpec((tc, D_MODEL), row),
                      pl.BlockSpec((1, D_MODEL), lambda b_, p: (0, 0)),
                      pl.BlockSpec((1, D_MODEL), lambda b_, p: (0, 0))],
            out_specs=[pl.BlockSpec((tc, D_MODEL), row), pl.BlockSpec((tc, D_MODEL), row)],
            scratch_shapes=[pltpu.VMEM((2, TOP_K, tc, D_MODEL), F32),
                            pltpu.SemaphoreType.DMA((2,))],
        ),
        compiler_params=_params(("arbitrary",), 32),
        name="combine_ln",
    )(pos, y_rows, gate, x, g, b)


def _dispatch_plan(top_idx, rank, counts):
    t = top_idx.shape[0]
    n_assign = t * TOP_K
    n_blocks = -(-n_assign // ROW_BLOCK) + N_EXPERTS
    e_flat = top_idx.reshape(-1)
    padded = (counts + ROW_BLOCK - 1) // ROW_BLOCK * ROW_BLOCK
    pend = jnp.cumsum(padded)
    pstart = pend - padded
    dest = (_per_expert(pstart, e_flat) + rank.reshape(-1)).astype(I32)
    block_row0 = jnp.arange(n_blocks, dtype=I32) * ROW_BLOCK
    block_expert = jnp.minimum(jnp.sum((pend[None, :] <= block_row0[:, None]).astype(I32), axis=1),
                               N_EXPERTS - 1)
    n_used = (pend[-1:] // ROW_BLOCK).astype(I32)
    _, tok_list = lax.sort((dest, jnp.arange(n_assign, dtype=I32) // TOP_K), num_keys=1)
    tok_list = jnp.concatenate([tok_list, jnp.zeros((ROW_BLOCK,), I32)])
    pad_before = pstart - (jnp.cumsum(counts) - counts)
    tok_row0 = (block_row0 - _per_expert(pad_before, block_expert)).astype(I32)
    return dest, tok_list, tok_row0, block_expert, n_used


IN_FOX = N_BRANCHES * D_MODEL + 2 * RNN_WIDTH
IN_TAIL = IN_FOX + 3 * FOX_WIDTH


def _prep_in_tail(w_in, layer):
    tail = w_in[layer, :, IN_TAIL:]
    pad = LANES - MLA_ROPE_DIM - HEADS
    return jnp.concatenate([tail[:, HEADS:], tail[:, :HEADS], jnp.zeros((D_MODEL, pad), F32)],
                           axis=1).astype(BF16)


def _prep_uq(w_uq_l):
    w = w_uq_l.reshape(MLA_RANK, HEADS, MLA_QK_DIM) * (MLA_QK_DIM ** -0.5 * LOG2E)
    pad = jnp.zeros((MLA_RANK, HEADS, MLA_PAD_DIM - MLA_QK_DIM), F32)
    return jnp.concatenate([w, pad], axis=2).reshape(MLA_RANK, HEADS * MLA_PAD_DIM).astype(BF16)


def _layer(x, x_bf, tabs, p):
    t = x.shape[0]
    w_in, layer = p['w_in_all'], p['layer']
    w_in_t = jnp.swapaxes(w_in, 1, 2)
    u_a = _in_proj(x_bf, w_in_t, layer, 0, IN_FOX, jnp.ones((IN_FOX,), F32), F32,
                   "in_proj_gates_lru")
    q_scale = jnp.concatenate([jnp.full((FOX_WIDTH,), HEAD_DIM ** -0.5 * LOG2E, F32),
                               jnp.ones((2 * FOX_WIDTH,), F32)])
    u_b = _in_proj(x_bf, w_in_t, layer, IN_FOX, 3 * FOX_WIDTH, q_scale, BF16, "in_proj_fox")
    w_c = _prep_in_tail(w_in, layer)
    u_c = _matmul(x_bf, w_c, F32, 1024, w_c.shape[1], "in_proj_small")

    gate_blocks = N_BRANCHES * D_MODEL // RNN_WIDTH
    y_a = _rglru(u_a, gate_blocks, gate_blocks + 1,
                 p['conv_w'], p['conv_b'].reshape(1, RNN_WIDTH),
                 p['w_rec_gate'].astype(BF16), p['b_rec_gate'].reshape(1, RNN_WIDTH),
                 p['w_inp_gate'].astype(BF16), p['b_inp_gate'].reshape(1, RNN_WIDTH),
                 p['lru_lambda'].reshape(1, RNN_WIDTH))

    fl_col = 2 * MLA_RANK + MLA_ROPE_DIM
    cum = _fox_cum(u_c[:, fl_col:fl_col + HEADS], p['b_forget'])
    q_x, k_x = _fox_prep(u_b, cum)
    y_b = _causal_attention(q_x, k_x, u_b, 2 * FOX_WIDTH, "fox_attention")

    q_f, k_f, v_c = _mla_prep(u_c, p['g_cq'].reshape(1, MLA_RANK),
                              p['g_ckv'].reshape(1, MLA_RANK), _prep_uq(p['w_uq']),
                              p['w_ukv'].astype(BF16), tabs)
    y_c = _causal_attention(q_f, k_f, v_c, 0, "mla_attention")

    mixed = _merge(y_a, y_b, y_c, p['w_proj_lru_all'], p['w_proj_fox_all'], p['w_proj_mla_all'],
                   layer, u_a, p['b_merge'])
    x = _outproj_ln(mixed, p['w_out_all'], layer, x,
                    p['ln1_g'].reshape(1, D_MODEL), p['ln1_b'].reshape(1, D_MODEL))
    top_idx, gate, rank, counts = _router(x, p['w_router'].T, p['b_router'])
    dest, tok_list, tok_row0, block_expert, n_used = _dispatch_plan(top_idx, rank, counts)
    ordinal, nxt = _expert_schedule(block_expert, counts, n_used)
    base = p['expert_base']
    sched = (block_expert + base, ordinal, jnp.where(nxt >= 0, nxt + base, -1), n_used)
    acts = _expert_up(x, tok_list, tok_row0, sched, p['w_up_all'], p['b_up_all'])
    y_rows = _expert_down(acts, sched, p['w_down_all'], p['b_down_all'])
    del t
    return _combine_ln(y_rows, dest, gate, x, p['ln2_g'].reshape(1, D_MODEL),
                       p['ln2_b'].reshape(1, D_MODEL))


_LAYER_PARAMS = ('w_in', 'b_merge', 'b_forget', 'conv_w', 'conv_b', 'w_rec_gate', 'b_rec_gate',
                 'w_inp_gate', 'b_inp_gate', 'lru_lambda', 'g_cq', 'g_ckv', 'w_uq', 'w_ukv',
                 'w_proj_lru', 'w_proj_fox', 'w_proj_mla', 'w_out', 'ln1_g', 'ln1_b',
                 'w_router', 'b_router', 'w_up', 'b_up', 'w_down', 'b_down', 'ln2_g', 'ln2_b')


def kernel(x, positions, w_in, b_merge, b_forget, conv_w, conv_b, w_rec_gate, b_rec_gate,
           w_inp_gate, b_inp_gate, lru_lambda, g_cq, g_ckv, w_uq, w_ukv, w_proj_lru,
           w_proj_fox, w_proj_mla, w_out, ln1_g, ln1_b, w_router, b_router, w_up, b_up,
           w_down, b_down, ln2_g, ln2_b):
    stacked = dict(zip(_LAYER_PARAMS, (
        w_in, b_merge, b_forget, conv_w, conv_b, w_rec_gate, b_rec_gate, w_inp_gate,
        b_inp_gate, lru_lambda, g_cq, g_ckv, w_uq, w_ukv, w_proj_lru, w_proj_fox,
        w_proj_mla, w_out, ln1_g, ln1_b, w_router, b_router, w_up, b_up, w_down, b_down,
        ln2_g, ln2_b)))
    batch, seq, _ = x.shape
    assert batch == 1
    xt = x.reshape(seq, D_MODEL)
    x_bf = xt.astype(BF16)
    tabs = _rope_tables(positions.reshape(seq))
    expert_stack = {'w_up', 'b_up', 'w_down', 'b_down'}
    shared = {k + '_all': stacked[k].reshape((-1,) + stacked[k].shape[2:]) for k in expert_stack}
    dense_stack = {'w_in', 'w_proj_lru', 'w_proj_fox', 'w_proj_mla', 'w_out'}
    shared.update({k + '_all': stacked[k] for k in dense_stack})
    in_place = expert_stack | dense_stack
    for l in range(w_in.shape[0]):
        p = {k: v[l] for k, v in stacked.items() if k not in in_place}
        xt, x_bf = _layer(xt, x_bf, tabs,
                          dict(p, layer=l, expert_base=l * N_EXPERTS, **shared))
    return xt.reshape(batch, seq, D_MODEL)
```
